```python
import jax, jax.numpy as jnp
from jax import lax
import numpy as np

D_MODEL = 1024
BATCH = 4
SEQ = 8192
DEPTH = 4

GRID_W = 64
HEAD_DIM = 64
N_HEADS_A = 8
WIN_ROWS_MAX = 8
WIN_COLS = 16
WIDTH_A = N_HEADS_A * HEAD_DIM
N_HEADS_B = 4
WIDTH_B = N_HEADS_B * 2 * HEAD_DIM
Q_BLOCK = 128
D_IN_AB = 3 * WIDTH_A + 3 * WIDTH_B
D_MIX_AB = WIDTH_A + WIDTH_B
CHUNK = 128
GMLP_WIDTH = D_MODEL
GMLP_GROUPS = 8
N_EXPERTS = 16
EC_FACTOR = 2
D_EXPERT = 2 * D_MODEL
N_EVEN = (DEPTH + 1) // 2
N_ODD = DEPTH // 2
LN_EPS = 1e-5
RMS_EPS = 1e-6
DEEPNORM_ALPHA = (2.0 * DEPTH) ** 0.25
DEEPNORM_BETA = (8.0 * DEPTH) ** -0.25

kernel_name = "hybrid_natten_diffattn_gmlp_ecmoe_encoder"


def layer_norm(x, g, b):
    xf = x.astype(jnp.float32)
    mu = jnp.mean(xf, axis=-1, keepdims=True)
    var = jnp.mean(jnp.square(xf - mu), axis=-1, keepdims=True)
    y = (xf - mu) * lax.rsqrt(var + LN_EPS) * g.astype(jnp.float32) + b.astype(jnp.float32)
    return y.astype(x.dtype)


def alibi_slopes(n_heads):
    s = 2.0 ** (-8.0 * (np.arange(n_heads) + 1) / n_heads)
    return jnp.asarray(s, dtype=jnp.float32)


def lambda_init(layer_number):
    return 0.8 - 0.6 * float(np.exp(-0.3 * (layer_number - 1)))


def neighbourhood_attention(q, k, v, rpb):
    b, S, H, d = q.shape
    R = S // GRID_W
    kr = min(WIN_ROWS_MAX, R)

    def to_grid(t):
        return t.reshape(b, R, GRID_W, H, d).transpose(0, 3, 1, 2, 4)

    qg, kg, vg = to_grid(q), to_grid(k), to_grid(v)
    cols = np.arange(GRID_W)
    c_start = np.clip(cols - WIN_COLS // 2, 0, GRID_W - WIN_COLS)
    cidx = c_start[:, None] + np.arange(WIN_COLS)[None, :]
    dc = cidx - cols[:, None] + (WIN_COLS - 1)
    scale = d ** -0.5

    def row(r):
        rs = jnp.clip(r - kr // 2, 0, R - kr)
        q_r = lax.dynamic_index_in_dim(qg, r, axis=2, keepdims=False)
        k_rows = lax.dynamic_slice_in_dim(kg, rs, kr, axis=2)
        v_rows = lax.dynamic_slice_in_dim(vg, rs, kr, axis=2)
        k_nb = k_rows[:, :, :, cidx]
        v_nb = v_rows[:, :, :, cidx]
        dr = rs + jnp.arange(kr) - r + (WIN_ROWS_MAX - 1)
        bias = rpb[:, dr[:, None, None], dc[None, :, :]]
        bias = bias.transpose(0, 2, 1, 3).astype(jnp.float32)
        s = jnp.einsum('bhcd,bhicjd->bhcij', q_r, k_nb).astype(jnp.float32) * scale + bias[None]
        p = jax.nn.softmax(s.reshape(b, H, GRID_W, kr * WIN_COLS), axis=-1)
        p = p.reshape(b, H, GRID_W, kr, WIN_COLS).astype(v.dtype)
        return jnp.einsum('bhcij,bhicjd->bhcd', p, v_nb)

    out = lax.map(row, jnp.arange(R))
    return out.transpose(1, 0, 3, 2, 4).reshape(b, S, H * d)


def differential_attention(q, k, v, lam_qk, subln_g, lam_init):
    b, S, H, _, d = q.shape
    nblk = S // Q_BLOCK
    scale = d ** -0.5
    slopes = alibi_slopes(H)
    lf = lam_qk.astype(jnp.float32)
    lam = jnp.exp(jnp.sum(lf[0] * lf[1])) - jnp.exp(jnp.sum(lf[2] * lf[3])) + lam_init
    pos = jnp.arange(S)
    kt = k.transpose(0, 2, 3, 1, 4)
    vt = v.transpose(0, 2, 1, 3)
    qb = q.reshape(b, nblk, Q_BLOCK, H, 2, d).transpose(1, 0, 3, 4, 2, 5)

    def blk(args):
        q_blk, i = args
        s = jnp.einsum('bhmqd,bhmkd->bhmqk', q_blk, kt).astype(jnp.float32) * scale
        tq = i * Q_BLOCK + jnp.arange(Q_BLOCK)
        dist = jnp.abs(tq[:, None] - pos[None, :]).astype(jnp.float32)
        s = s - slopes[None, :, None, None, None] * dist[None, None, None]
        p = jax.nn.softmax(s, axis=-1)
        a = (p[:, :, 0] - lam * p[:, :, 1]).astype(v.dtype)
        return jnp.einsum('bhqk,bhkd->bhqd', a, vt)

    o = lax.map(blk, (qb, jnp.arange(nblk)))
    o = o.transpose(1, 0, 3, 2, 4).reshape(b, S, H, 2 * d)
    of = o.astype(jnp.float32)
    of = of * lax.rsqrt(jnp.mean(jnp.square(of), axis=-1, keepdims=True) + RMS_EPS)
    of = of * subln_g.astype(jnp.float32) * (1.0 - lam_init)
    return of.astype(v.dtype).reshape(b, S, H * 2 * d)


def parallel_attention_mixer(x, w_in, rpb, lam_qk, subln_g, w_out, lam_init):
    b, S, _ = x.shape
    qkv = x @ w_in
    qa, ka, va, qb, kb, vb = jnp.split(
        qkv, [WIDTH_A, 2 * WIDTH_A, 3 * WIDTH_A, 3 * WIDTH_A + WIDTH_B, 3 * WIDTH_A + 2 * WIDTH_B], axis=-1)
    sa = (b, S, N_HEADS_A, HEAD_DIM)
    sb = (b, S, N_HEADS_B, 2, HEAD_DIM)
    oa = neighbourhood_attention(qa.reshape(sa), ka.reshape(sa), va.reshape(sa), rpb)
    ob = differential_attention(qb.reshape(sb), kb.reshape(sb),
                                vb.reshape(b, S, N_HEADS_B, 2 * HEAD_DIM), lam_qk, subln_g, lam_init)
    return jnp.concatenate([oa, ob], axis=-1) @ w_out


def spatial_gating_mlp(x, w_in, ln_g, ln_b, w_s, b_s, w_out):
    b, S, _ = x.shape
    z = jax.nn.gelu(x @ w_in, approximate=False)
    u, v = jnp.split(z, 2, axis=-1)
    v = layer_norm(v, ln_g, ln_b)
    G = w_s.shape[0]
    v = v.reshape(b, S // CHUNK, CHUNK, G, GMLP_WIDTH // G)
    sv = jnp.einsum('gts,bnsgc->bntgc', w_s, v) + b_s.T[None, None, :, :, None]
    return (u * sv.reshape(b, S, GMLP_WIDTH)) @ w_out


def expert_choice_moe(x, w_router, w_gate, w_up, w_down):
    b, S, D = x.shape
    cap = EC_FACTOR * S // N_EXPERTS
    aff = jax.nn.softmax((x @ w_router).astype(jnp.float32), axis=-1)
    gate, idx = lax.top_k(aff.transpose(0, 2, 1), cap)
    bidx = jnp.arange(b)[:, None, None]
    xg = x[bidx, idx]
    h = jax.nn.silu(jnp.einsum('becd,edf->becf', xg, w_gate)) * jnp.einsum('becd,edf->becf', xg, w_up)
    y = jnp.einsum('becf,efd->becd', h, w_down) * gate[..., None].astype(x.dtype)
    return jnp.zeros_like(x).at[bidx, idx].add(y)


def setup_inputs(seed: int = 0) -> dict:
    key = jax.random.key(seed)
    ks = jax.random.split(key, 24)
    f32 = jnp.float32

    def nrm(k, shape, scale):
        return jax.random.normal(k, shape, f32) * scale

    beta = DEEPNORM_BETA
    return {
        "x": nrm(ks[0], (BATCH, SEQ, D_MODEL), 1.0),
        "w_in_ab": nrm(ks[1], (N_EVEN, D_MODEL, D_IN_AB), D_MODEL ** -0.5),
        "rpb_a": nrm(ks[2], (N_EVEN, N_HEADS_A, 2 * WIN_ROWS_MAX - 1, 2 * WIN_COLS - 1), 0.1),
        "lambda_qk": nrm(ks[3], (N_EVEN, 4, HEAD_DIM), 0.1),
        "subln_g": 1.0 + nrm(ks[4], (N_EVEN, N_HEADS_B, 2 * HEAD_DIM), 0.05),
        "w_out_ab": nrm(ks[5], (N_EVEN, D_MIX_AB, D_MODEL), beta * D_MIX_AB ** -0.5),
        "w_in_c": nrm(ks[6], (N_ODD, D_MODEL, 2 * GMLP_WIDTH), D_MODEL ** -0.5),
        "ln_v_g": 1.0 + nrm(ks[7], (N_ODD, GMLP_WIDTH), 0.05),
        "ln_v_b": nrm(ks[8], (N_ODD, GMLP_WIDTH), 0.02),
        "w_s": nrm(ks[9], (N_ODD, GMLP_GROUPS, CHUNK, CHUNK), CHUNK ** -0.5),
        "b_s": 1.0 + nrm(ks[10], (N_ODD, GMLP_GROUPS, CHUNK), 0.1),
        "w_out_c": nrm(ks[11], (N_ODD, GMLP_WIDTH, D_MODEL), beta * GMLP_WIDTH ** -0.5),
        "ln_mix_g": 1.0 + nrm(ks[12], (DEPTH, D_MODEL), 0.05),
        "ln_mix_b": nrm(ks[13], (DEPTH, D_MODEL), 0.02),
        "w_router": nrm(ks[14], (DEPTH, D_MODEL, N_EXPERTS), D_MODEL ** -0.5),
        "w_gate": nrm(ks[15], (DEPTH, N_EXPERTS, D_MODEL, D_EXPERT), D_MODEL ** -0.5),
        "w_up": nrm(ks[16], (DEPTH, N_EXPERTS, D_MODEL, D_EXPERT), D_MODEL ** -0.5),
        "w_down": nrm(ks[17], (DEPTH, N_EXPERTS, D_EXPERT, D_MODEL), beta * D_EXPERT ** -0.5),
        "ln_ffn_g": 1.0 + nrm(ks[18], (DEPTH, D_MODEL), 0.05),
        "ln_ffn_b": nrm(ks[19], (DEPTH, D_MODEL), 0.02),
    }


def reference(x, w_in_ab, rpb_a, lambda_qk, subln_g, w_out_ab, w_in_c, ln_v_g, ln_v_b, w_s, b_s,
              w_out_c, ln_mix_g, ln_mix_b, w_router, w_gate, w_up, w_down, ln_ffn_g, ln_ffn_b):
    for l in range(DEPTH):
        i = l // 2
        if l % 2 == 0:
            h = parallel_attention_mixer(x, w_in_ab[i], rpb_a[i], lambda_qk[i], subln_g[i],
                                         w_out_ab[i], lambda_init(l + 1))
        else:
            h = spatial_gating_mlp(x, w_in_c[i], ln_v_g[i], ln_v_b[i], w_s[i], b_s[i], w_out_c[i])
        x = layer_norm(DEEPNORM_ALPHA * x + h, ln_mix_g[l], ln_mix_b[l])
        h = expert_choice_moe(x, w_router[l], w_gate[l], w_up[l], w_down[l])
        x = layer_norm(DEEPNORM_ALPHA * x + h, ln_ffn_g[l], ln_ffn_b[l])
    return x
```

```python
import functools

import jax
import jax.numpy as jnp
import numpy as np
from jax import lax
from jax.experimental import pallas as pl
from jax.experimental.pallas import tpu as pltpu

F32 = jnp.float32
BF16 = jnp.bfloat16

DEPTH = 4
GRID_W = 64
HEAD_DIM = 64
N_HEADS_A = 8
WIN_ROWS = 8
WIN_COLS = 16
WIDTH_A = N_HEADS_A * HEAD_DIM
N_HEADS_B = 4
WIDTH_B = N_HEADS_B * 2 * HEAD_DIM
CHUNK = 128
GMLP_GROUPS = 8
N_EXPERTS = 16
EC_FACTOR = 2
LN_EPS = 1e-5
RMS_EPS = 1e-6
ALPHA = (2.0 * DEPTH) ** 0.25
QK_SCALE = HEAD_DIM ** -0.5
MASK_VALUE = -1e30

LANES = 128
VMEM_LIMIT = 56 * 1024 * 1024

NT_DIMS = (((1,), (1,)), ((), ()))


def _params(*sem):
    return pltpu.CompilerParams(dimension_semantics=sem, vmem_limit_bytes=VMEM_LIMIT)


def _layer_norm(y, g, b):
    mu = jnp.mean(y, axis=-1, keepdims=True)
    yc = y - mu
    var = jnp.mean(yc * yc, axis=-1, keepdims=True)
    return yc * lax.rsqrt(var + LN_EPS) * g + b


def _proj_kernel(x_ref, w_ref, s_ref, o_ref):
    acc = jnp.dot(x_ref[...].astype(BF16), w_ref[...], preferred_element_type=F32)
    o_ref[...] = (acc * s_ref[...]).astype(o_ref.dtype)


def _project(x2, w_bf16, col_scale, tm=1024, tn=1024):
    t, k = x2.shape
    n = w_bf16.shape[1]
    return pl.pallas_call(
        _proj_kernel,
        grid=(t // tm, n // tn),
        in_specs=[
            pl.BlockSpec((tm, k), lambda i, j: (i, 0)),
            pl.BlockSpec((k, tn), lambda i, j: (0, j)),
            pl.BlockSpec((1, tn), lambda i, j: (0, j)),
        ],
        out_specs=pl.BlockSpec((tm, tn), lambda i, j: (i, j)),
        out_shape=jax.ShapeDtypeStruct((t, n), BF16),
        compiler_params=_params("parallel", "arbitrary"),
        name="qkv_proj",
    )(x2, w_bf16, col_scale)


def _natten_bias(rpb):
    delta = np.arange(WIN_ROWS)[:, None]
    i = np.arange(WIN_ROWS)[None, :]
    ridx = i - delta + (WIN_ROWS - 1)
    c = np.arange(GRID_W)[:, None]
    kc = np.arange(GRID_W)[None, :]
    c_start = np.clip(c - WIN_COLS // 2, 0, GRID_W - WIN_COLS)
    valid = (kc >= c_start) & (kc < c_start + WIN_COLS)
    cidx = np.clip(kc - c + (WIN_COLS - 1), 0, 2 * WIN_COLS - 2)
    g = rpb.astype(F32)[:, ridx[:, :, None, None], cidx[None, None, :, :]]
    g = jnp.where(valid[None, None, None], g, MASK_VALUE)
    g = g.transpose(1, 0, 3, 2, 4)
    return g.reshape(WIN_ROWS, rpb.shape[0], GRID_W, WIN_ROWS * GRID_W)


def _natten_kernel(q_ref, k_ref, v_ref, bias_ref, o_ref, *, n_rows):
    lane = lax.broadcasted_iota(jnp.int32, (GRID_W, LANES), 1)
    low = lane < HEAD_DIM
    nkeys = WIN_ROWS * GRID_W

    def row(r, carry):
        rs = jnp.clip(r - WIN_ROWS // 2, 0, n_rows - WIN_ROWS)
        delta = r - rs
        q = q_ref[pl.ds(pl.multiple_of(r * GRID_W, GRID_W), GRID_W), :]
        k = k_ref[pl.ds(pl.multiple_of(rs * GRID_W, GRID_W), nkeys), :]
        v = v_ref[pl.ds(pl.multiple_of(rs * GRID_W, GRID_W), nkeys), :]
        outs = []
        for hh in range(2):
            qm = jnp.where(low if hh == 0 else jnp.logical_not(low), q, jnp.zeros_like(q))
            s = lax.dot_general(qm, k, NT_DIMS, preferred_element_type=F32)
            s = s + bias_ref[delta, hh]
            m = jnp.max(s, axis=-1, keepdims=True)
            p = jnp.exp(s - m)
            l = jnp.sum(p, axis=-1, keepdims=True)
            o = jnp.dot(p.astype(BF16), v, preferred_element_type=F32)
            outs.append(o / l)
        out = jnp.where(low, outs[0], outs[1])
        o_ref[pl.ds(pl.multiple_of(r * GRID_W, GRID_W), GRID_W), :] = out.astype(o_ref.dtype)
        return carry

    lax.fori_loop(0, n_rows, row, 0)


def _natten(qkv, bias, batch, seq):
    n_rows = seq // GRID_W
    hp = N_HEADS_A // 2
    blk = lambda off: pl.BlockSpec((seq, LANES), lambda b, h: (b, off + h))
    return pl.pallas_call(
        functools.partial(_natten_kernel, n_rows=n_rows),
        grid=(batch, hp),
        in_specs=[
            blk(0), blk(hp), blk(2 * hp),
            pl.BlockSpec((WIN_ROWS, 2, GRID_W, WIN_ROWS * GRID_W), lambda b, h: (0, h, 0, 0)),
        ],
        out_specs=pl.BlockSpec((seq, LANES), lambda b, h: (b, h)),
        out_shape=jax.ShapeDtypeStruct((batch * seq, WIDTH_A), BF16),
        compiler_params=_params("parallel", "parallel"),
        name="natten",
    )(qkv, qkv, qkv, bias)


def _diff_kernel(slopes_ref, lam_ref, q_ref, k_ref, v_ref, g_ref, o_ref,
                 m_ref, l_ref, acc_ref, *, tq, tk, out_scale):
    h = pl.program_id(1)
    qi = pl.program_id(2)
    ki = pl.program_id(3)

    @pl.when(ki == 0)
    def _():
        m_ref[...] = jnp.full_like(m_ref, -jnp.inf)
        l_ref[...] = jnp.zeros_like(l_ref)
        acc_ref[...] = jnp.zeros_like(acc_ref)

    q = q_ref[...]
    k = k_ref[...]
    v = v_ref[...]
    lane = lax.broadcasted_iota(jnp.int32, q.shape, 1)
    low = lane < HEAD_DIM
    ti = qi * tq + lax.broadcasted_iota(jnp.int32, (tq, tk), 0)
    tj = ki * tk + lax.broadcasted_iota(jnp.int32, (tq, tk), 1)
    pen = slopes_ref[h] * jnp.abs(ti - tj).astype(F32)

    for mi in range(2):
        qm = jnp.where(low if mi == 0 else jnp.logical_not(low), q, jnp.zeros_like(q))
        s = lax.dot_general(qm, k, NT_DIMS, preferred_element_type=F32) - pen
        m_prev = m_ref[mi]
        m_new = jnp.maximum(m_prev, jnp.max(s, axis=-1, keepdims=True))
        a = jnp.exp(m_prev - m_new)
        p = jnp.exp(s - m_new)
        l_ref[mi] = a * l_ref[mi] + jnp.sum(p, axis=-1, keepdims=True)
        acc_ref[mi] = a * acc_ref[mi] + jnp.dot(p.astype(BF16), v, preferred_element_type=F32)
        m_ref[mi] = m_new

    @pl.when(ki == pl.num_programs(3) - 1)
    def _():
        o = acc_ref[0] / l_ref[0] - lam_ref[0] * (acc_ref[1] / l_ref[1])
        o = o * lax.rsqrt(jnp.mean(o * o, axis=-1, keepdims=True) + RMS_EPS)
        o_ref[...] = (o * g_ref[0] * out_scale).astype(o_ref.dtype)


def _diff_attention(qkv, slopes, lam, subln_g, lam_init, batch, seq, tq=1024, tk=512):
    nq, nk = seq // tq, seq // tk
    base = 3 * WIDTH_A // LANES
    hb = N_HEADS_B
    smem = pl.BlockSpec(memory_space=pltpu.SMEM)
    return pl.pallas_call(
        functools.partial(_diff_kernel, tq=tq, tk=tk, out_scale=1.0 - lam_init),
        grid=(batch, hb, nq, nk),
        in_specs=[
            smem, smem,
            pl.BlockSpec((tq, LANES), lambda b, h, i, j: (b * nq + i, base + h)),
            pl.BlockSpec((tk, LANES), lambda b, h, i, j: (b * nk + j, base + hb + h)),
            pl.BlockSpec((tk, LANES), lambda b, h, i, j: (b * nk + j, base + 2 * hb + h)),
            pl.BlockSpec((1, 1, LANES), lambda b, h, i, j: (h, 0, 0)),
        ],
        out_specs=pl.BlockSpec((tq, LANES), lambda b, h, i, j: (b * nq + i, h)),
        out_shape=jax.ShapeDtypeStruct((batch * seq, WIDTH_B), BF16),
        scratch_shapes=[
            pltpu.VMEM((2, tq, 1), F32),
            pltpu.VMEM((2, tq, 1), F32),
            pltpu.VMEM((2, tq, LANES), F32),
        ],
        compiler_params=_params("parallel", "parallel", "parallel", "arbitrary"),
        name="diff_attn",
    )(slopes, lam, qkv, qkv, qkv, subln_g.astype(F32).reshape(hb, 1, LANES))


def _attn_out_kernel(oa_ref, ob_ref, wa_ref, wb_ref, x_ref, g_ref, b_ref, o_ref):
    h = jnp.dot(oa_ref[...], wa_ref[...], preferred_element_type=F32)
    h = h + jnp.dot(ob_ref[...], wb_ref[...], preferred_element_type=F32)
    o_ref[...] = _layer_norm(ALPHA * x_ref[...] + h, g_ref[...], b_ref[...])


def _attn_out(oa, ob, w_out_bf16, x2, g, b, tm=512):
    t, d = x2.shape
    wa, wb = w_out_bf16[:WIDTH_A], w_out_bf16[WIDTH_A:]
    row = lambda w: pl.BlockSpec((tm, w), lambda i: (i, 0))
    full = lambda a: pl.BlockSpec(a.shape, lambda i: (0,) * a.ndim)
    return pl.pallas_call(
        _attn_out_kernel,
        grid=(t // tm,),
        in_specs=[row(WIDTH_A), row(WIDTH_B), full(wa), full(wb), row(d), full(g), full(b)],
        out_specs=row(d),
        out_shape=jax.ShapeDtypeStruct((t, d), F32),
        compiler_params=_params("parallel"),
        name="attn_out_ln",
    )(oa, ob, wa, wb, x2, g, b)


def _gmlp_kernel(x_ref, win_ref, lg_ref, lb_ref, ws_ref, bs_ref, wout_ref, g_ref, b_ref, o_ref, *, tm):
    x = x_ref[...]
    width = wout_ref.shape[0]
    gw = width // GMLP_GROUPS
    z = jnp.dot(x.astype(BF16), win_ref[...], preferred_element_type=F32)
    z = 0.5 * z * (1.0 + lax.erf(z * (2.0 ** -0.5)))
    u = z[:, :width]
    v = _layer_norm(z[:, width:], lg_ref[...], lb_ref[...]).astype(BF16)
    chunks = []
    for n in range(tm // CHUNK):
        groups = []
        for gi in range(GMLP_GROUPS):
            vg = v[n * CHUNK:(n + 1) * CHUNK, gi * gw:(gi + 1) * gw]
            groups.append(jnp.dot(ws_ref[gi], vg, preferred_element_type=F32))
        chunks.append(jnp.concatenate(groups, axis=1) + bs_ref[...])
    sv = jnp.concatenate(chunks, axis=0)
    h = jnp.dot((u * sv).astype(BF16), wout_ref[...], preferred_element_type=F32)
    o_ref[...] = _layer_norm(ALPHA * x + h, g_ref[...], b_ref[...])


def _gmlp_layer(x2, w_in, ln_g, ln_b, w_s, bs_full, w_out, g, b, tm=256):
    t, d = x2.shape
    row = pl.BlockSpec((tm, d), lambda i: (i, 0))
    full = lambda a: pl.BlockSpec(a.shape, lambda i: (0,) * a.ndim)
    args = (w_in, ln_g, ln_b, w_s, bs_full, w_out, g, b)
    return pl.pallas_call(
        functools.partial(_gmlp_kernel, tm=tm),
        grid=(t // tm,),
        in_specs=[row] + [full(a) for a in args],
        out_specs=row,
        out_shape=jax.ShapeDtypeStruct((t, d), F32),
        compiler_params=_params("parallel"),
        name="gmlp_layer",
    )(x2, *args)


def _router_kernel(x_ref, wr_ref, o_ref):
    logits = lax.dot_general(wr_ref[...], x_ref[...], NT_DIMS, preferred_element_type=F32,
                             precision=lax.Precision.HIGHEST)
    m = jnp.max(logits, axis=0, keepdims=True)
    p = jnp.exp(logits - m)
    o_ref[0] = p / jnp.sum(p, axis=0, keepdims=True)


def _router(x2, w_router_t, batch, seq, tm=1024):
    d = x2.shape[1]
    ns = seq // tm
    return pl.pallas_call(
        _router_kernel,
        grid=(batch, ns),
        in_specs=[
            pl.BlockSpec((tm, d), lambda b, i: (b * ns + i, 0)),
            pl.BlockSpec((N_EXPERTS, d), lambda b, i: (0, 0)),
        ],
        out_specs=pl.BlockSpec((1, N_EXPERTS, tm), lambda b, i: (b, 0, i)),
        out_shape=jax.ShapeDtypeStruct((batch, N_EXPERTS, seq), F32),
        compiler_params=_params("parallel", "parallel"),
        name="router",
    )(x2, w_router_t)


def _expert_kernel(x_ref, wg_ref, wu_ref, wd_ref, gate_ref, o_ref):
    f = pl.program_id(2)
    x = x_ref[0, 0]
    a = jnp.dot(x, wg_ref[0], preferred_element_type=F32)
    u = jnp.dot(x, wu_ref[0], preferred_element_type=F32)
    hmid = (a * jax.nn.sigmoid(a) * u).astype(BF16)
    y = jnp.dot(hmid, wd_ref[0], preferred_element_type=F32)

    @pl.when(f == 0)
    def _():
        o_ref[0, 0] = y

    @pl.when(f > 0)
    def _():
        o_ref[0, 0] += y

    @pl.when(f == pl.num_programs(2) - 1)
    def _():
        o_ref[0, 0] = o_ref[0, 0] * gate_ref[0, 0]


def _experts(xg, wg, wu, wd, gate, tf=512):
    batch, ne, cap, d = xg.shape
    dff = wg.shape[2]
    return pl.pallas_call(
        _expert_kernel,
        grid=(ne, batch, dff // tf),
        in_specs=[
            pl.BlockSpec((1, 1, cap, d), lambda e, b, f: (b, e, 0, 0)),
            pl.BlockSpec((1, d, tf), lambda e, b, f: (e, 0, f)),
            pl.BlockSpec((1, d, tf), lambda e, b, f: (e, 0, f)),
            pl.BlockSpec((1, tf, d), lambda e, b, f: (e, f, 0)),
            pl.BlockSpec((1, 1, cap, 1), lambda e, b, f: (b, e, 0, 0)),
        ],
        out_specs=pl.BlockSpec((1, 1, cap, d), lambda e, b, f: (b, e, 0, 0)),
        out_shape=jax.ShapeDtypeStruct((batch, ne, cap, d), F32),
        compiler_params=_params("parallel", "parallel", "arbitrary"),
        name="experts",
    )(xg, wg, wu, wd, gate)


def _res_ln_kernel(x_ref, h_ref, g_ref, b_ref, o_ref):
    o_ref[...] = _layer_norm(ALPHA * x_ref[...] + h_ref[...], g_ref[...], b_ref[...])


def _res_ln(x2, h2, g, b, tm=1024):
    t, d = x2.shape
    row = pl.BlockSpec((tm, d), lambda i: (i, 0))
    vec = pl.BlockSpec((1, d), lambda i: (0, 0))
    return pl.pallas_call(
        _res_ln_kernel,
        grid=(t // tm,),
        in_specs=[row, row, vec, vec],
        out_specs=row,
        out_shape=jax.ShapeDtypeStruct((t, d), F32),
        compiler_params=_params("parallel"),
        name="res_ln",
    )(x2, h2, g, b)


def _moe_layer(x2, w_router, wg, wu, wd, g, b, batch, seq):
    d = x2.shape[1]
    cap = EC_FACTOR * seq // N_EXPERTS
    aff_t = _router(x2, w_router.astype(F32).T, batch, seq)
    gate, idx = lax.top_k(aff_t, cap)
    x3 = x2.reshape(batch, seq, d)
    bidx = jnp.arange(batch)[:, None, None]
    xg = x3[bidx, idx].astype(BF16)
    y = _experts(xg, wg, wu, wd, gate[..., None])
    h = jnp.zeros_like(x3).at[bidx, idx].add(y)
    return _res_ln(x2, h.reshape(batch * seq, d), g, b)


def _lambda_init(layer_number):
    return 0.8 - 0.6 * float(np.exp(-0.3 * (layer_number - 1)))


def _row(v):
    return v.astype(F32).reshape(1, -1)


def kernel(x, w_in_ab, rpb_a, lambda_qk, subln_g, w_out_ab, w_in_c, ln_v_g, ln_v_b, w_s, b_s,
           w_out_c, ln_mix_g, ln_mix_b, w_router, w_gate, w_up, w_down, ln_ffn_g, ln_ffn_b):
    batch, seq, d = x.shape
    x2 = x.astype(F32).reshape(batch * seq, d)
    slopes = jnp.asarray(2.0 ** (-8.0 * (np.arange(N_HEADS_B) + 1) / N_HEADS_B), F32)
    col_scale = np.ones((1, 3 * WIDTH_A + 3 * WIDTH_B), np.float32)
    col_scale[:, :WIDTH_A] = QK_SCALE
    col_scale[:, 3 * WIDTH_A:3 * WIDTH_A + WIDTH_B] = QK_SCALE
    col_scale = jnp.asarray(col_scale)

    for l in range(DEPTH):
        i = l // 2
        if l % 2 == 0:
            lam_init = _lambda_init(l + 1)
            lf = lambda_qk[i].astype(F32)
            lam = jnp.exp(jnp.sum(lf[0] * lf[1])) - jnp.exp(jnp.sum(lf[2] * lf[3])) + lam_init
            qkv = _project(x2, w_in_ab[i].astype(BF16), col_scale)
            oa = _natten(qkv, _natten_bias(rpb_a[i]), batch, seq)
            ob = _diff_attention(qkv, slopes, lam.reshape(1), subln_g[i], lam_init, batch, seq)
            x2 = _attn_out(oa, ob, w_out_ab[i].astype(BF16), x2, _row(ln_mix_g[l]), _row(ln_mix_b[l]))
        else:
            gw = w_in_c.shape[2] // 2 // GMLP_GROUPS
            bs_full = jnp.repeat(b_s[i].astype(F32).T, gw, axis=1)
            x2 = _gmlp_layer(x2, w_in_c[i].astype(BF16), _row(ln_v_g[i]), _row(ln_v_b[i]),
                             w_s[i].astype(BF16), bs_full, w_out_c[i].astype(BF16),
                             _row(ln_mix_g[l]), _row(ln_mix_b[l]))
        x2 = _moe_layer(x2, w_router[l], w_gate[l].astype(BF16), w_up[l].astype(BF16),
                        w_down[l].astype(BF16), _row(ln_ffn_g[l]), _row(ln_ffn_b[l]), batch, seq)
    return x2.reshape(batch, seq, d).astype(x.dtype)
```

```python
import functools

import jax
import jax.numpy as jnp
import numpy as np
from jax import lax
from jax.experimental import pallas as pl
from jax.experimental.pallas import tpu as pltpu

F32 = jnp.float32
BF16 = jnp.bfloat16

DEPTH = 4
GRID_W = 64
HEAD_DIM = 64
N_HEADS_A = 8
WIN_ROWS = 8
WIN_COLS = 16
WIDTH_A = N_HEADS_A * HEAD_DIM
N_HEADS_B = 4
WIDTH_B = N_HEADS_B * 2 * HEAD_DIM
CHUNK = 128
GMLP_GROUPS = 8
N_EXPERTS = 16
EC_FACTOR = 2
LN_EPS = 1e-5
RMS_EPS = 1e-6
ALPHA = (2.0 * DEPTH) ** 0.25
QK_SCALE = HEAD_DIM ** -0.5
MASK_VALUE = -1e30
LOG2E = float(np.log2(np.e))

LANES = 128
VMEM_LIMIT = 56 * 1024 * 1024

NT_DIMS = (((1,), (1,)), ((), ()))


def _params(*sem):
    return pltpu.CompilerParams(dimension_semantics=sem, vmem_limit_bytes=VMEM_LIMIT)


def _layer_norm(y, g, b):
    mu = jnp.mean(y, axis=-1, keepdims=True)
    yc = y - mu
    var = jnp.mean(yc * yc, axis=-1, keepdims=True)
    return yc * lax.rsqrt(var + LN_EPS) * g + b


def _bf16_split(value, parts=3):
    out, rest = [], np.float64(value)
    for _ in range(parts):
        piece = float(np.asarray(rest, np.float32).astype(jnp.bfloat16).astype(np.float64))
        out.append(piece)
        rest = rest - piece
    return out


def _proj_kernel(x_ref, w_ref, s_ref, o_ref):
    acc = jnp.dot(x_ref[...].astype(BF16), w_ref[...], preferred_element_type=F32)
    o_ref[...] = (acc * s_ref[...]).astype(o_ref.dtype)


def _project(x2, w_bf16, col_scale, tm=1024, tn=512):
    t, k = x2.shape
    n = w_bf16.shape[1]
    return pl.pallas_call(
        _proj_kernel,
        grid=(t // tm, n // tn),
        in_specs=[
            pl.BlockSpec((tm, k), lambda i, j: (i, 0)),
            pl.BlockSpec((k, tn), lambda i, j: (0, j)),
            pl.BlockSpec((1, tn), lambda i, j: (0, j)),
        ],
        out_specs=pl.BlockSpec((tm, tn), lambda i, j: (i, j)),
        out_shape=jax.ShapeDtypeStruct((t, n), BF16),
        compiler_params=_params("parallel", "arbitrary"),
        name="qkv_proj",
    )(x2, w_bf16, col_scale)


def _proj_t_kernel(x_ref, wt_ref, o_ref):
    acc = lax.dot_general(wt_ref[...], x_ref[...].astype(BF16), NT_DIMS, preferred_element_type=F32)
    o_ref[...] = acc.astype(o_ref.dtype)


def _project_t(x2, wt_bf16, tm=1024):
    t, k = x2.shape
    n = wt_bf16.shape[0]
    return pl.pallas_call(
        _proj_t_kernel,
        grid=(t // tm,),
        in_specs=[
            pl.BlockSpec((tm, k), lambda i: (i, 0)),
            pl.BlockSpec((n, k), lambda i: (0, 0)),
        ],
        out_specs=pl.BlockSpec((n, tm), lambda i: (0, i)),
        out_shape=jax.ShapeDtypeStruct((n, t), BF16),
        compiler_params=_params("parallel"),
        name="v_proj_t",
    )(x2, wt_bf16)


def _natten_bias(rpb):
    n_heads = rpb.shape[0]
    span = 2 * WIN_COLS - 1
    pad_lo = GRID_W - WIN_COLS
    padded = jnp.pad(rpb.astype(F32), ((0, 0), (0, 0), (pad_lo, 2 * GRID_W - 1 - pad_lo - span)))
    toep = jnp.stack([padded[:, :, GRID_W - 1 - c:2 * GRID_W - 1 - c] for c in range(GRID_W)], axis=2)
    c = np.arange(GRID_W)[:, None]
    kc = np.arange(GRID_W)[None, :]
    c_start = np.clip(c - WIN_COLS // 2, 0, GRID_W - WIN_COLS)
    valid = (kc >= c_start) & (kc < c_start + WIN_COLS)
    toep = jnp.where(valid[None, None], toep, MASK_VALUE)
    per_delta = [toep[:, WIN_ROWS - 1 - dl:2 * WIN_ROWS - 1 - dl] for dl in range(WIN_ROWS)]
    g = jnp.stack(per_delta, axis=0)
    g = g.transpose(0, 1, 3, 2, 4)
    return g.reshape(WIN_ROWS, n_heads, GRID_W, WIN_ROWS * GRID_W)


def _natten_kernel(q_ref, k_ref, v_ref, bias_ref, o_ref, *, n_rows):
    lane = lax.broadcasted_iota(jnp.int32, (GRID_W, LANES), 1)
    low = lane < HEAD_DIM
    nkeys = WIN_ROWS * GRID_W

    def row(r, carry):
        rs = jnp.clip(r - WIN_ROWS // 2, 0, n_rows - WIN_ROWS)
        delta = r - rs
        q = q_ref[pl.ds(pl.multiple_of(r * GRID_W, GRID_W), GRID_W), :]
        k = k_ref[pl.ds(pl.multiple_of(rs * GRID_W, GRID_W), nkeys), :]
        v = v_ref[pl.ds(pl.multiple_of(rs * GRID_W, GRID_W), nkeys), :]
        outs = []
        for hh in range(2):
            qm = jnp.where(low if hh == 0 else jnp.logical_not(low), q, jnp.zeros_like(q))
            s = lax.dot_general(qm, k, NT_DIMS, preferred_element_type=F32)
            s = s + bias_ref[delta, hh]
            m = jnp.max(s, axis=-1, keepdims=True)
            p = jnp.exp(s - m)
            l = jnp.sum(p, axis=-1, keepdims=True)
            o = jnp.dot(p.astype(BF16), v, preferred_element_type=F32)
            outs.append(o / l)
        out = jnp.where(low, outs[0], outs[1])
        o_ref[pl.ds(pl.multiple_of(r * GRID_W, GRID_W), GRID_W), :] = out.astype(o_ref.dtype)
        return carry

    lax.fori_loop(0, n_rows, row, 0)


def _natten(qkv, bias, batch, seq):
    n_rows = seq // GRID_W
    hp = N_HEADS_A // 2
    blk = lambda off: pl.BlockSpec((seq, LANES), lambda b, h: (b, off + h))
    return pl.pallas_call(
        functools.partial(_natten_kernel, n_rows=n_rows),
        grid=(batch, hp),
        in_specs=[
            blk(0), blk(hp), blk(2 * hp),
            pl.BlockSpec((WIN_ROWS, 2, GRID_W, WIN_ROWS * GRID_W), lambda b, h: (0, h, 0, 0)),
        ],
        out_specs=pl.BlockSpec((seq, LANES), lambda b, h: (b, h)),
        out_shape=jax.ShapeDtypeStruct((batch * seq, WIDTH_A), BF16),
        compiler_params=_params("parallel", "parallel"),
        name="natten",
    )(qkv, qkv, qkv, bias)


def _diff_kernel(sl_ref, lam_ref, q_ref, k_ref, vt_ref, f_ref, pen_ref, g_ref, o_ref,
                 qop_ref, raw_ref, cmax_ref, coff_ref, acc_ref, *, tq, tk, out_scale):
    h = pl.program_id(1)
    qi = pl.program_id(2)
    n_kt = k_ref.shape[0] // tk
    sl = sl_ref[h]
    a1, a2, a3 = _bf16_split(LOG2E)

    q = q_ref[...]
    lane_q = lax.broadcasted_iota(jnp.int32, q.shape, 1)
    low_q = lane_q < HEAD_DIM
    zero = jnp.zeros_like(q)
    q1 = jnp.where(low_q, q, zero)
    q2 = jnp.where(low_q, zero, q)

    def q_ops(sign):
        if sign == 0:
            f = zero
        else:
            f = jnp.where(lane_q == 0, sign * a1,
                          jnp.where(lane_q == 1, sign * a2, jnp.where(lane_q == 2, sign * a3, 0.0))).astype(BF16)
        return jnp.concatenate([jnp.concatenate([q1, f], axis=1), jnp.concatenate([q2, f], axis=1)], axis=0)

    qop_ref[0] = q_ops(1.0)
    qop_ref[1] = q_ops(-1.0)
    fk = f_ref[0]
    pos = lax.broadcasted_iota(jnp.int32, (1, 2 * tq), 1)
    iq = (qi * tq + jnp.where(pos < tq, pos, pos - tq)).astype(F32)

    kt_mix = (qi * tq) // tk
    sel = qi - kt_mix * (tk // tq)

    def load_k(kt):
        off = pl.multiple_of(kt * tk, tk)
        return jnp.concatenate([k_ref[pl.ds(off, tk), :], fk], axis=1), vt_ref[:, pl.ds(off, tk)]

    def tile_of(s):
        t = s - 1
        return jnp.where(s == 0, kt_mix, t + (t >= kt_mix).astype(jnp.int32))

    def scores(s, slot):
        kt = tile_of(s)
        kx, _ = load_k(kt)
        raw = lax.dot_general(kx, qop_ref[(kt > kt_mix).astype(jnp.int32)], NT_DIMS, preferred_element_type=F32)
        c = -sl * jnp.abs(iq - (kt * tk + tk // 2).astype(F32))
        raw_ref[slot] = raw
        cmax_ref[slot] = jnp.max(raw, axis=0, keepdims=True) + c
        coff_ref[slot] = c

    def absorb(s, slot, state):
        m_prev, l_prev = state
        _, vt = load_k(tile_of(s))
        m_new = jnp.maximum(m_prev, cmax_ref[slot])
        a = jnp.exp2(m_prev - m_new)
        p = jnp.exp2(raw_ref[slot] - (m_new - coff_ref[slot]))
        l_new = a * l_prev + jnp.sum(p, axis=0, keepdims=True)
        acc_ref[...] = a * acc_ref[...] + jnp.dot(vt, p.astype(BF16), preferred_element_type=F32)
        return m_new, l_new

    acc_ref[...] = jnp.zeros_like(acc_ref)
    state = (jnp.full((1, 2 * tq), -jnp.inf, F32), jnp.zeros((1, 2 * tq), F32))

    kx, _ = load_k(kt_mix)
    pen = pen_ref[0, sel]
    raw = lax.dot_general(kx, q_ops(0), NT_DIMS, preferred_element_type=F32) - jnp.concatenate([pen, pen], axis=1)
    raw_ref[0] = raw
    cmax_ref[0] = jnp.max(raw, axis=0, keepdims=True)
    coff_ref[0] = jnp.zeros((1, 2 * tq), F32)

    def body(u, st):
        s = 2 * u
        scores(s + 1, 1)
        st = absorb(s, 0, st)
        scores(s + 2, 0)
        return absorb(s + 1, 1, st)

    assert n_kt % 2 == 0
    state = lax.fori_loop(0, n_kt // 2 - 1, body, state)
    scores(n_kt - 1, 1)
    state = absorb(n_kt - 2, 0, state)
    _, l = absorb(n_kt - 1, 1, state)

    on = acc_ref[...] / l
    o = (on[:, :tq] - lam_ref[0] * on[:, tq:]).T
    o = o * lax.rsqrt(jnp.mean(o * o, axis=-1, keepdims=True) + RMS_EPS)
    o_ref[...] = (o * g_ref[0] * out_scale).astype(o_ref.dtype)


def _diff_tables(slopes, tq, tk):
    j = np.arange(tk)
    feat = slopes[:, None] * (j - tk // 2)[None, :]
    feat = np.broadcast_to(feat[:, :, None], (len(slopes), tk, LANES))
    i = np.arange(tq)
    pos = np.arange(tk // tq)[:, None, None] * tq + i[None, None, :]
    dist = np.abs(pos - j[None, :, None])
    pen = (slopes * LOG2E)[:, None, None, None] * dist[None]
    return jnp.asarray(feat, BF16), jnp.asarray(pen, F32)


def _diff_attention(qk, vt, lam, subln_g, lam_init, batch, seq, tq=256, tk=512):
    nq = seq // tq
    base = 3 * WIDTH_A // LANES
    hb = N_HEADS_B
    slopes = 2.0 ** (-8.0 * (np.arange(hb) + 1) / hb)
    feat, pen = _diff_tables(slopes, tq, tk)
    sl = jnp.asarray(slopes * LOG2E, F32)
    smem = pl.BlockSpec(memory_space=pltpu.SMEM)
    return pl.pallas_call(
        functools.partial(_diff_kernel, tq=tq, tk=tk, out_scale=1.0 - lam_init),
        grid=(batch, hb, nq),
        in_specs=[
            smem, smem,
            pl.BlockSpec((tq, LANES), lambda b, h, i: (b * nq + i, base + h)),
            pl.BlockSpec((seq, LANES), lambda b, h, i: (b, base + hb + h)),
            pl.BlockSpec((LANES, seq), lambda b, h, i: (h, b)),
            pl.BlockSpec((1, tk, LANES), lambda b, h, i: (h, 0, 0)),
            pl.BlockSpec((1, tk // tq, tk, tq), lambda b, h, i: (h, 0, 0, 0)),
            pl.BlockSpec((1, 1, LANES), lambda b, h, i: (h, 0, 0)),
        ],
        out_specs=pl.BlockSpec((tq, LANES), lambda b, h, i: (b * nq + i, h)),
        out_shape=jax.ShapeDtypeStruct((batch * seq, WIDTH_B), BF16),
        scratch_shapes=[
            pltpu.VMEM((2, 2 * tq, 2 * LANES), BF16),
            pltpu.VMEM((2, tk, 2 * tq), F32),
            pltpu.VMEM((2, 1, 2 * tq), F32),
            pltpu.VMEM((2, 1, 2 * tq), F32),
            pltpu.VMEM((LANES, 2 * tq), F32),
        ],
        compiler_params=_params("parallel", "parallel", "parallel"),
        name="diff_attn",
    )(sl, lam, qk, qk, vt, feat, pen, subln_g.astype(F32).reshape(hb, 1, LANES))


def _attn_out_kernel(oa_ref, ob_ref, wa_ref, wb_ref, x_ref, g_ref, b_ref, o_ref):
    h = jnp.dot(oa_ref[...], wa_ref[...], preferred_element_type=F32)
    h = h + jnp.dot(ob_ref[...], wb_ref[...], preferred_element_type=F32)
    o_ref[...] = _layer_norm(ALPHA * x_ref[...] + h, g_ref[...], b_ref[...])


def _attn_out(oa, ob, w_out_bf16, x2, g, b, tm=512):
    t, d = x2.shape
    wa, wb = w_out_bf16[:WIDTH_A], w_out_bf16[WIDTH_A:]
    row = lambda w: pl.BlockSpec((tm, w), lambda i: (i, 0))
    full = lambda a: pl.BlockSpec(a.shape, lambda i: (0,) * a.ndim)
    return pl.pallas_call(
        _attn_out_kernel,
        grid=(t // tm,),
        in_specs=[row(WIDTH_A), row(WIDTH_B), full(wa), full(wb), row(d), full(g), full(b)],
        out_specs=row(d),
        out_shape=jax.ShapeDtypeStruct((t, d), F32),
        compiler_params=_params("parallel"),
        name="attn_out_ln",
    )(oa, ob, wa, wb, x2, g, b)


def _gmlp_kernel(x_ref, win_ref, lg_ref, lb_ref, ws_ref, bs_ref, wout_ref, g_ref, b_ref, o_ref, *, tm):
    x = x_ref[...]
    width = wout_ref.shape[0]
    gw = width // GMLP_GROUPS
    z = jnp.dot(x.astype(BF16), win_ref[...], preferred_element_type=F32)
    z = 0.5 * z * (1.0 + lax.erf(z * (2.0 ** -0.5)))
    u = z[:, :width]
    v = _layer_norm(z[:, width:], lg_ref[...], lb_ref[...]).astype(BF16)
    chunks = []
    for n in range(tm // CHUNK):
        groups = []
        for gi in range(GMLP_GROUPS):
            vg = v[n * CHUNK:(n + 1) * CHUNK, gi * gw:(gi + 1) * gw]
            groups.append(jnp.dot(ws_ref[gi], vg, preferred_element_type=F32))
        chunks.append(jnp.concatenate(groups, axis=1) + bs_ref[...])
    sv = jnp.concatenate(chunks, axis=0)
    h = jnp.dot((u * sv).astype(BF16), wout_ref[...], preferred_element_type=F32)
    o_ref[...] = _layer_norm(ALPHA * x + h, g_ref[...], b_ref[...])


def _gmlp_layer(x2, w_in, ln_g, ln_b, w_s, bs_full, w_out, g, b, tm=256):
    t, d = x2.shape
    row = pl.BlockSpec((tm, d), lambda i: (i, 0))
    full = lambda a: pl.BlockSpec(a.shape, lambda i: (0,) * a.ndim)
    args = (w_in, ln_g, ln_b, w_s, bs_full, w_out, g, b)
    return pl.pallas_call(
        functools.partial(_gmlp_kernel, tm=tm),
        grid=(t // tm,),
        in_specs=[row] + [full(a) for a in args],
        out_specs=row,
        out_shape=jax.ShapeDtypeStruct((t, d), F32),
        compiler_params=_params("parallel"),
        name="gmlp_layer",
    )(x2, *args)


def _router_kernel(x_ref, wr_ref, o_ref):
    logits = lax.dot_general(wr_ref[...], x_ref[...], NT_DIMS, preferred_element_type=F32,
                             precision=lax.Precision.HIGHEST)
    m = jnp.max(logits, axis=0, keepdims=True)
    p = jnp.exp(logits - m)
    o_ref[0] = p / jnp.sum(p, axis=0, keepdims=True)


def _router(x2, w_router_t, batch, seq, tm=1024):
    d = x2.shape[1]
    ns = seq // tm
    return pl.pallas_call(
        _router_kernel,
        grid=(batch, ns),
        in_specs=[
            pl.BlockSpec((tm, d), lambda b, i: (b * ns + i, 0)),
            pl.BlockSpec((N_EXPERTS, d), lambda b, i: (0, 0)),
        ],
        out_specs=pl.BlockSpec((1, N_EXPERTS, tm), lambda b, i: (b, 0, i)),
        out_shape=jax.ShapeDtypeStruct((batch, N_EXPERTS, seq), F32),
        compiler_params=_params("parallel", "parallel"),
        name="router",
    )(x2, w_router_t)


def _expert_kernel(x_ref, wg_ref, wu_ref, wd_ref, gate_ref, o_ref):
    f = pl.program_id(2)
    x = x_ref[0, 0]
    a = jnp.dot(x, wg_ref[0], preferred_element_type=F32)
    u = jnp.dot(x, wu_ref[0], preferred_element_type=F32)
    hmid = (a * jax.nn.sigmoid(a) * u).astype(BF16)
    y = jnp.dot(hmid, wd_ref[0], preferred_element_type=F32)

    @pl.when(f == 0)
    def _():
        o_ref[0, 0] = y

    @pl.when(f > 0)
    def _():
        o_ref[0, 0] += y

    @pl.when(f == pl.num_programs(2) - 1)
    def _():
        o_ref[0, 0] = o_ref[0, 0] * gate_ref[0, 0]


def _experts(xg, wg, wu, wd, gate, tf=512):
    batch, ne, cap, d = xg.shape
    dff = wg.shape[2]
    return pl.pallas_call(
        _expert_kernel,
        grid=(ne, batch, dff // tf),
        in_specs=[
            pl.BlockSpec((1, 1, cap, d), lambda e, b, f: (b, e, 0, 0)),
            pl.BlockSpec((1, d, tf), lambda e, b, f: (e, 0, f)),
            pl.BlockSpec((1, d, tf), lambda e, b, f: (e, 0, f)),
            pl.BlockSpec((1, tf, d), lambda e, b, f: (e, f, 0)),
            pl.BlockSpec((1, 1, cap, 1), lambda e, b, f: (b, e, 0, 0)),
        ],
        out_specs=pl.BlockSpec((1, 1, cap, d), lambda e, b, f: (b, e, 0, 0)),
        out_shape=jax.ShapeDtypeStruct((batch, ne, cap, d), F32),
        compiler_params=_params("parallel", "parallel", "arbitrary"),
        name="experts",
    )(xg, wg, wu, wd, gate)


def _res_ln_kernel(x_ref, h_ref, g_ref, b_ref, o_ref):
    o_ref[...] = _layer_norm(ALPHA * x_ref[...] + h_ref[...], g_ref[...], b_ref[...])


def _res_ln(x2, h2, g, b, tm=1024):
    t, d = x2.shape
    row = pl.BlockSpec((tm, d), lambda i: (i, 0))
    vec = pl.BlockSpec((1, d), lambda i: (0, 0))
    return pl.pallas_call(
        _res_ln_kernel,
        grid=(t // tm,),
        in_specs=[row, row, vec, vec],
        out_specs=row,
        out_shape=jax.ShapeDtypeStruct((t, d), F32),
        compiler_params=_params("parallel"),
        name="res_ln",
    )(x2, h2, g, b)


def _moe_layer(x2, w_router, wg, wu, wd, g, b, batch, seq):
    d = x2.shape[1]
    cap = EC_FACTOR * seq // N_EXPERTS
    aff_t = _router(x2, w_router.astype(F32).T, batch, seq)
    gate, idx = lax.top_k(aff_t, cap)
    x3 = x2.reshape(batch, seq, d)
    bidx = jnp.arange(batch)[:, None, None]
    xg = x3[bidx, idx].astype(BF16)
    y = _experts(xg, wg, wu, wd, gate[..., None])
    h = jnp.zeros_like(x3).at[bidx, idx].add(y)
    return _res_ln(x2, h.reshape(batch * seq, d), g, b)


def _lambda_init(layer_number):
    return 0.8 - 0.6 * float(np.exp(-0.3 * (layer_number - 1)))


def _row(v):
    return v.astype(F32).reshape(1, -1)


def kernel(x, w_in_ab, rpb_a, lambda_qk, subln_g, w_out_ab, w_in_c, ln_v_g, ln_v_b, w_s, b_s,
           w_out_c, ln_mix_g, ln_mix_b, w_router, w_gate, w_up, w_down, ln_ffn_g, ln_ffn_b):
    batch, seq, d = x.shape
    x2 = x.astype(F32).reshape(batch * seq, d)
    n_qk = 3 * WIDTH_A + 2 * WIDTH_B
    col_scale = np.ones((1, n_qk), np.float32)
    col_scale[:, :WIDTH_A] = QK_SCALE
    col_scale[:, 3 * WIDTH_A:3 * WIDTH_A + WIDTH_B] = QK_SCALE * LOG2E
    col_scale = jnp.asarray(col_scale)

    for l in range(DEPTH):
        i = l // 2
        if l % 2 == 0:
            lam_init = _lambda_init(l + 1)
            lf = lambda_qk[i].astype(F32)
            lam = jnp.exp(jnp.sum(lf[0] * lf[1])) - jnp.exp(jnp.sum(lf[2] * lf[3])) + lam_init
            w_in = w_in_ab[i].astype(BF16)
            qk = _project(x2, w_in[:, :n_qk], col_scale)
            vt = _project_t(x2, w_in[:, n_qk:].T)
            oa = _natten(qk, _natten_bias(rpb_a[i]), batch, seq)
            ob = _diff_attention(qk, vt, lam.reshape(1), subln_g[i], lam_init, batch, seq)
            x2 = _attn_out(oa, ob, w_out_ab[i].astype(BF16), x2, _row(ln_mix_g[l]), _row(ln_mix_b[l]))
        else:
            gw = w_in_c.shape[2] // 2 // GMLP_GROUPS
            bs_full = jnp.repeat(b_s[i].astype(F32).T, gw, axis=1)
            x2 = _gmlp_layer(x2, w_in_c[i].astype(BF16), _row(ln_v_g[i]), _row(ln_v_b[i]),
                             w_s[i].astype(BF16), bs_full, w_out_c[i].astype(BF16),
                             _row(ln_mix_g[l]), _row(ln_mix_b[l]))
        x2 = _moe_layer(x2, w_router[l], w_gate[l].astype(BF16), w_up[l].astype(BF16),
                        w_down[l].astype(BF16), _row(ln_ffn_g[l]), _row(ln_ffn_b[l]), batch, seq)
    return x2.reshape(batch, seq, d).astype(x.dtype)
```

```python
import functools

import jax
import jax.numpy as jnp
import numpy as np
from jax import lax
from jax.experimental import pallas as pl
from jax.experimental.pallas import tpu as pltpu

F32 = jnp.float32
BF16 = jnp.bfloat16

DEPTH = 4
GRID_W = 64
HEAD_DIM = 64
N_HEADS_A = 8
WIN_ROWS = 8
WIN_COLS = 16
WIDTH_A = N_HEADS_A * HEAD_DIM
N_HEADS_B = 4
WIDTH_B = N_HEADS_B * 2 * HEAD_DIM
CHUNK = 128
GMLP_GROUPS = 8
N_EXPERTS = 16
EC_FACTOR = 2
LN_EPS = 1e-5
RMS_EPS = 1e-6
ALPHA = (2.0 * DEPTH) ** 0.25
QK_SCALE = HEAD_DIM ** -0.5
MASK_VALUE = -1e30
LOG2E = float(np.log2(np.e))

LANES = 128
VMEM_LIMIT = 56 * 1024 * 1024

NT_DIMS = (((1,), (1,)), ((), ()))


def _params(*sem):
    return pltpu.CompilerParams(dimension_semantics=sem, vmem_limit_bytes=VMEM_LIMIT)


def _layer_norm(y, g, b):
    mu = jnp.mean(y, axis=-1, keepdims=True)
    yc = y - mu
    var = jnp.mean(yc * yc, axis=-1, keepdims=True)
    return yc * lax.rsqrt(var + LN_EPS) * g + b


def _bf16_split(value, parts=3):
    out, rest = [], np.float64(value)
    for _ in range(parts):
        piece = float(np.asarray(rest, np.float32).astype(jnp.bfloat16).astype(np.float64))
        out.append(piece)
        rest = rest - piece
    return out


def _proj_kernel(x_ref, w_ref, s_ref, o_ref):
    acc = jnp.dot(x_ref[...].astype(BF16), w_ref[...], preferred_element_type=F32)
    o_ref[...] = (acc * s_ref[...]).astype(o_ref.dtype)


def _project(x2, w_bf16, col_scale, tm=1024, tn=512):
    t, k = x2.shape
    n = w_bf16.shape[1]
    return pl.pallas_call(
        _proj_kernel,
        grid=(t // tm, n // tn),
        in_specs=[
            pl.BlockSpec((tm, k), lambda i, j: (i, 0)),
            pl.BlockSpec((k, tn), lambda i, j: (0, j)),
            pl.BlockSpec((1, tn), lambda i, j: (0, j)),
        ],
        out_specs=pl.BlockSpec((tm, tn), lambda i, j: (i, j)),
        out_shape=jax.ShapeDtypeStruct((t, n), BF16),
        compiler_params=_params("parallel", "arbitrary"),
        name="qkv_proj",
    )(x2, w_bf16, col_scale)


def _proj_t_kernel(x_ref, wt_ref, o_ref):
    acc = lax.dot_general(wt_ref[...], x_ref[...].astype(BF16), NT_DIMS, preferred_element_type=F32)
    o_ref[...] = acc.astype(o_ref.dtype)


def _project_t(x2, wt_bf16, tm=1024):
    t, k = x2.shape
    n = wt_bf16.shape[0]
    return pl.pallas_call(
        _proj_t_kernel,
        grid=(t // tm,),
        in_specs=[
            pl.BlockSpec((tm, k), lambda i: (i, 0)),
            pl.BlockSpec((n, k), lambda i: (0, 0)),
        ],
        out_specs=pl.BlockSpec((n, tm), lambda i: (0, i)),
        out_shape=jax.ShapeDtypeStruct((n, t), BF16),
        compiler_params=_params("parallel"),
        name="v_proj_t",
    )(x2, wt_bf16)


NAT_ROWS = 4
NAT_KROWS = NAT_ROWS + WIN_ROWS


def _natten_bias(rpb, n_rows):
    n_heads = rpb.shape[0]
    span = 2 * WIN_COLS - 1
    pad_lo = GRID_W - WIN_COLS
    padded = jnp.pad(rpb.astype(F32) * LOG2E, ((0, 0), (0, 0), (pad_lo, 2 * GRID_W - 1 - pad_lo - span)))
    toep = jnp.stack([padded[:, :, GRID_W - 1 - c:2 * GRID_W - 1 - c] for c in range(GRID_W)], axis=3)
    c = np.arange(GRID_W)[None, :]
    kc = np.arange(GRID_W)[:, None]
    c_start = np.clip(c - WIN_COLS // 2, 0, GRID_W - WIN_COLS)
    valid = (kc >= c_start) & (kc < c_start + WIN_COLS)
    toep = jnp.where(valid[None, None], toep, MASK_VALUE)
    masked = jnp.full((n_heads, GRID_W, GRID_W), MASK_VALUE, F32)
    n_blk = n_rows // NAT_ROWS
    kinds = []
    for blk in (0, 1, n_blk - 1):
        r0 = blk * NAT_ROWS
        ks = int(np.clip(r0 - WIN_ROWS // 2, 0, n_rows - NAT_KROWS))
        cols = []
        for a in range(NAT_ROWS):
            r = r0 + a
            rs = int(np.clip(r - WIN_ROWS // 2, 0, n_rows - WIN_ROWS))
            rows = []
            for i in range(NAT_KROWS):
                kr = ks + i
                rows.append(toep[:, kr - r + WIN_ROWS - 1] if rs <= kr < rs + WIN_ROWS else masked)
            cols.append(jnp.concatenate(rows, axis=1))
        kinds.append(jnp.concatenate(cols, axis=2))
    b = jnp.stack(kinds, axis=0)
    nk, nq = b.shape[2], b.shape[3]
    b = b.reshape(3, n_heads // 2, 2, nk, nq).transpose(0, 1, 3, 2, 4)
    return b.reshape(3, n_heads // 2, nk, 2 * nq)


def _natten_kernel(q_ref, k_ref, vt_ref, bias_ref, o_ref, raw_ref, cmax_ref, *, n_rows):
    nq = NAT_ROWS * GRID_W
    nk = NAT_KROWS * GRID_W
    n_blk = n_rows // NAT_ROWS
    lane = lax.broadcasted_iota(jnp.int32, (nq, LANES), 1)
    low = lane < HEAD_DIM

    def key_start(blk):
        ks = jnp.clip(blk * NAT_ROWS - WIN_ROWS // 2, 0, n_rows - NAT_KROWS)
        return pl.multiple_of(ks * GRID_W, NAT_ROWS * GRID_W)

    def scores(blk, slot):
        q = q_ref[pl.ds(pl.multiple_of(blk * nq, nq), nq), :]
        zero = jnp.zeros_like(q)
        qx = jnp.concatenate([jnp.where(low, q, zero), jnp.where(low, zero, q)], axis=0)
        k = k_ref[pl.ds(key_start(blk), nk), :]
        kind = jnp.where(blk == 0, 0, jnp.where(blk == n_blk - 1, 2, 1))
        raw = lax.dot_general(k, qx, NT_DIMS, preferred_element_type=F32) + bias_ref[kind, 0]
        raw_ref[slot] = raw
        cmax_ref[slot] = jnp.max(raw, axis=0, keepdims=True)

    def absorb(blk, slot):
        vt = vt_ref[:, pl.ds(key_start(blk), nk)]
        p = jnp.exp2(raw_ref[slot] - cmax_ref[slot])
        l = jnp.sum(p, axis=0, keepdims=True)
        res = jnp.dot(vt, p.astype(BF16), preferred_element_type=F32) / l
        o_t = jnp.concatenate([res[:HEAD_DIM, :nq], res[HEAD_DIM:, nq:]], axis=0)
        o_ref[pl.ds(pl.multiple_of(blk * nq, nq), nq), :] = o_t.T.astype(o_ref.dtype)

    scores(0, 0)

    def body(u, carry):
        blk = 2 * u
        scores(blk + 1, 1)
        absorb(blk, 0)
        scores(blk + 2, 0)
        absorb(blk + 1, 1)
        return carry

    assert n_blk % 2 == 0
    lax.fori_loop(0, n_blk // 2 - 1, body, 0)
    scores(n_blk - 1, 1)
    absorb(n_blk - 2, 0)
    absorb(n_blk - 1, 1)


def _natten(qk, vt, bias, batch, seq):
    n_rows = seq // GRID_W
    hp = N_HEADS_A // 2
    nq = NAT_ROWS * GRID_W
    nk = NAT_KROWS * GRID_W
    return pl.pallas_call(
        functools.partial(_natten_kernel, n_rows=n_rows),
        grid=(batch, hp),
        in_specs=[
            pl.BlockSpec((seq, LANES), lambda b, h: (b, h)),
            pl.BlockSpec((seq, LANES), lambda b, h: (b, hp + h)),
            pl.BlockSpec((LANES, seq), lambda b, h: (h, b)),
            pl.BlockSpec((3, 1, nk, 2 * nq), lambda b, h: (0, h, 0, 0)),
        ],
        out_specs=pl.BlockSpec((seq, LANES), lambda b, h: (b, h)),
        out_shape=jax.ShapeDtypeStruct((batch * seq, WIDTH_A), BF16),
        scratch_shapes=[pltpu.VMEM((2, nk, 2 * nq), F32), pltpu.VMEM((2, 1, 2 * nq), F32)],
        compiler_params=_params("parallel", "parallel"),
        name="natten",
    )(qk, qk, vt, bias)


def _diff_kernel(sl_ref, lam_ref, q_ref, k_ref, vt_ref, f_ref, pen_ref, g_ref, o_ref,
                 qop_ref, raw_ref, cmax_ref, coff_ref, acc_ref, *, tq, tk, out_scale):
    h = pl.program_id(1)
    qi = pl.program_id(2)
    n_kt = k_ref.shape[0] // tk
    sl = sl_ref[h]
    a1, a2, a3 = _bf16_split(LOG2E)

    q = q_ref[...]
    lane_q = lax.broadcasted_iota(jnp.int32, q.shape, 1)
    low_q = lane_q < HEAD_DIM
    zero = jnp.zeros_like(q)
    q1 = jnp.where(low_q, q, zero)
    q2 = jnp.where(low_q, zero, q)

    def q_ops(sign):
        if sign == 0:
            f = zero
        else:
            f = jnp.where(lane_q == 0, sign * a1,
                          jnp.where(lane_q == 1, sign * a2, jnp.where(lane_q == 2, sign * a3, 0.0))).astype(BF16)
        return jnp.concatenate([jnp.concatenate([q1, f], axis=1), jnp.concatenate([q2, f], axis=1)], axis=0)

    qop_ref[0] = q_ops(1.0)
    qop_ref[1] = q_ops(-1.0)
    fk = f_ref[0]
    pos = lax.broadcasted_iota(jnp.int32, (1, 2 * tq), 1)
    iq = (qi * tq + jnp.where(pos < tq, pos, pos - tq)).astype(F32)

    kt_mix = (qi * tq) // tk
    sel = qi - kt_mix * (tk // tq)

    def load_k(kt):
        off = pl.multiple_of(kt * tk, tk)
        return jnp.concatenate([k_ref[pl.ds(off, tk), :], fk], axis=1), vt_ref[:, pl.ds(off, tk)]

    def tile_of(s):
        t = s - 1
        return jnp.where(s == 0, kt_mix, t + (t >= kt_mix).astype(jnp.int32))

    def scores(s, slot):
        kt = tile_of(s)
        kx, _ = load_k(kt)
        raw = lax.dot_general(kx, qop_ref[(kt > kt_mix).astype(jnp.int32)], NT_DIMS, preferred_element_type=F32)
        c = -sl * jnp.abs(iq - (kt * tk + tk // 2).astype(F32))
        raw_ref[slot] = raw
        cmax_ref[slot] = jnp.max(raw, axis=0, keepdims=True) + c
        coff_ref[slot] = c

    def absorb(s, slot, state):
        m_prev, l_prev = state
        _, vt = load_k(tile_of(s))
        m_new = jnp.maximum(m_prev, cmax_ref[slot])
        a = jnp.exp2(m_prev - m_new)
        p = jnp.exp2(raw_ref[slot] - (m_new - coff_ref[slot]))
        l_new = a * l_prev + jnp.sum(p, axis=0, keepdims=True)
        acc_ref[...] = a * acc_ref[...] + jnp.dot(vt, p.astype(BF16), preferred_element_type=F32)
        return m_new, l_new

    acc_ref[...] = jnp.zeros_like(acc_ref)
    state = (jnp.full((1, 2 * tq), -jnp.inf, F32), jnp.zeros((1, 2 * tq), F32))

    kx, _ = load_k(kt_mix)
    pen = pen_ref[0, sel]
    raw = lax.dot_general(kx, q_ops(0), NT_DIMS, preferred_element_type=F32) - jnp.concatenate([pen, pen], axis=1)
    raw_ref[0] = raw
    cmax_ref[0] = jnp.max(raw, axis=0, keepdims=True)
    coff_ref[0] = jnp.zeros((1, 2 * tq), F32)

    def body(u, st):
        s = 2 * u
        scores(s + 1, 1)
        st = absorb(s, 0, st)
        scores(s + 2, 0)
        return absorb(s + 1, 1, st)

    assert n_kt % 2 == 0
    state = lax.fori_loop(0, n_kt // 2 - 1, body, state)
    scores(n_kt - 1, 1)
    state = absorb(n_kt - 2, 0, state)
    _, l = absorb(n_kt - 1, 1, state)

    on = acc_ref[...] / l
    o = (on[:, :tq] - lam_ref[0] * on[:, tq:]).T
    o = o * lax.rsqrt(jnp.mean(o * o, axis=-1, keepdims=True) + RMS_EPS)
    o_ref[...] = (o * g_ref[0] * out_scale).astype(o_ref.dtype)


def _diff_tables(slopes, tq, tk):
    j = np.arange(tk)
    feat = slopes[:, None] * (j - tk // 2)[None, :]
    feat = np.broadcast_to(feat[:, :, None], (len(slopes), tk, LANES))
    i = np.arange(tq)
    pos = np.arange(tk // tq)[:, None, None] * tq + i[None, None, :]
    dist = np.abs(pos - j[None, :, None])
    pen = (slopes * LOG2E)[:, None, None, None] * dist[None]
    return jnp.asarray(feat, BF16), jnp.asarray(pen, F32)


def _diff_attention(qk, vt, lam, subln_g, lam_init, batch, seq, tq=256, tk=512):
    nq = seq // tq
    base = 2 * WIDTH_A // LANES
    hb = N_HEADS_B
    slopes = 2.0 ** (-8.0 * (np.arange(hb) + 1) / hb)
    feat, pen = _diff_tables(slopes, tq, tk)
    sl = jnp.asarray(slopes * LOG2E, F32)
    smem = pl.BlockSpec(memory_space=pltpu.SMEM)
    return pl.pallas_call(
        functools.partial(_diff_kernel, tq=tq, tk=tk, out_scale=1.0 - lam_init),
        grid=(batch, hb, nq),
        in_specs=[
            smem, smem,
            pl.BlockSpec((tq, LANES), lambda b, h, i: (b * nq + i, base + h)),
            pl.BlockSpec((seq, LANES), lambda b, h, i: (b, base + hb + h)),
            pl.BlockSpec((LANES, seq), lambda b, h, i: (WIDTH_A // LANES + h, b)),
            pl.BlockSpec((1, tk, LANES), lambda b, h, i: (h, 0, 0)),
            pl.BlockSpec((1, tk // tq, tk, tq), lambda b, h, i: (h, 0, 0, 0)),
            pl.BlockSpec((1, 1, LANES), lambda b, h, i: (h, 0, 0)),
        ],
        out_specs=pl.BlockSpec((tq, LANES), lambda b, h, i: (b * nq + i, h)),
        out_shape=jax.ShapeDtypeStruct((batch * seq, WIDTH_B), BF16),
        scratch_shapes=[
            pltpu.VMEM((2, 2 * tq, 2 * LANES), BF16),
            pltpu.VMEM((2, tk, 2 * tq), F32),
            pltpu.VMEM((2, 1, 2 * tq), F32),
            pltpu.VMEM((2, 1, 2 * tq), F32),
            pltpu.VMEM((LANES, 2 * tq), F32),
        ],
        compiler_params=_params("parallel", "parallel", "parallel"),
        name="diff_attn",
    )(sl, lam, qk, qk, vt, feat, pen, subln_g.astype(F32).reshape(hb, 1, LANES))


def _attn_out_kernel(oa_ref, ob_ref, wa_ref, wb_ref, x_ref, g_ref, b_ref, o_ref):
    h = jnp.dot(oa_ref[...], wa_ref[...], preferred_element_type=F32)
    h = h + jnp.dot(ob_ref[...], wb_ref[...], preferred_element_type=F32)
    o_ref[...] = _layer_norm(ALPHA * x_ref[...] + h, g_ref[...], b_ref[...])


def _attn_out(oa, ob, w_out_bf16, x2, g, b, tm=512):
    t, d = x2.shape
    wa, wb = w_out_bf16[:WIDTH_A], w_out_bf16[WIDTH_A:]
    row = lambda w: pl.BlockSpec((tm, w), lambda i: (i, 0))
    full = lambda a: pl.BlockSpec(a.shape, lambda i: (0,) * a.ndim)
    return pl.pallas_call(
        _attn_out_kernel,
        grid=(t // tm,),
        in_specs=[row(WIDTH_A), row(WIDTH_B), full(wa), full(wb), row(d), full(g), full(b)],
        out_specs=row(d),
        out_shape=jax.ShapeDtypeStruct((t, d), F32),
        compiler_params=_params("parallel"),
        name="attn_out_ln",
    )(oa, ob, wa, wb, x2, g, b)


def _gmlp_kernel(x_ref, win_ref, lg_ref, lb_ref, ws_ref, bs_ref, wout_ref, g_ref, b_ref, o_ref, *, tm):
    x = x_ref[...]
    width = wout_ref.shape[0]
    gw = width // GMLP_GROUPS
    z = jnp.dot(x.astype(BF16), win_ref[...], preferred_element_type=F32)
    z = 0.5 * z * (1.0 + lax.erf(z * (2.0 ** -0.5)))
    u = z[:, :width]
    v = _layer_norm(z[:, width:], lg_ref[...], lb_ref[...]).astype(BF16)
    chunks = []
    for n in range(tm // CHUNK):
        groups = []
        for gi in range(GMLP_GROUPS):
            vg = v[n * CHUNK:(n + 1) * CHUNK, gi * gw:(gi + 1) * gw]
            groups.append(jnp.dot(ws_ref[gi], vg, preferred_element_type=F32))
        chunks.append(jnp.concatenate(groups, axis=1) + bs_ref[...])
    sv = jnp.concatenate(chunks, axis=0)
    h = jnp.dot((u * sv).astype(BF16), wout_ref[...], preferred_element_type=F32)
    o_ref[...] = _layer_norm(ALPHA * x + h, g_ref[...], b_ref[...])


def _gmlp_layer(x2, w_in, ln_g, ln_b, w_s, bs_full, w_out, g, b, tm=256):
    t, d = x2.shape
    row = pl.BlockSpec((tm, d), lambda i: (i, 0))
    full = lambda a: pl.BlockSpec(a.shape, lambda i: (0,) * a.ndim)
    args = (w_in, ln_g, ln_b, w_s, bs_full, w_out, g, b)
    return pl.pallas_call(
        functools.partial(_gmlp_kernel, tm=tm),
        grid=(t // tm,),
        in_specs=[row] + [full(a) for a in args],
        out_specs=row,
        out_shape=jax.ShapeDtypeStruct((t, d), F32),
        compiler_params=_params("parallel"),
        name="gmlp_layer",
    )(x2, *args)


def _router_kernel(x_ref, wr_ref, o_ref):
    logits = lax.dot_general(wr_ref[...], x_ref[...], NT_DIMS, preferred_element_type=F32,
                             precision=lax.Precision.HIGHEST)
    m = jnp.max(logits, axis=0, keepdims=True)
    p = jnp.exp(logits - m)
    o_ref[0] = p / jnp.sum(p, axis=0, keepdims=True)


def _router(x2, w_router_t, batch, seq, tm=1024):
    d = x2.shape[1]
    ns = seq // tm
    return pl.pallas_call(
        _router_kernel,
        grid=(batch, ns),
        in_specs=[
            pl.BlockSpec((tm, d), lambda b, i: (b * ns + i, 0)),
            pl.BlockSpec((N_EXPERTS, d), lambda b, i: (0, 0)),
        ],
        out_specs=pl.BlockSpec((1, N_EXPERTS, tm), lambda b, i: (b, 0, i)),
        out_shape=jax.ShapeDtypeStruct((batch, N_EXPERTS, seq), F32),
        compiler_params=_params("parallel", "parallel"),
        name="router",
    )(x2, w_router_t)


def _expert_kernel(x_ref, wg_ref, wu_ref, wd_ref, gate_ref, o_ref):
    f = pl.program_id(2)
    x = x_ref[0, 0]
    a = jnp.dot(x, wg_ref[0], preferred_element_type=F32)
    u = jnp.dot(x, wu_ref[0], preferred_element_type=F32)
    hmid = (a * jax.nn.sigmoid(a) * u).astype(BF16)
    y = jnp.dot(hmid, wd_ref[0], preferred_element_type=F32)

    @pl.when(f == 0)
    def _():
        o_ref[0, 0] = y

    @pl.when(f > 0)
    def _():
        o_ref[0, 0] += y

    @pl.when(f == pl.num_programs(2) - 1)
    def _():
        o_ref[0, 0] = o_ref[0, 0] * gate_ref[0, 0]


def _experts(xg, wg, wu, wd, gate, tf=512):
    batch, ne, cap, d = xg.shape
    dff = wg.shape[2]
    return pl.pallas_call(
        _expert_kernel,
        grid=(ne, batch, dff // tf),
        in_specs=[
            pl.BlockSpec((1, 1, cap, d), lambda e, b, f: (b, e, 0, 0)),
            pl.BlockSpec((1, d, tf), lambda e, b, f: (e, 0, f)),
            pl.BlockSpec((1, d, tf), lambda e, b, f: (e, 0, f)),
            pl.BlockSpec((1, tf, d), lambda e, b, f: (e, f, 0)),
            pl.BlockSpec((1, 1, cap, 1), lambda e, b, f: (b, e, 0, 0)),
        ],
        out_specs=pl.BlockSpec((1, 1, cap, d), lambda e, b, f: (b, e, 0, 0)),
        out_shape=jax.ShapeDtypeStruct((batch, ne, cap, d), F32),
        compiler_params=_params("parallel", "parallel", "arbitrary"),
        name="experts",
    )(xg, wg, wu, wd, gate)


def _gather_kernel(idx_ref, x_hbm, o_ref, buf, sem, *, cap):
    def issue(r, carry):
        pltpu.make_async_copy(x_hbm.at[pl.ds(idx_ref[0, 0, r], 1), :], buf.at[pl.ds(r, 1), :], sem).start()
        return carry

    lax.fori_loop(0, cap, issue, 0, unroll=8)
    pltpu.make_async_copy(x_hbm.at[pl.ds(0, cap), :], buf, sem).wait()
    o_ref[...] = buf[...].astype(o_ref.dtype)


def _gather_rows(x2, rows, cap):
    g = rows.shape[0]
    d = x2.shape[1]
    return pl.pallas_call(
        functools.partial(_gather_kernel, cap=cap),
        grid=(g,),
        in_specs=[
            pl.BlockSpec((1, 1, cap), lambda i: (i, 0, 0), memory_space=pltpu.SMEM),
            pl.BlockSpec(memory_space=pl.ANY),
        ],
        out_specs=pl.BlockSpec((cap, d), lambda i: (i, 0)),
        out_shape=jax.ShapeDtypeStruct((g * cap, d), BF16),
        scratch_shapes=[pltpu.VMEM((cap, d), F32), pltpu.SemaphoreType.DMA],
        compiler_params=_params("arbitrary"),
        name="gather_rows",
    )(rows, x2)


COMBINE_ROWS_PER_STEP = 8


def _combine_kernel(idx_ref, y_ref, o_hbm, acc_ref, sem, *, cap, seq):
    b = pl.program_id(0)
    e = pl.program_id(1)

    @pl.when(e == 0)
    def _():
        acc_ref[...] = jnp.zeros_like(acc_ref)

    def step(i, carry):
        r0 = pl.multiple_of(i * COMBINE_ROWS_PER_STEP, COMBINE_ROWS_PER_STEP)
        toks = [idx_ref[0, 0, r0 + j] for j in range(COMBINE_ROWS_PER_STEP)]
        rows = [acc_ref[pl.ds(toks[j], 1), :] + y_ref[0, 0, pl.ds(r0 + j, 1), :] for j in range(COMBINE_ROWS_PER_STEP)]
        for j in range(COMBINE_ROWS_PER_STEP):
            acc_ref[pl.ds(toks[j], 1), :] = rows[j]
        return carry

    lax.fori_loop(0, cap // COMBINE_ROWS_PER_STEP, step, 0)

    @pl.when(e == pl.num_programs(1) - 1)
    def _():
        cp = pltpu.make_async_copy(acc_ref, o_hbm.at[pl.ds(pl.multiple_of(b * seq, seq), seq), :], sem)
        cp.start()
        cp.wait()


def _combine(y, idx, seq):
    batch, ne, cap, d = y.shape
    return pl.pallas_call(
        functools.partial(_combine_kernel, cap=cap, seq=seq),
        grid=(batch, ne),
        in_specs=[
            pl.BlockSpec((1, 1, cap), lambda b, e: (b * ne + e, 0, 0), memory_space=pltpu.SMEM),
            pl.BlockSpec((1, 1, cap, d), lambda b, e: (b, e, 0, 0)),
        ],
        out_specs=pl.BlockSpec(memory_space=pl.ANY),
        out_shape=jax.ShapeDtypeStruct((batch * seq, d), F32),
        scratch_shapes=[pltpu.VMEM((seq, d), F32), pltpu.SemaphoreType.DMA],
        compiler_params=_params("arbitrary", "arbitrary"),
        name="combine",
    )(idx.reshape(batch * ne, 1, cap), y)


def _res_ln_kernel(x_ref, h_ref, g_ref, b_ref, o_ref):
    o_ref[...] = _layer_norm(ALPHA * x_ref[...] + h_ref[...], g_ref[...], b_ref[...])


def _res_ln(x2, h2, g, b, tm=1024):
    t, d = x2.shape
    row = pl.BlockSpec((tm, d), lambda i: (i, 0))
    vec = pl.BlockSpec((1, d), lambda i: (0, 0))
    return pl.pallas_call(
        _res_ln_kernel,
        grid=(t // tm,),
        in_specs=[row, row, vec, vec],
        out_specs=row,
        out_shape=jax.ShapeDtypeStruct((t, d), F32),
        compiler_params=_params("parallel"),
        name="res_ln",
    )(x2, h2, g, b)


def _moe_layer(x2, w_router, wg, wu, wd, g, b, batch, seq):
    d = x2.shape[1]
    cap = EC_FACTOR * seq // N_EXPERTS
    aff_t = _router(x2, w_router.astype(F32).T, batch, seq)
    gate, idx = lax.top_k(aff_t, cap)
    rows = idx + (jnp.arange(batch, dtype=idx.dtype) * seq)[:, None, None]
    xg = _gather_rows(x2, rows.reshape(batch * N_EXPERTS, 1, cap), cap).reshape(batch, N_EXPERTS, cap, d)
    y = _experts(xg, wg, wu, wd, gate[..., None])
    return _res_ln(x2, _combine(y, idx, seq), g, b)


def _lambda_init(layer_number):
    return 0.8 - 0.6 * float(np.exp(-0.3 * (layer_number - 1)))


def _row(v):
    return v.astype(F32).reshape(1, -1)


def kernel(x, w_in_ab, rpb_a, lambda_qk, subln_g, w_out_ab, w_in_c, ln_v_g, ln_v_b, w_s, b_s,
           w_out_c, ln_mix_g, ln_mix_b, w_router, w_gate, w_up, w_down, ln_ffn_g, ln_ffn_b):
    batch, seq, d = x.shape
    x2 = x.astype(F32).reshape(batch * seq, d)
    v_a = slice(2 * WIDTH_A, 3 * WIDTH_A)
    v_b = slice(3 * WIDTH_A + 2 * WIDTH_B, 3 * WIDTH_A + 3 * WIDTH_B)
    col_scale = np.ones((1, 2 * WIDTH_A + 2 * WIDTH_B), np.float32)
    col_scale[:, :WIDTH_A] = QK_SCALE * LOG2E
    col_scale[:, 2 * WIDTH_A:2 * WIDTH_A + WIDTH_B] = QK_SCALE * LOG2E
    col_scale = jnp.asarray(col_scale)

    for l in range(DEPTH):
        i = l // 2
        if l % 2 == 0:
            lam_init = _lambda_init(l + 1)
            lf = lambda_qk[i].astype(F32)
            lam = jnp.exp(jnp.sum(lf[0] * lf[1])) - jnp.exp(jnp.sum(lf[2] * lf[3])) + lam_init
            w_in = w_in_ab[i].astype(BF16)
            qk = _project(x2, jnp.concatenate([w_in[:, :v_a.start], w_in[:, v_a.stop:v_b.start]], axis=1), col_scale)
            vt = _project_t(x2, jnp.concatenate([w_in[:, v_a], w_in[:, v_b]], axis=1).T)
            oa = _natten(qk, vt, _natten_bias(rpb_a[i], seq // GRID_W), batch, seq)
            ob = _diff_attention(qk, vt, lam.reshape(1), subln_g[i], lam_init, batch, seq)
            x2 = _attn_out(oa, ob, w_out_ab[i].astype(BF16), x2, _row(ln_mix_g[l]), _row(ln_mix_b[l]))
        else:
            gw = w_in_c.shape[2] // 2 // GMLP_GROUPS
            bs_full = jnp.repeat(b_s[i].astype(F32).T, gw, axis=1)
            x2 = _gmlp_layer(x2, w_in_c[i].astype(BF16), _row(ln_v_g[i]), _row(ln_v_b[i]),
                             w_s[i].astype(BF16), bs_full, w_out_c[i].astype(BF16),
                             _row(ln_mix_g[l]), _row(ln_mix_b[l]))
        x2 = _moe_layer(x2, w_router[l], w_gate[l].astype(BF16), w_up[l].astype(BF16),
                        w_down[l].astype(BF16), _row(ln_ffn_g[l]), _row(ln_ffn_b[l]), batch, seq)
    return x2.reshape(batch, seq, d).astype(x.dtype)
```

```python
import functools

import jax
import jax.numpy as jnp
import numpy as np
from jax import lax
from jax.experimental import pallas as pl
from jax.experimental.pallas import tpu as pltpu

F32 = jnp.float32
BF16 = jnp.bfloat16

DEPTH = 4
GRID_W = 64
HEAD_DIM = 64
N_HEADS_A = 8
WIN_ROWS = 8
WIN_COLS = 16
WIDTH_A = N_HEADS_A * HEAD_DIM
N_HEADS_B = 4
WIDTH_B = N_HEADS_B * 2 * HEAD_DIM
CHUNK = 128
GMLP_GROUPS = 8
N_EXPERTS = 16
EC_FACTOR = 2
LN_EPS = 1e-5
RMS_EPS = 1e-6
ALPHA = (2.0 * DEPTH) ** 0.25
QK_SCALE = HEAD_DIM ** -0.5
MASK_VALUE = -1e30
LOG2E = float(np.log2(np.e))
SKIP_LOG2 = 60.0
SKIP_MARGIN = 1.02

LANES = 128
VMEM_LIMIT = 56 * 1024 * 1024

NT_DIMS = (((1,), (1,)), ((), ()))


def _params(*sem):
    return pltpu.CompilerParams(dimension_semantics=sem, vmem_limit_bytes=VMEM_LIMIT)


def _layer_norm(y, g, b):
    mu = jnp.mean(y, axis=-1, keepdims=True)
    yc = y - mu
    var = jnp.mean(yc * yc, axis=-1, keepdims=True)
    return yc * lax.rsqrt(var + LN_EPS) * g + b


def _bf16_split(value, parts=3):
    out, rest = [], np.float64(value)
    for _ in range(parts):
        piece = float(np.asarray(rest, np.float32).astype(jnp.bfloat16).astype(np.float64))
        out.append(piece)
        rest = rest - piece
    return out


def _proj_kernel(x_ref, w_ref, s_ref, o_ref):
    acc = jnp.dot(x_ref[...].astype(BF16), w_ref[...], preferred_element_type=F32)
    o_ref[...] = (acc * s_ref[...]).astype(o_ref.dtype)


def _project(x2, w_bf16, col_scale, tm=1024, tn=512):
    t, k = x2.shape
    n = w_bf16.shape[1]
    return pl.pallas_call(
        _proj_kernel,
        grid=(t // tm, n // tn),
        in_specs=[
            pl.BlockSpec((tm, k), lambda i, j: (i, 0)),
            pl.BlockSpec((k, tn), lambda i, j: (0, j)),
            pl.BlockSpec((1, tn), lambda i, j: (0, j)),
        ],
        out_specs=pl.BlockSpec((tm, tn), lambda i, j: (i, j)),
        out_shape=jax.ShapeDtypeStruct((t, n), BF16),
        compiler_params=_params("parallel", "arbitrary"),
        name="qkv_proj",
    )(x2, w_bf16, col_scale)


def _proj_t_kernel(x_ref, wt_ref, o_ref):
    acc = lax.dot_general(wt_ref[...], x_ref[...].astype(BF16), NT_DIMS, preferred_element_type=F32)
    o_ref[...] = acc.astype(o_ref.dtype)


def _project_t(x2, wt_bf16, tm=1024):
    t, k = x2.shape
    n = wt_bf16.shape[0]
    return pl.pallas_call(
        _proj_t_kernel,
        grid=(t // tm,),
        in_specs=[
            pl.BlockSpec((tm, k), lambda i: (i, 0)),
            pl.BlockSpec((n, k), lambda i: (0, 0)),
        ],
        out_specs=pl.BlockSpec((n, tm), lambda i: (0, i)),
        out_shape=jax.ShapeDtypeStruct((n, t), BF16),
        compiler_params=_params("parallel"),
        name="v_proj_t",
    )(x2, wt_bf16)


NAT_ROWS = 4
NAT_KROWS = NAT_ROWS + WIN_ROWS


def _natten_bias(rpb, n_rows):
    n_heads = rpb.shape[0]
    span = 2 * WIN_COLS - 1
    pad_lo = GRID_W - WIN_COLS
    padded = jnp.pad(rpb.astype(F32) * LOG2E, ((0, 0), (0, 0), (pad_lo, 2 * GRID_W - 1 - pad_lo - span)))
    toep = jnp.stack([padded[:, :, GRID_W - 1 - c:2 * GRID_W - 1 - c] for c in range(GRID_W)], axis=3)
    c = np.arange(GRID_W)[None, :]
    kc = np.arange(GRID_W)[:, None]
    c_start = np.clip(c - WIN_COLS // 2, 0, GRID_W - WIN_COLS)
    valid = (kc >= c_start) & (kc < c_start + WIN_COLS)
    toep = jnp.where(valid[None, None], toep, MASK_VALUE)
    masked = jnp.full((n_heads, GRID_W, GRID_W), MASK_VALUE, F32)
    n_blk = n_rows // NAT_ROWS
    kinds = []
    for blk in (0, 1, n_blk - 1):
        r0 = blk * NAT_ROWS
        ks = int(np.clip(r0 - WIN_ROWS // 2, 0, n_rows - NAT_KROWS))
        cols = []
        for a in range(NAT_ROWS):
            r = r0 + a
            rs = int(np.clip(r - WIN_ROWS // 2, 0, n_rows - WIN_ROWS))
            rows = []
            for i in range(NAT_KROWS):
                kr = ks + i
                rows.append(toep[:, kr - r + WIN_ROWS - 1] if rs <= kr < rs + WIN_ROWS else masked)
            cols.append(jnp.concatenate(rows, axis=1))
        kinds.append(jnp.concatenate(cols, axis=2))
    b = jnp.stack(kinds, axis=0)
    nk, nq = b.shape[2], b.shape[3]
    b = b.reshape(3, n_heads // 2, 2, nk, nq).transpose(0, 1, 3, 2, 4)
    return b.reshape(3, n_heads // 2, nk, 2 * nq)


def _natten_kernel(q_ref, k_ref, vt_ref, bias_ref, o_ref, raw_ref, cmax_ref, *, n_rows):
    nq = NAT_ROWS * GRID_W
    nk = NAT_KROWS * GRID_W
    n_blk = n_rows // NAT_ROWS
    lane = lax.broadcasted_iota(jnp.int32, (nq, LANES), 1)
    low = lane < HEAD_DIM

    def key_start(blk):
        ks = jnp.clip(blk * NAT_ROWS - WIN_ROWS // 2, 0, n_rows - NAT_KROWS)
        return pl.multiple_of(ks * GRID_W, NAT_ROWS * GRID_W)

    def scores(blk, slot):
        q = q_ref[pl.ds(pl.multiple_of(blk * nq, nq), nq), :]
        zero = jnp.zeros_like(q)
        qx = jnp.concatenate([jnp.where(low, q, zero), jnp.where(low, zero, q)], axis=0)
        k = k_ref[pl.ds(key_start(blk), nk), :]
        kind = jnp.where(blk == 0, 0, jnp.where(blk == n_blk - 1, 2, 1))
        raw = lax.dot_general(k, qx, NT_DIMS, preferred_element_type=F32) + bias_ref[kind, 0]
        raw_ref[slot] = raw
        cmax_ref[slot] = jnp.max(raw, axis=0, keepdims=True)

    def absorb(blk, slot):
        vt = vt_ref[:, pl.ds(key_start(blk), nk)]
        p = jnp.exp2(raw_ref[slot] - cmax_ref[slot])
        l = jnp.sum(p, axis=0, keepdims=True)
        res = jnp.dot(vt, p.astype(BF16), preferred_element_type=F32) / l
        o_t = jnp.concatenate([res[:HEAD_DIM, :nq], res[HEAD_DIM:, nq:]], axis=0)
        o_ref[pl.ds(pl.multiple_of(blk * nq, nq), nq), :] = o_t.T.astype(o_ref.dtype)

    scores(0, 0)

    def body(u, carry):
        blk = 2 * u
        scores(blk + 1, 1)
        absorb(blk, 0)
        scores(blk + 2, 0)
        absorb(blk + 1, 1)
        return carry

    assert n_blk % 2 == 0
    lax.fori_loop(0, n_blk // 2 - 1, body, 0)
    scores(n_blk - 1, 1)
    absorb(n_blk - 2, 0)
    absorb(n_blk - 1, 1)


def _natten(qk, vt, bias, batch, seq):
    n_rows = seq // GRID_W
    hp = N_HEADS_A // 2
    nq = NAT_ROWS * GRID_W
    nk = NAT_KROWS * GRID_W
    return pl.pallas_call(
        functools.partial(_natten_kernel, n_rows=n_rows),
        grid=(batch, hp),
        in_specs=[
            pl.BlockSpec((seq, LANES), lambda b, h: (b, h)),
            pl.BlockSpec((seq, LANES), lambda b, h: (b, hp + h)),
            pl.BlockSpec((LANES, seq), lambda b, h: (h, b)),
            pl.BlockSpec((3, 1, nk, 2 * nq), lambda b, h: (0, h, 0, 0)),
        ],
        out_specs=pl.BlockSpec((seq, LANES), lambda b, h: (b, h)),
        out_shape=jax.ShapeDtypeStruct((batch * seq, WIDTH_A), BF16),
        scratch_shapes=[pltpu.VMEM((2, nk, 2 * nq), F32), pltpu.VMEM((2, 1, 2 * nq), F32)],
        compiler_params=_params("parallel", "parallel"),
        name="natten",
    )(qk, qk, vt, bias)


def _diff_kernel(sl_ref, lam_ref, q_ref, k_ref, vt_ref, f_ref, pen_ref, g_ref, o_ref,
                 qop_ref, raw_ref, cmax_ref, coff_ref, acc_ref, knorm_ref, lfin_ref, *, tq, tk, out_scale):
    h = pl.program_id(1)
    qi = pl.program_id(2)
    n_kt = k_ref.shape[0] // tk
    sl = sl_ref[h]
    a1, a2, a3 = _bf16_split(LOG2E)

    q = q_ref[...]
    lane_q = lax.broadcasted_iota(jnp.int32, q.shape, 1)
    low_q = lane_q < HEAD_DIM
    zero = jnp.zeros_like(q)
    q1 = jnp.where(low_q, q, zero)
    q2 = jnp.where(low_q, zero, q)

    def q_ops(sign):
        if sign == 0:
            f = zero
        else:
            f = jnp.where(lane_q == 0, sign * a1,
                          jnp.where(lane_q == 1, sign * a2, jnp.where(lane_q == 2, sign * a3, 0.0))).astype(BF16)
        return jnp.concatenate([jnp.concatenate([q1, f], axis=1), jnp.concatenate([q2, f], axis=1)], axis=0)

    qop_ref[0] = q_ops(1.0)
    qop_ref[1] = q_ops(-1.0)
    fk = f_ref[0]
    pos = lax.broadcasted_iota(jnp.int32, (1, 2 * tq), 1)
    iq = (qi * tq + jnp.where(pos < tq, pos, pos - tq)).astype(F32)

    kt_mix = (qi * tq) // tk
    sel = qi - kt_mix * (tk // tq)

    @pl.when(qi == 0)
    def _():
        def tile_norm(kt, best):
            kk = k_ref[pl.ds(pl.multiple_of(kt * tk, tk), tk), :].astype(F32)
            kk = kk * kk
            low_k = lax.broadcasted_iota(jnp.int32, kk.shape, 1) < HEAD_DIM
            n1 = jnp.sum(jnp.where(low_k, kk, 0.0), axis=1, keepdims=True)
            n2 = jnp.sum(jnp.where(low_k, 0.0, kk), axis=1, keepdims=True)
            return jnp.maximum(best, jnp.max(jnp.maximum(n1, n2), axis=0, keepdims=True))

        best = lax.fori_loop(0, n_kt, tile_norm, jnp.zeros((1, 1), F32))
        knorm_ref[...] = jnp.sqrt(best)

    qq = q.astype(F32)
    qq = qq * qq
    qn1 = jnp.sum(jnp.where(low_q, qq, 0.0), axis=1, keepdims=True)
    qn2 = jnp.sum(jnp.where(low_q, 0.0, qq), axis=1, keepdims=True)
    qnorm = jnp.sqrt(jnp.max(jnp.maximum(qn1, qn2), axis=0, keepdims=True))
    reach = (SKIP_LOG2 + 2.0 * SKIP_MARGIN * qnorm * knorm_ref[...]) / sl
    q_lo = (qi * tq).astype(F32)
    kt_lo = jnp.maximum(jnp.floor((q_lo - reach - 1.0) / tk), 0.0).astype(jnp.int32)[0, 0]
    kt_hi = jnp.minimum(jnp.floor((q_lo + (tq + reach)) / tk), n_kt - 1.0).astype(jnp.int32)[0, 0]
    n_steps = kt_hi - kt_lo + 1

    def load_k(kt):
        off = pl.multiple_of(kt * tk, tk)
        return jnp.concatenate([k_ref[pl.ds(off, tk), :], fk], axis=1), vt_ref[:, pl.ds(off, tk)]

    def tile_of(s):
        t = kt_lo + jnp.minimum(s, n_steps - 1) - 1
        return jnp.where(s == 0, kt_mix, t + (t >= kt_mix).astype(jnp.int32))

    def scores(s, slot):
        kt = tile_of(s)
        kx, _ = load_k(kt)
        raw = lax.dot_general(kx, qop_ref[(kt > kt_mix).astype(jnp.int32)], NT_DIMS, preferred_element_type=F32)
        c = -sl * jnp.abs(iq - (kt * tk + tk // 2).astype(F32))
        raw_ref[slot] = raw
        cmax_ref[slot] = jnp.max(raw, axis=0, keepdims=True) + c
        coff_ref[slot] = c

    def absorb(s, slot, state):
        m_prev, l_prev = state
        _, vt = load_k(tile_of(s))
        m_new = jnp.maximum(m_prev, cmax_ref[slot])
        a = jnp.exp2(m_prev - m_new)
        p = jnp.exp2(raw_ref[slot] - (m_new - coff_ref[slot]))
        l_new = a * l_prev + jnp.sum(p, axis=0, keepdims=True)
        acc_ref[...] = a * acc_ref[...] + jnp.dot(vt, p.astype(BF16), preferred_element_type=F32)
        return m_new, l_new

    acc_ref[...] = jnp.zeros_like(acc_ref)
    state = (jnp.full((1, 2 * tq), -jnp.inf, F32), jnp.zeros((1, 2 * tq), F32))

    kx, _ = load_k(kt_mix)
    pen = pen_ref[0, sel]
    raw = lax.dot_general(kx, q_ops(0), NT_DIMS, preferred_element_type=F32) - jnp.concatenate([pen, pen], axis=1)
    raw_ref[0] = raw
    cmax_ref[0] = jnp.max(raw, axis=0, keepdims=True)
    coff_ref[0] = jnp.zeros((1, 2 * tq), F32)

    def body(u, st):
        s = 2 * u
        scores(s + 1, 1)
        st = absorb(s, 0, st)
        scores(s + 2, 0)
        return absorb(s + 1, 1, st)

    m_run, l_run = lax.fori_loop(0, n_steps // 2, body, state)
    lfin_ref[...] = l_run

    @pl.when(n_steps % 2 == 1)
    def _():
        _, l_last = absorb(n_steps - 1, 0, (m_run, l_run))
        lfin_ref[...] = l_last

    on = acc_ref[...] / lfin_ref[...]
    o = (on[:, :tq] - lam_ref[0] * on[:, tq:]).T
    o = o * lax.rsqrt(jnp.mean(o * o, axis=-1, keepdims=True) + RMS_EPS)
    o_ref[...] = (o * g_ref[0] * out_scale).astype(o_ref.dtype)


def _diff_tables(slopes, tq, tk):
    j = np.arange(tk)
    feat = slopes[:, None] * (j - tk // 2)[None, :]
    feat = np.broadcast_to(feat[:, :, None], (len(slopes), tk, LANES))
    i = np.arange(tq)
    pos = np.arange(tk // tq)[:, None, None] * tq + i[None, None, :]
    dist = np.abs(pos - j[None, :, None])
    pen = (slopes * LOG2E)[:, None, None, None] * dist[None]
    return jnp.asarray(feat, BF16), jnp.asarray(pen, F32)


def _diff_attention(qk, vt, lam, subln_g, lam_init, batch, seq, tq=256, tk=512):
    nq = seq // tq
    base = 2 * WIDTH_A // LANES
    hb = N_HEADS_B
    slopes = 2.0 ** (-8.0 * (np.arange(hb) + 1) / hb)
    feat, pen = _diff_tables(slopes, tq, tk)
    sl = jnp.asarray(slopes * LOG2E, F32)
    smem = pl.BlockSpec(memory_space=pltpu.SMEM)
    return pl.pallas_call(
        functools.partial(_diff_kernel, tq=tq, tk=tk, out_scale=1.0 - lam_init),
        grid=(batch, hb, nq),
        in_specs=[
            smem, smem,
            pl.BlockSpec((tq, LANES), lambda b, h, i: (b * nq + i, base + h)),
            pl.BlockSpec((seq, LANES), lambda b, h, i: (b, base + hb + h)),
            pl.BlockSpec((LANES, seq), lambda b, h, i: (WIDTH_A // LANES + h, b)),
            pl.BlockSpec((1, tk, LANES), lambda b, h, i: (h, 0, 0)),
            pl.BlockSpec((1, tk // tq, tk, tq), lambda b, h, i: (h, 0, 0, 0)),
            pl.BlockSpec((1, 1, LANES), lambda b, h, i: (h, 0, 0)),
        ],
        out_specs=pl.BlockSpec((tq, LANES), lambda b, h, i: (b * nq + i, h)),
        out_shape=jax.ShapeDtypeStruct((batch * seq, WIDTH_B), BF16),
        scratch_shapes=[
            pltpu.VMEM((2, 2 * tq, 2 * LANES), BF16),
            pltpu.VMEM((2, tk, 2 * tq), F32),
            pltpu.VMEM((2, 1, 2 * tq), F32),
            pltpu.VMEM((2, 1, 2 * tq), F32),
            pltpu.VMEM((LANES, 2 * tq), F32),
            pltpu.VMEM((1, 1), F32),
            pltpu.VMEM((1, 2 * tq), F32),
        ],
        compiler_params=_params("arbitrary", "arbitrary", "arbitrary"),
        name="diff_attn",
    )(sl, lam, qk, qk, vt, feat, pen, subln_g.astype(F32).reshape(hb, 1, LANES))


def _attn_out_kernel(oa_ref, ob_ref, wa_ref, wb_ref, x_ref, g_ref, b_ref, o_ref):
    h = jnp.dot(oa_ref[...], wa_ref[...], preferred_element_type=F32)
    h = h + jnp.dot(ob_ref[...], wb_ref[...], preferred_element_type=F32)
    o_ref[...] = _layer_norm(ALPHA * x_ref[...] + h, g_ref[...], b_ref[...])


def _attn_out(oa, ob, w_out_bf16, x2, g, b, tm=512):
    t, d = x2.shape
    wa, wb = w_out_bf16[:WIDTH_A], w_out_bf16[WIDTH_A:]
    row = lambda w: pl.BlockSpec((tm, w), lambda i: (i, 0))
    full = lambda a: pl.BlockSpec(a.shape, lambda i: (0,) * a.ndim)
    return pl.pallas_call(
        _attn_out_kernel,
        grid=(t // tm,),
        in_specs=[row(WIDTH_A), row(WIDTH_B), full(wa), full(wb), row(d), full(g), full(b)],
        out_specs=row(d),
        out_shape=jax.ShapeDtypeStruct((t, d), F32),
        compiler_params=_params("parallel"),
        name="attn_out_ln",
    )(oa, ob, wa, wb, x2, g, b)


def _gmlp_kernel(x_ref, win_ref, lg_ref, lb_ref, ws_ref, bs_ref, wout_ref, g_ref, b_ref, o_ref, *, tm):
    x = x_ref[...]
    width = wout_ref.shape[0]
    gw = width // GMLP_GROUPS
    z = jnp.dot(x.astype(BF16), win_ref[...], preferred_element_type=F32)
    z = 0.5 * z * (1.0 + lax.erf(z * (2.0 ** -0.5)))
    u = z[:, :width]
    v = _layer_norm(z[:, width:], lg_ref[...], lb_ref[...]).astype(BF16)
    chunks = []
    for n in range(tm // CHUNK):
        groups = []
        for gi in range(GMLP_GROUPS):
            vg = v[n * CHUNK:(n + 1) * CHUNK, gi * gw:(gi + 1) * gw]
            groups.append(jnp.dot(ws_ref[gi], vg, preferred_element_type=F32))
        chunks.append(jnp.concatenate(groups, axis=1) + bs_ref[...])
    sv = jnp.concatenate(chunks, axis=0)
    h = jnp.dot((u * sv).astype(BF16), wout_ref[...], preferred_element_type=F32)
    o_ref[...] = _layer_norm(ALPHA * x + h, g_ref[...], b_ref[...])


def _gmlp_layer(x2, w_in, ln_g, ln_b, w_s, bs_full, w_out, g, b, tm=256):
    t, d = x2.shape
    row = pl.BlockSpec((tm, d), lambda i: (i, 0))
    full = lambda a: pl.BlockSpec(a.shape, lambda i: (0,) * a.ndim)
    args = (w_in, ln_g, ln_b, w_s, bs_full, w_out, g, b)
    return pl.pallas_call(
        functools.partial(_gmlp_kernel, tm=tm),
        grid=(t // tm,),
        in_specs=[row] + [full(a) for a in args],
        out_specs=row,
        out_shape=jax.ShapeDtypeStruct((t, d), F32),
        compiler_params=_params("parallel"),
        name="gmlp_layer",
    )(x2, *args)


def _router_kernel(x_ref, wr_ref, o_ref):
    logits = lax.dot_general(wr_ref[...], x_ref[...], NT_DIMS, preferred_element_type=F32,
                             precision=lax.Precision.HIGHEST)
    m = jnp.max(logits, axis=0, keepdims=True)
    p = jnp.exp(logits - m)
    o_ref[0] = p / jnp.sum(p, axis=0, keepdims=True)


def _router(x2, w_router_t, batch, seq, tm=1024):
    d = x2.shape[1]
    ns = seq // tm
    return pl.pallas_call(
        _router_kernel,
        grid=(batch, ns),
        in_specs=[
            pl.BlockSpec((tm, d), lambda b, i: (b * ns + i, 0)),
            pl.BlockSpec((N_EXPERTS, d), lambda b, i: (0, 0)),
        ],
        out_specs=pl.BlockSpec((1, N_EXPERTS, tm), lambda b, i: (b, 0, i)),
        out_shape=jax.ShapeDtypeStruct((batch, N_EXPERTS, seq), F32),
        compiler_params=_params("parallel", "parallel"),
        name="router",
    )(x2, w_router_t)


def _expert_kernel(x_ref, wg_ref, wu_ref, wd_ref, gate_ref, o_ref):
    f = pl.program_id(2)
    x = x_ref[0, 0]
    a = jnp.dot(x, wg_ref[0, 0].astype(BF16), preferred_element_type=F32)
    u = jnp.dot(x, wu_ref[0, 0].astype(BF16), preferred_element_type=F32)
    hmid = (a * jax.nn.sigmoid(a) * u).astype(BF16)
    y = jnp.dot(hmid, wd_ref[0, 0].astype(BF16), preferred_element_type=F32)

    @pl.when(f == 0)
    def _():
        o_ref[0, 0] = y

    @pl.when(f > 0)
    def _():
        o_ref[0, 0] += y

    @pl.when(f == pl.num_programs(2) - 1)
    def _():
        o_ref[0, 0] = o_ref[0, 0] * gate_ref[0, 0]


def _experts(xg, weights, gate, tf=512):
    wg, wu, wd, layer = weights
    batch, ne, cap, d = xg.shape
    dff = wg.shape[3]
    return pl.pallas_call(
        _expert_kernel,
        grid=(ne, batch, dff // tf),
        in_specs=[
            pl.BlockSpec((1, 1, cap, d), lambda e, b, f: (b, e, 0, 0)),
            pl.BlockSpec((1, 1, d, tf), lambda e, b, f: (layer, e, 0, f)),
            pl.BlockSpec((1, 1, d, tf), lambda e, b, f: (layer, e, 0, f)),
            pl.BlockSpec((1, 1, tf, d), lambda e, b, f: (layer, e, f, 0)),
            pl.BlockSpec((1, 1, cap, 1), lambda e, b, f: (b, e, 0, 0)),
        ],
        out_specs=pl.BlockSpec((1, 1, cap, d), lambda e, b, f: (b, e, 0, 0)),
        out_shape=jax.ShapeDtypeStruct((batch, ne, cap, d), F32),
        compiler_params=_params("parallel", "parallel", "arbitrary"),
        name="experts",
    )(xg, wg, wu, wd, gate)


def _gather_kernel(idx_ref, x_hbm, o_ref, buf, sem, *, cap):
    def issue(r, carry):
        pltpu.make_async_copy(x_hbm.at[pl.ds(idx_ref[0, 0, r], 1), :], buf.at[pl.ds(r, 1), :], sem).start()
        return carry

    lax.fori_loop(0, cap, issue, 0, unroll=8)
    pltpu.make_async_copy(x_hbm.at[pl.ds(0, cap), :], buf, sem).wait()
    o_ref[...] = buf[...].astype(o_ref.dtype)


def _gather_rows(x2, rows, cap):
    g = rows.shape[0]
    d = x2.shape[1]
    return pl.pallas_call(
        functools.partial(_gather_kernel, cap=cap),
        grid=(g,),
        in_specs=[
            pl.BlockSpec((1, 1, cap), lambda i: (i, 0, 0), memory_space=pltpu.SMEM),
            pl.BlockSpec(memory_space=pl.ANY),
        ],
        out_specs=pl.BlockSpec((cap, d), lambda i: (i, 0)),
        out_shape=jax.ShapeDtypeStruct((g * cap, d), BF16),
        scratch_shapes=[pltpu.VMEM((cap, d), F32), pltpu.SemaphoreType.DMA],
        compiler_params=_params("arbitrary"),
        name="gather_rows",
    )(rows, x2)


COMBINE_ROWS_PER_STEP = 8


def _combine_kernel(idx_ref, y_ref, o_hbm, acc_ref, sem, *, cap, seq):
    b = pl.program_id(0)
    e = pl.program_id(1)

    @pl.when(e == 0)
    def _():
        acc_ref[...] = jnp.zeros_like(acc_ref)

    def step(i, carry):
        r0 = pl.multiple_of(i * COMBINE_ROWS_PER_STEP, COMBINE_ROWS_PER_STEP)
        toks = [idx_ref[0, 0, r0 + j] for j in range(COMBINE_ROWS_PER_STEP)]
        rows = [acc_ref[pl.ds(toks[j], 1), :] + y_ref[0, 0, pl.ds(r0 + j, 1), :] for j in range(COMBINE_ROWS_PER_STEP)]
        for j in range(COMBINE_ROWS_PER_STEP):
            acc_ref[pl.ds(toks[j], 1), :] = rows[j]
        return carry

    lax.fori_loop(0, cap // COMBINE_ROWS_PER_STEP, step, 0)

    @pl.when(e == pl.num_programs(1) - 1)
    def _():
        cp = pltpu.make_async_copy(acc_ref, o_hbm.at[pl.ds(pl.multiple_of(b * seq, seq), seq), :], sem)
        cp.start()
        cp.wait()


def _combine(y, idx, seq):
    batch, ne, cap, d = y.shape
    return pl.pallas_call(
        functools.partial(_combine_kernel, cap=cap, seq=seq),
        grid=(batch, ne),
        in_specs=[
            pl.BlockSpec((1, 1, cap), lambda b, e: (b * ne + e, 0, 0), memory_space=pltpu.SMEM),
            pl.BlockSpec((1, 1, cap, d), lambda b, e: (b, e, 0, 0)),
        ],
        out_specs=pl.BlockSpec(memory_space=pl.ANY),
        out_shape=jax.ShapeDtypeStruct((batch * seq, d), F32),
        scratch_shapes=[pltpu.VMEM((seq, d), F32), pltpu.SemaphoreType.DMA],
        compiler_params=_params("arbitrary", "arbitrary"),
        name="combine",
    )(idx.reshape(batch * ne, 1, cap), y)


def _res_ln_kernel(x_ref, h_ref, g_ref, b_ref, o_ref):
    o_ref[...] = _layer_norm(ALPHA * x_ref[...] + h_ref[...], g_ref[...], b_ref[...])


def _res_ln(x2, h2, g, b, tm=1024):
    t, d = x2.shape
    row = pl.BlockSpec((tm, d), lambda i: (i, 0))
    vec = pl.BlockSpec((1, d), lambda i: (0, 0))
    return pl.pallas_call(
        _res_ln_kernel,
        grid=(t // tm,),
        in_specs=[row, row, vec, vec],
        out_specs=row,
        out_shape=jax.ShapeDtypeStruct((t, d), F32),
        compiler_params=_params("parallel"),
        name="res_ln",
    )(x2, h2, g, b)


def _moe_layer(x2, w_router, weights, g, b, batch, seq):
    d = x2.shape[1]
    cap = EC_FACTOR * seq // N_EXPERTS
    aff_t = _router(x2, w_router.astype(F32).T, batch, seq)
    gate, idx = lax.top_k(aff_t, cap)
    rows = idx + (jnp.arange(batch, dtype=idx.dtype) * seq)[:, None, None]
    xg = _gather_rows(x2, rows.reshape(batch * N_EXPERTS, 1, cap), cap).reshape(batch, N_EXPERTS, cap, d)
    y = _experts(xg, weights, gate[..., None])
    return _res_ln(x2, _combine(y, idx, seq), g, b)


def _lambda_init(layer_number):
    return 0.8 - 0.6 * float(np.exp(-0.3 * (layer_number - 1)))


def _row(v):
    return v.astype(F32).reshape(1, -1)


def kernel(x, w_in_ab, rpb_a, lambda_qk, subln_g, w_out_ab, w_in_c, ln_v_g, ln_v_b, w_s, b_s,
           w_out_c, ln_mix_g, ln_mix_b, w_router, w_gate, w_up, w_down, ln_ffn_g, ln_ffn_b):
    batch, seq, d = x.shape
    x2 = x.astype(F32).reshape(batch * seq, d)
    v_a = slice(2 * WIDTH_A, 3 * WIDTH_A)
    v_b = slice(3 * WIDTH_A + 2 * WIDTH_B, 3 * WIDTH_A + 3 * WIDTH_B)
    col_scale = np.ones((1, 2 * WIDTH_A + 2 * WIDTH_B), np.float32)
    col_scale[:, :WIDTH_A] = QK_SCALE * LOG2E
    col_scale[:, 2 * WIDTH_A:2 * WIDTH_A + WIDTH_B] = QK_SCALE * LOG2E
    col_scale = jnp.asarray(col_scale)

    for l in range(DEPTH):
        i = l // 2
        if l % 2 == 0:
            lam_init = _lambda_init(l + 1)
            lf = lambda_qk[i].astype(F32)
            lam = jnp.exp(jnp.sum(lf[0] * lf[1])) - jnp.exp(jnp.sum(lf[2] * lf[3])) + lam_init
            w_in = w_in_ab[i].astype(BF16)
            qk = _project(x2, jnp.concatenate([w_in[:, :v_a.start], w_in[:, v_a.stop:v_b.start]], axis=1), col_scale)
            vt = _project_t(x2, jnp.concatenate([w_in[:, v_a], w_in[:, v_b]], axis=1).T)
            oa = _natten(qk, vt, _natten_bias(rpb_a[i], seq // GRID_W), batch, seq)
            ob = _diff_attention(qk, vt, lam.reshape(1), subln_g[i], lam_init, batch, seq)
            x2 = _attn_out(oa, ob, w_out_ab[i].astype(BF16), x2, _row(ln_mix_g[l]), _row(ln_mix_b[l]))
        else:
            gw = w_in_c.shape[2] // 2 // GMLP_GROUPS
            bs_full = jnp.repeat(b_s[i].astype(F32).T, gw, axis=1)
            x2 = _gmlp_layer(x2, w_in_c[i].astype(BF16), _row(ln_v_g[i]), _row(ln_v_b[i]),
                             w_s[i].astype(BF16), bs_full, w_out_c[i].astype(BF16),
                             _row(ln_mix_g[l]), _row(ln_mix_b[l]))
        x2 = _moe_layer(x2, w_router[l], (w_gate, w_up, w_down, l),
                        _row(ln_ffn_g[l]), _row(ln_ffn_b[l]), batch, seq)
    return x2.reshape(batch, seq, d).astype(x.dtype)
```

```python
import functools

import jax
import jax.numpy as jnp
import numpy as np
from jax import lax
from jax.experimental import pallas as pl
from jax.experimental.pallas import tpu as pltpu

F32 = jnp.float32
BF16 = jnp.bfloat16

DEPTH = 4
GRID_W = 64
HEAD_DIM = 64
N_HEADS_A = 8
WIN_ROWS = 8
WIN_COLS = 16
WIDTH_A = N_HEADS_A * HEAD_DIM
N_HEADS_B = 4
WIDTH_B = N_HEADS_B * 2 * HEAD_DIM
CHUNK = 128
GMLP_GROUPS = 8
N_EXPERTS = 16
EC_FACTOR = 2
LN_EPS = 1e-5
RMS_EPS = 1e-6
ALPHA = (2.0 * DEPTH) ** 0.25
QK_SCALE = HEAD_DIM ** -0.5
MASK_VALUE = -1e30
LOG2E = float(np.log2(np.e))
SKIP_LOG2 = 60.0
SKIP_MARGIN = 1.02

LANES = 128
VMEM_LIMIT = 56 * 1024 * 1024

NT_DIMS = (((1,), (1,)), ((), ()))


def _params(*sem):
    return pltpu.CompilerParams(dimension_semantics=sem, vmem_limit_bytes=VMEM_LIMIT)


def _layer_norm(y, g, b):
    mu = jnp.mean(y, axis=-1, keepdims=True)
    yc = y - mu
    var = jnp.mean(yc * yc, axis=-1, keepdims=True)
    return yc * lax.rsqrt(var + LN_EPS) * g + b


def _bf16_split(value, parts=3):
    out, rest = [], np.float64(value)
    for _ in range(parts):
        piece = float(np.asarray(rest, np.float32).astype(jnp.bfloat16).astype(np.float64))
        out.append(piece)
        rest = rest - piece
    return out


def _proj_kernel(x_ref, w_ref, s_ref, o_ref):
    acc = jnp.dot(x_ref[...].astype(BF16), w_ref[...], preferred_element_type=F32)
    o_ref[...] = (acc * s_ref[...]).astype(o_ref.dtype)


def _project(x2, w_bf16, col_scale, tm=1024, tn=512):
    t, k = x2.shape
    n = w_bf16.shape[1]
    return pl.pallas_call(
        _proj_kernel,
        grid=(t // tm, n // tn),
        in_specs=[
            pl.BlockSpec((tm, k), lambda i, j: (i, 0)),
            pl.BlockSpec((k, tn), lambda i, j: (0, j)),
            pl.BlockSpec((1, tn), lambda i, j: (0, j)),
        ],
        out_specs=pl.BlockSpec((tm, tn), lambda i, j: (i, j)),
        out_shape=jax.ShapeDtypeStruct((t, n), BF16),
        compiler_params=_params("parallel", "arbitrary"),
        name="qkv_proj",
    )(x2, w_bf16, col_scale)


def _proj_t_kernel(x_ref, wt_ref, o_ref):
    acc = lax.dot_general(wt_ref[...], x_ref[...].astype(BF16), NT_DIMS, preferred_element_type=F32)
    o_ref[...] = acc.astype(o_ref.dtype)


def _project_t(x2, wt_bf16, tm=1024):
    t, k = x2.shape
    n = wt_bf16.shape[0]
    return pl.pallas_call(
        _proj_t_kernel,
        grid=(t // tm,),
        in_specs=[
            pl.BlockSpec((tm, k), lambda i: (i, 0)),
            pl.BlockSpec((n, k), lambda i: (0, 0)),
        ],
        out_specs=pl.BlockSpec((n, tm), lambda i: (0, i)),
        out_shape=jax.ShapeDtypeStruct((n, t), BF16),
        compiler_params=_params("parallel"),
        name="v_proj_t",
    )(x2, wt_bf16)


NAT_ROWS = 4
NAT_KROWS = NAT_ROWS + WIN_ROWS


def _natten_bias(rpb, n_rows):
    n_heads = rpb.shape[0]
    span = 2 * WIN_COLS - 1
    pad_lo = GRID_W - WIN_COLS
    padded = jnp.pad(rpb.astype(F32) * LOG2E, ((0, 0), (0, 0), (pad_lo, 2 * GRID_W - 1 - pad_lo - span)))
    toep = jnp.stack([padded[:, :, GRID_W - 1 - c:2 * GRID_W - 1 - c] for c in range(GRID_W)], axis=3)
    c = np.arange(GRID_W)[None, :]
    kc = np.arange(GRID_W)[:, None]
    c_start = np.clip(c - WIN_COLS // 2, 0, GRID_W - WIN_COLS)
    valid = (kc >= c_start) & (kc < c_start + WIN_COLS)
    toep = jnp.where(valid[None, None], toep, MASK_VALUE)
    masked = jnp.full((n_heads, GRID_W, GRID_W), MASK_VALUE, F32)
    n_blk = n_rows // NAT_ROWS
    kinds = []
    for blk in (0, 1, n_blk - 1):
        r0 = blk * NAT_ROWS
        ks = int(np.clip(r0 - WIN_ROWS // 2, 0, n_rows - NAT_KROWS))
        cols = []
        for a in range(NAT_ROWS):
            r = r0 + a
            rs = int(np.clip(r - WIN_ROWS // 2, 0, n_rows - WIN_ROWS))
            rows = []
            for i in range(NAT_KROWS):
                kr = ks + i
                rows.append(toep[:, kr - r + WIN_ROWS - 1] if rs <= kr < rs + WIN_ROWS else masked)
            cols.append(jnp.concatenate(rows, axis=1))
        kinds.append(jnp.concatenate(cols, axis=2))
    b = jnp.stack(kinds, axis=0)
    nk, nq = b.shape[2], b.shape[3]
    b = b.reshape(3, n_heads // 2, 2, nk, nq).transpose(0, 1, 3, 2, 4)
    return b.reshape(3, n_heads // 2, nk, 2 * nq)


def _natten_kernel(q_ref, k_ref, vt_ref, bias_ref, o_ref, raw_ref, cmax_ref, *, n_rows):
    nq = NAT_ROWS * GRID_W
    nk = NAT_KROWS * GRID_W
    n_blk = n_rows // NAT_ROWS
    lane = lax.broadcasted_iota(jnp.int32, (nq, LANES), 1)
    low = lane < HEAD_DIM

    def key_start(blk):
        ks = jnp.clip(blk * NAT_ROWS - WIN_ROWS // 2, 0, n_rows - NAT_KROWS)
        return pl.multiple_of(ks * GRID_W, NAT_ROWS * GRID_W)

    def scores(blk, slot):
        q = q_ref[pl.ds(pl.multiple_of(blk * nq, nq), nq), :]
        zero = jnp.zeros_like(q)
        qx = jnp.concatenate([jnp.where(low, q, zero), jnp.where(low, zero, q)], axis=0)
        k = k_ref[pl.ds(key_start(blk), nk), :]
        kind = jnp.where(blk == 0, 0, jnp.where(blk == n_blk - 1, 2, 1))
        raw = lax.dot_general(k, qx, NT_DIMS, preferred_element_type=F32) + bias_ref[kind, 0]
        raw_ref[slot] = raw
        cmax_ref[slot] = jnp.max(raw, axis=0, keepdims=True)

    def absorb(blk, slot):
        vt = vt_ref[:, pl.ds(key_start(blk), nk)]
        p = jnp.exp2(raw_ref[slot] - cmax_ref[slot])
        l = jnp.sum(p, axis=0, keepdims=True)
        res = jnp.dot(vt, p.astype(BF16), preferred_element_type=F32) / l
        o_t = jnp.concatenate([res[:HEAD_DIM, :nq], res[HEAD_DIM:, nq:]], axis=0)
        o_ref[pl.ds(pl.multiple_of(blk * nq, nq), nq), :] = o_t.T.astype(o_ref.dtype)

    scores(0, 0)

    def body(u, carry):
        blk = 2 * u
        scores(blk + 1, 1)
        absorb(blk, 0)
        scores(blk + 2, 0)
        absorb(blk + 1, 1)
        return carry

    assert n_blk % 2 == 0
    lax.fori_loop(0, n_blk // 2 - 1, body, 0)
    scores(n_blk - 1, 1)
    absorb(n_blk - 2, 0)
    absorb(n_blk - 1, 1)


def _natten(qk, vt, bias, batch, seq):
    n_rows = seq // GRID_W
    hp = N_HEADS_A // 2
    nq = NAT_ROWS * GRID_W
    nk = NAT_KROWS * GRID_W
    return pl.pallas_call(
        functools.partial(_natten_kernel, n_rows=n_rows),
        grid=(batch, hp),
        in_specs=[
            pl.BlockSpec((seq, LANES), lambda b, h: (b, h)),
            pl.BlockSpec((seq, LANES), lambda b, h: (b, hp + h)),
            pl.BlockSpec((LANES, seq), lambda b, h: (h, b)),
            pl.BlockSpec((3, 1, nk, 2 * nq), lambda b, h: (0, h, 0, 0)),
        ],
        out_specs=pl.BlockSpec((seq, LANES), lambda b, h: (b, h)),
        out_shape=jax.ShapeDtypeStruct((batch * seq, WIDTH_A), BF16),
        scratch_shapes=[pltpu.VMEM((2, nk, 2 * nq), F32), pltpu.VMEM((2, 1, 2 * nq), F32)],
        compiler_params=_params("parallel", "parallel"),
        name="natten",
    )(qk, qk, vt, bias)


def _diff_kernel(sl_ref, lam_ref, q_ref, k_ref, vt_ref, f_ref, pen_ref, g_ref, o_ref,
                 qop_ref, raw_ref, cmax_ref, coff_ref, acc_ref, knorm_ref, lfin_ref, *, tq, tk, out_scale):
    h = pl.program_id(1)
    qi = pl.program_id(2)
    n_kt = k_ref.shape[0] // tk
    sl = sl_ref[h]
    a1, a2, a3 = _bf16_split(LOG2E)

    q = q_ref[...]
    lane_q = lax.broadcasted_iota(jnp.int32, q.shape, 1)
    low_q = lane_q < HEAD_DIM
    zero = jnp.zeros_like(q)
    q1 = jnp.where(low_q, q, zero)
    q2 = jnp.where(low_q, zero, q)

    def q_ops(sign):
        if sign == 0:
            f = zero
        else:
            f = jnp.where(lane_q == 0, sign * a1,
                          jnp.where(lane_q == 1, sign * a2, jnp.where(lane_q == 2, sign * a3, 0.0))).astype(BF16)
        return jnp.concatenate([jnp.concatenate([q1, f], axis=1), jnp.concatenate([q2, f], axis=1)], axis=0)

    qop_ref[0] = q_ops(1.0)
    qop_ref[1] = q_ops(-1.0)
    fk = f_ref[0]
    pos = lax.broadcasted_iota(jnp.int32, (1, 2 * tq), 1)
    iq = (qi * tq + jnp.where(pos < tq, pos, pos - tq)).astype(F32)

    kt_mix = (qi * tq) // tk
    sel = qi - kt_mix * (tk // tq)

    @pl.when(qi == 0)
    def _():
        def tile_norm(kt, best):
            kk = k_ref[pl.ds(pl.multiple_of(kt * tk, tk), tk), :].astype(F32)
            kk = kk * kk
            low_k = lax.broadcasted_iota(jnp.int32, kk.shape, 1) < HEAD_DIM
            n1 = jnp.sum(jnp.where(low_k, kk, 0.0), axis=1, keepdims=True)
            n2 = jnp.sum(jnp.where(low_k, 0.0, kk), axis=1, keepdims=True)
            return jnp.maximum(best, jnp.max(jnp.maximum(n1, n2), axis=0, keepdims=True))

        best = lax.fori_loop(0, n_kt, tile_norm, jnp.zeros((1, 1), F32))
        knorm_ref[...] = jnp.sqrt(best)

    qq = q.astype(F32)
    qq = qq * qq
    qn1 = jnp.sum(jnp.where(low_q, qq, 0.0), axis=1, keepdims=True)
    qn2 = jnp.sum(jnp.where(low_q, 0.0, qq), axis=1, keepdims=True)
    qnorm = jnp.sqrt(jnp.max(jnp.maximum(qn1, qn2), axis=0, keepdims=True))
    reach = (SKIP_LOG2 + 2.0 * SKIP_MARGIN * qnorm * knorm_ref[...]) / sl
    q_lo = (qi * tq).astype(F32)
    kt_lo = jnp.maximum(jnp.floor((q_lo - reach - 1.0) / tk), 0.0).astype(jnp.int32)[0, 0]
    kt_hi = jnp.minimum(jnp.floor((q_lo + (tq + reach)) / tk), n_kt - 1.0).astype(jnp.int32)[0, 0]
    n_steps = kt_hi - kt_lo + 1

    def load_k(kt):
        off = pl.multiple_of(kt * tk, tk)
        return jnp.concatenate([k_ref[pl.ds(off, tk), :], fk], axis=1), vt_ref[:, pl.ds(off, tk)]

    def tile_of(s):
        t = kt_lo + jnp.minimum(s, n_steps - 1) - 1
        return jnp.where(s == 0, kt_mix, t + (t >= kt_mix).astype(jnp.int32))

    def scores(s, slot):
        kt = tile_of(s)
        kx, _ = load_k(kt)
        raw = lax.dot_general(kx, qop_ref[(kt > kt_mix).astype(jnp.int32)], NT_DIMS, preferred_element_type=F32)
        c = -sl * jnp.abs(iq - (kt * tk + tk // 2).astype(F32))
        raw_ref[slot] = raw
        cmax_ref[slot] = jnp.max(raw, axis=0, keepdims=True) + c
        coff_ref[slot] = c

    def absorb(s, slot, state):
        m_prev, l_prev = state
        _, vt = load_k(tile_of(s))
        m_new = jnp.maximum(m_prev, cmax_ref[slot])
        a = jnp.exp2(m_prev - m_new)
        p = jnp.exp2(raw_ref[slot] - (m_new - coff_ref[slot]))
        l_new = a * l_prev + jnp.sum(p, axis=0, keepdims=True)
        acc_ref[...] = a * acc_ref[...] + jnp.dot(vt, p.astype(BF16), preferred_element_type=F32)
        return m_new, l_new

    acc_ref[...] = jnp.zeros_like(acc_ref)
    state = (jnp.full((1, 2 * tq), -jnp.inf, F32), jnp.zeros((1, 2 * tq), F32))

    kx, _ = load_k(kt_mix)
    pen = pen_ref[0, sel]
    raw = lax.dot_general(kx, q_ops(0), NT_DIMS, preferred_element_type=F32) - jnp.concatenate([pen, pen], axis=1)
    raw_ref[0] = raw
    cmax_ref[0] = jnp.max(raw, axis=0, keepdims=True)
    coff_ref[0] = jnp.zeros((1, 2 * tq), F32)

    def body(u, st):
        s = 2 * u
        scores(s + 1, 1)
        st = absorb(s, 0, st)
        scores(s + 2, 0)
        return absorb(s + 1, 1, st)

    m_run, l_run = lax.fori_loop(0, n_steps // 2, body, state)
    lfin_ref[...] = l_run

    @pl.when(n_steps % 2 == 1)
    def _():
        _, l_last = absorb(n_steps - 1, 0, (m_run, l_run))
        lfin_ref[...] = l_last

    on = acc_ref[...] / lfin_ref[...]
    o = (on[:, :tq] - lam_ref[0] * on[:, tq:]).T
    o = o * lax.rsqrt(jnp.mean(o * o, axis=-1, keepdims=True) + RMS_EPS)
    o_ref[...] = (o * g_ref[0] * out_scale).astype(o_ref.dtype)


def _diff_tables(slopes, tq, tk):
    j = np.arange(tk)
    feat = slopes[:, None] * (j - tk // 2)[None, :]
    feat = np.broadcast_to(feat[:, :, None], (len(slopes), tk, LANES))
    i = np.arange(tq)
    pos = np.arange(tk // tq)[:, None, None] * tq + i[None, None, :]
    dist = np.abs(pos - j[None, :, None])
    pen = (slopes * LOG2E)[:, None, None, None] * dist[None]
    return jnp.asarray(feat, BF16), jnp.asarray(pen, F32)


def _diff_attention(qk, vt, lam, subln_g, lam_init, batch, seq, tq=256, tk=512):
    nq = seq // tq
    base = 2 * WIDTH_A // LANES
    hb = N_HEADS_B
    slopes = 2.0 ** (-8.0 * (np.arange(hb) + 1) / hb)
    feat, pen = _diff_tables(slopes, tq, tk)
    sl = jnp.asarray(slopes * LOG2E, F32)
    smem = pl.BlockSpec(memory_space=pltpu.SMEM)
    return pl.pallas_call(
        functools.partial(_diff_kernel, tq=tq, tk=tk, out_scale=1.0 - lam_init),
        grid=(batch, hb, nq),
        in_specs=[
            smem, smem,
            pl.BlockSpec((tq, LANES), lambda b, h, i: (b * nq + i, base + h)),
            pl.BlockSpec((seq, LANES), lambda b, h, i: (b, base + hb + h)),
            pl.BlockSpec((LANES, seq), lambda b, h, i: (WIDTH_A // LANES + h, b)),
            pl.BlockSpec((1, tk, LANES), lambda b, h, i: (h, 0, 0)),
            pl.BlockSpec((1, tk // tq, tk, tq), lambda b, h, i: (h, 0, 0, 0)),
            pl.BlockSpec((1, 1, LANES), lambda b, h, i: (h, 0, 0)),
        ],
        out_specs=pl.BlockSpec((tq, LANES), lambda b, h, i: (b * nq + i, h)),
        out_shape=jax.ShapeDtypeStruct((batch * seq, WIDTH_B), BF16),
        scratch_shapes=[
            pltpu.VMEM((2, 2 * tq, 2 * LANES), BF16),
            pltpu.VMEM((2, tk, 2 * tq), F32),
            pltpu.VMEM((2, 1, 2 * tq), F32),
            pltpu.VMEM((2, 1, 2 * tq), F32),
            pltpu.VMEM((LANES, 2 * tq), F32),
            pltpu.VMEM((1, 1), F32),
            pltpu.VMEM((1, 2 * tq), F32),
        ],
        compiler_params=_params("arbitrary", "arbitrary", "arbitrary"),
        name="diff_attn",
    )(sl, lam, qk, qk, vt, feat, pen, subln_g.astype(F32).reshape(hb, 1, LANES))


def _attn_out_kernel(oa_ref, ob_ref, wa_ref, wb_ref, x_ref, g_ref, b_ref, o_ref):
    h = jnp.dot(oa_ref[...], wa_ref[...], preferred_element_type=F32)
    h = h + jnp.dot(ob_ref[...], wb_ref[...], preferred_element_type=F32)
    o_ref[...] = _layer_norm(ALPHA * x_ref[...] + h, g_ref[...], b_ref[...])


def _attn_out(oa, ob, w_out_bf16, x2, g, b, tm=512):
    t, d = x2.shape
    wa, wb = w_out_bf16[:WIDTH_A], w_out_bf16[WIDTH_A:]
    row = lambda w: pl.BlockSpec((tm, w), lambda i: (i, 0))
    full = lambda a: pl.BlockSpec(a.shape, lambda i: (0,) * a.ndim)
    return pl.pallas_call(
        _attn_out_kernel,
        grid=(t // tm,),
        in_specs=[row(WIDTH_A), row(WIDTH_B), full(wa), full(wb), row(d), full(g), full(b)],
        out_specs=row(d),
        out_shape=jax.ShapeDtypeStruct((t, d), F32),
        compiler_params=_params("parallel"),
        name="attn_out_ln",
    )(oa, ob, wa, wb, x2, g, b)


def _gmlp_kernel(x_ref, win_ref, lg_ref, lb_ref, ws_ref, bs_ref, wout_ref, g_ref, b_ref, o_ref, *, tm):
    x = x_ref[...]
    width = wout_ref.shape[0]
    gw = width // GMLP_GROUPS
    z = jnp.dot(x.astype(BF16), win_ref[...], preferred_element_type=F32)
    z = 0.5 * z * (1.0 + lax.erf(z * (2.0 ** -0.5)))
    u = z[:, :width]
    v = _layer_norm(z[:, width:], lg_ref[...], lb_ref[...]).astype(BF16)
    chunks = []
    for n in range(tm // CHUNK):
        groups = []
        for gi in range(GMLP_GROUPS):
            vg = v[n * CHUNK:(n + 1) * CHUNK, gi * gw:(gi + 1) * gw]
            groups.append(jnp.dot(ws_ref[gi], vg, preferred_element_type=F32))
        chunks.append(jnp.concatenate(groups, axis=1) + bs_ref[...])
    sv = jnp.concatenate(chunks, axis=0)
    h = jnp.dot((u * sv).astype(BF16), wout_ref[...], preferred_element_type=F32)
    o_ref[...] = _layer_norm(ALPHA * x + h, g_ref[...], b_ref[...])


def _gmlp_layer(x2, w_in, ln_g, ln_b, w_s, bs_full, w_out, g, b, tm=256):
    t, d = x2.shape
    row = pl.BlockSpec((tm, d), lambda i: (i, 0))
    full = lambda a: pl.BlockSpec(a.shape, lambda i: (0,) * a.ndim)
    args = (w_in, ln_g, ln_b, w_s, bs_full, w_out, g, b)
    return pl.pallas_call(
        functools.partial(_gmlp_kernel, tm=tm),
        grid=(t // tm,),
        in_specs=[row] + [full(a) for a in args],
        out_specs=row,
        out_shape=jax.ShapeDtypeStruct((t, d), F32),
        compiler_params=_params("parallel"),
        name="gmlp_layer",
    )(x2, *args)


def _router_kernel(x_ref, wr_ref, o_ref):
    logits = lax.dot_general(wr_ref[...], x_ref[...], NT_DIMS, preferred_element_type=F32,
                             precision=lax.Precision.HIGHEST)
    m = jnp.max(logits, axis=0, keepdims=True)
    p = jnp.exp(logits - m)
    o_ref[0] = p / jnp.sum(p, axis=0, keepdims=True)


def _router(x2, w_router_t, batch, seq, tm=1024):
    d = x2.shape[1]
    ns = seq // tm
    return pl.pallas_call(
        _router_kernel,
        grid=(batch, ns),
        in_specs=[
            pl.BlockSpec((tm, d), lambda b, i: (b * ns + i, 0)),
            pl.BlockSpec((N_EXPERTS, d), lambda b, i: (0, 0)),
        ],
        out_specs=pl.BlockSpec((1, N_EXPERTS, tm), lambda b, i: (b, 0, i)),
        out_shape=jax.ShapeDtypeStruct((batch, N_EXPERTS, seq), F32),
        compiler_params=_params("parallel", "parallel"),
        name="router",
    )(x2, w_router_t)


def _route_kernel(aff_ref, idx_ref, gate_ref, thr_ref, *, cap):
    ne, nt = aff_ref.shape[1], aff_ref.shape[2]
    a_all = aff_ref[0]
    u_all = pltpu.bitcast(a_all, jnp.int32)

    def add_bit(i, thr):
        cand = thr | jnp.left_shift(1, 30 - i)
        count = jnp.sum(jnp.where(u_all >= cand, 1.0, 0.0), axis=(1, 2), keepdims=True)
        return jnp.where(count >= cap, cand, thr)

    thr_ref[...] = lax.fori_loop(0, 31, add_bit, jnp.zeros((ne, 1, 1), jnp.int32))

    lane = lax.broadcasted_iota(jnp.int32, (LANES, LANES), 1)
    sub = lax.broadcasted_iota(jnp.int32, (LANES, LANES), 0)
    tri_incl = jnp.where(sub <= lane, 1.0, 0.0).astype(BF16)
    r64 = lax.broadcasted_iota(jnp.int32, (nt, nt), 0)
    c64 = lax.broadcasted_iota(jnp.int32, (nt, nt), 1)
    low_strict = jnp.where(c64 < r64, 1.0, 0.0).astype(BF16)
    up_incl = jnp.where(r64 <= c64, 1.0, 0.0).astype(BF16)
    ones_rows = jnp.ones((8, LANES), BF16)
    slot = lax.broadcasted_iota(jnp.int32, (cap, 1), 0).astype(F32)
    lane_nt = lax.broadcasted_iota(jnp.int32, (cap, nt), 1)
    lane_f = lax.broadcasted_iota(jnp.int32, (cap, LANES), 1).astype(F32)

    def cumsum_tokens(mask_f32):
        mb = mask_f32.astype(BF16)
        local = jnp.dot(mb, tri_incl, preferred_element_type=F32)
        n_b = jnp.broadcast_to(jnp.sum(mask_f32, axis=1, keepdims=True), mask_f32.shape)
        before = jnp.dot(low_strict, n_b.astype(BF16), preferred_element_type=F32)
        return before + local, local

    def one_expert(e, carry):
        a = aff_ref[0, e]
        u = pltpu.bitcast(a, jnp.int32)
        t = thr_ref[e]
        gt = u > t
        eq = u == t
        need = cap - jnp.sum(jnp.where(gt, 1.0, 0.0), axis=(0, 1), keepdims=True)
        eq_f = jnp.where(eq, 1.0, 0.0)
        eq_incl, _ = cumsum_tokens(eq_f)
        sel = jnp.logical_or(gt, jnp.logical_and(eq, eq_incl - eq_f < need))
        sel_f = jnp.where(sel, 1.0, 0.0)
        _, local = cumsum_tokens(sel_f)
        n_row = lax.dot_general(ones_rows, sel_f.astype(BF16), NT_DIMS, preferred_element_type=F32)[:1]
        incl_row = jnp.dot(jnp.broadcast_to(n_row, (8, nt)).astype(BF16), up_incl, preferred_element_type=F32)[:1]
        excl_row = incl_row - n_row
        krow = jnp.sum(jnp.where(incl_row <= slot, 1.0, 0.0), axis=1, keepdims=True)
        onehot = jnp.where(lane_nt == krow.astype(jnp.int32), 1.0, 0.0)
        rank = slot - jnp.sum(onehot * excl_row, axis=1, keepdims=True)
        row_counts = jnp.dot(onehot.astype(BF16), local.astype(BF16), preferred_element_type=F32)
        pos = jnp.sum(jnp.where(row_counts <= rank, 1.0, 0.0), axis=1, keepdims=True)
        idx_ref[0, e] = (krow * LANES + pos).astype(jnp.int32)
        row_aff = jnp.dot(onehot, a, preferred_element_type=F32, precision=lax.Precision.HIGHEST)
        gate_ref[0, e] = jnp.sum(jnp.where(lane_f == pos, row_aff, 0.0), axis=1, keepdims=True)
        return carry

    lax.fori_loop(0, ne, one_expert, 0)


def _route(aff_t, cap):
    batch, ne, seq = aff_t.shape
    nt = seq // LANES
    out = jax.ShapeDtypeStruct((batch, ne, cap, 1), jnp.int32), jax.ShapeDtypeStruct((batch, ne, cap, 1), F32)
    spec = pl.BlockSpec((1, ne, cap, 1), lambda b: (b, 0, 0, 0))
    return pl.pallas_call(
        functools.partial(_route_kernel, cap=cap),
        grid=(batch,),
        in_specs=[pl.BlockSpec((1, ne, nt, LANES), lambda b: (b, 0, 0, 0))],
        out_specs=(spec, spec),
        out_shape=out,
        scratch_shapes=[pltpu.VMEM((ne, 1, 1), jnp.int32)],
        compiler_params=_params("parallel"),
        name="route",
    )(aff_t.reshape(batch, ne, nt, LANES))


def _expert_kernel(x_ref, wg_ref, wu_ref, wd_ref, gate_ref, o_ref):
    f = pl.program_id(2)
    x = x_ref[0, 0]
    a = jnp.dot(x, wg_ref[0, 0].astype(BF16), preferred_element_type=F32)
    u = jnp.dot(x, wu_ref[0, 0].astype(BF16), preferred_element_type=F32)
    hmid = (a * jax.nn.sigmoid(a) * u).astype(BF16)
    y = jnp.dot(hmid, wd_ref[0, 0].astype(BF16), preferred_element_type=F32)

    @pl.when(f == 0)
    def _():
        o_ref[0, 0] = y

    @pl.when(f > 0)
    def _():
        o_ref[0, 0] += y

    @pl.when(f == pl.num_programs(2) - 1)
    def _():
        o_ref[0, 0] = o_ref[0, 0] * gate_ref[0, 0]


def _experts(xg, weights, gate, tf=512):
    wg, wu, wd, layer = weights
    batch, ne, cap, d = xg.shape
    dff = wg.shape[3]
    return pl.pallas_call(
        _expert_kernel,
        grid=(ne, batch, dff // tf),
        in_specs=[
            pl.BlockSpec((1, 1, cap, d), lambda e, b, f: (b, e, 0, 0)),
            pl.BlockSpec((1, 1, d, tf), lambda e, b, f: (layer, e, 0, f)),
            pl.BlockSpec((1, 1, d, tf), lambda e, b, f: (layer, e, 0, f)),
            pl.BlockSpec((1, 1, tf, d), lambda e, b, f: (layer, e, f, 0)),
            pl.BlockSpec((1, 1, cap, 1), lambda e, b, f: (b, e, 0, 0)),
        ],
        out_specs=pl.BlockSpec((1, 1, cap, d), lambda e, b, f: (b, e, 0, 0)),
        out_shape=jax.ShapeDtypeStruct((batch, ne, cap, d), F32),
        compiler_params=_params("parallel", "parallel", "arbitrary"),
        name="experts",
    )(xg, wg, wu, wd, gate)


def _gather_kernel(idx_ref, x_hbm, o_ref, buf, sem, *, cap):
    def issue(r, carry):
        pltpu.make_async_copy(x_hbm.at[pl.ds(idx_ref[0, 0, r], 1), :], buf.at[pl.ds(r, 1), :], sem).start()
        return carry

    lax.fori_loop(0, cap, issue, 0, unroll=8)
    pltpu.make_async_copy(x_hbm.at[pl.ds(0, cap), :], buf, sem).wait()
    o_ref[...] = buf[...].astype(o_ref.dtype)


def _gather_rows(x2, rows, cap):
    g = rows.shape[0]
    d = x2.shape[1]
    return pl.pallas_call(
        functools.partial(_gather_kernel, cap=cap),
        grid=(g,),
        in_specs=[
            pl.BlockSpec((1, 1, cap), lambda i: (i, 0, 0), memory_space=pltpu.SMEM),
            pl.BlockSpec(memory_space=pl.ANY),
        ],
        out_specs=pl.BlockSpec((cap, d), lambda i: (i, 0)),
        out_shape=jax.ShapeDtypeStruct((g * cap, d), BF16),
        scratch_shapes=[pltpu.VMEM((cap, d), F32), pltpu.SemaphoreType.DMA],
        compiler_params=_params("arbitrary"),
        name="gather_rows",
    )(rows, x2)


COMBINE_ROWS_PER_STEP = 8


def _combine_kernel(idx_ref, y_ref, o_hbm, acc_ref, sem, *, cap, seq):
    b = pl.program_id(0)
    e = pl.program_id(1)

    @pl.when(e == 0)
    def _():
        acc_ref[...] = jnp.zeros_like(acc_ref)

    def step(i, carry):
        r0 = pl.multiple_of(i * COMBINE_ROWS_PER_STEP, COMBINE_ROWS_PER_STEP)
        toks = [idx_ref[0, 0, r0 + j] for j in range(COMBINE_ROWS_PER_STEP)]
        rows = [acc_ref[pl.ds(toks[j], 1), :] + y_ref[0, 0, pl.ds(r0 + j, 1), :] for j in range(COMBINE_ROWS_PER_STEP)]
        for j in range(COMBINE_ROWS_PER_STEP):
            acc_ref[pl.ds(toks[j], 1), :] = rows[j]
        return carry

    lax.fori_loop(0, cap // COMBINE_ROWS_PER_STEP, step, 0)

    @pl.when(e == pl.num_programs(1) - 1)
    def _():
        cp = pltpu.make_async_copy(acc_ref, o_hbm.at[pl.ds(pl.multiple_of(b * seq, seq), seq), :], sem)
        cp.start()
        cp.wait()


def _combine(y, idx, seq):
    batch, ne, cap, d = y.shape
    return pl.pallas_call(
        functools.partial(_combine_kernel, cap=cap, seq=seq),
        grid=(batch, ne),
        in_specs=[
            pl.BlockSpec((1, 1, cap), lambda b, e: (b * ne + e, 0, 0), memory_space=pltpu.SMEM),
            pl.BlockSpec((1, 1, cap, d), lambda b, e: (b, e, 0, 0)),
        ],
        out_specs=pl.BlockSpec(memory_space=pl.ANY),
        out_shape=jax.ShapeDtypeStruct((batch * seq, d), F32),
        scratch_shapes=[pltpu.VMEM((seq, d), F32), pltpu.SemaphoreType.DMA],
        compiler_params=_params("arbitrary", "arbitrary"),
        name="combine",
    )(idx.reshape(batch * ne, 1, cap), y)


def _res_ln_kernel(x_ref, h_ref, g_ref, b_ref, o_ref):
    o_ref[...] = _layer_norm(ALPHA * x_ref[...] + h_ref[...], g_ref[...], b_ref[...])


def _res_ln(x2, h2, g, b, tm=1024):
    t, d = x2.shape
    row = pl.BlockSpec((tm, d), lambda i: (i, 0))
    vec = pl.BlockSpec((1, d), lambda i: (0, 0))
    return pl.pallas_call(
        _res_ln_kernel,
        grid=(t // tm,),
        in_specs=[row, row, vec, vec],
        out_specs=row,
        out_shape=jax.ShapeDtypeStruct((t, d), F32),
        compiler_params=_params("parallel"),
        name="res_ln",
    )(x2, h2, g, b)


def _moe_layer(x2, w_router, weights, g, b, batch, seq):
    d = x2.shape[1]
    cap = EC_FACTOR * seq // N_EXPERTS
    aff_t = _router(x2, w_router.astype(F32).T, batch, seq)
    idx, gate = _route(aff_t, cap)
    idx = idx.reshape(batch, N_EXPERTS, cap)
    rows = idx + (jnp.arange(batch, dtype=idx.dtype) * seq)[:, None, None]
    xg = _gather_rows(x2, rows.reshape(batch * N_EXPERTS, 1, cap), cap).reshape(batch, N_EXPERTS, cap, d)
    y = _experts(xg, weights, gate)
    return _res_ln(x2, _combine(y, idx, seq), g, b)


def _lambda_init(layer_number):
    return 0.8 - 0.6 * float(np.exp(-0.3 * (layer_number - 1)))


def _row(v):
    return v.astype(F32).reshape(1, -1)


def kernel(x, w_in_ab, rpb_a, lambda_qk, subln_g, w_out_ab, w_in_c, ln_v_g, ln_v_b, w_s, b_s,
           w_out_c, ln_mix_g, ln_mix_b, w_router, w_gate, w_up, w_down, ln_ffn_g, ln_ffn_b):
    batch, seq, d = x.shape
    x2 = x.astype(F32).reshape(batch * seq, d)
    v_a = slice(2 * WIDTH_A, 3 * WIDTH_A)
    v_b = slice(3 * WIDTH_A + 2 * WIDTH_B, 3 * WIDTH_A + 3 * WIDTH_B)
    col_scale = np.ones((1, 2 * WIDTH_A + 2 * WIDTH_B), np.float32)
    col_scale[:, :WIDTH_A] = QK_SCALE * LOG2E
    col_scale[:, 2 * WIDTH_A:2 * WIDTH_A + WIDTH_B] = QK_SCALE * LOG2E
    col_scale = jnp.asarray(col_scale)

    for l in range(DEPTH):
        i = l // 2
        if l % 2 == 0:
            lam_init = _lambda_init(l + 1)
            lf = lambda_qk[i].astype(F32)
            lam = jnp.exp(jnp.sum(lf[0] * lf[1])) - jnp.exp(jnp.sum(lf[2] * lf[3])) + lam_init
            w_in = w_in_ab[i].astype(BF16)
            qk = _project(x2, jnp.concatenate([w_in[:, :v_a.start], w_in[:, v_a.stop:v_b.start]], axis=1), col_scale)
            vt = _project_t(x2, jnp.concatenate([w_in[:, v_a], w_in[:, v_b]], axis=1).T)
            oa = _natten(qk, vt, _natten_bias(rpb_a[i], seq // GRID_W), batch, seq)
            ob = _diff_attention(qk, vt, lam.reshape(1), subln_g[i], lam_init, batch, seq)
            x2 = _attn_out(oa, ob, w_out_ab[i].astype(BF16), x2, _row(ln_mix_g[l]), _row(ln_mix_b[l]))
        else:
            gw = w_in_c.shape[2] // 2 // GMLP_GROUPS
            bs_full = jnp.repeat(b_s[i].astype(F32).T, gw, axis=1)
            x2 = _gmlp_layer(x2, w_in_c[i].astype(BF16), _row(ln_v_g[i]), _row(ln_v_b[i]),
                             w_s[i].astype(BF16), bs_full, w_out_c[i].astype(BF16),
                             _row(ln_mix_g[l]), _row(ln_mix_b[l]))
        x2 = _moe_layer(x2, w_router[l], (w_gate, w_up, w_down, l),
                        _row(ln_ffn_g[l]), _row(ln_ffn_b[l]), batch, seq)
    return x2.reshape(batch, seq, d).astype(x.dtype)
```

```python
import functools

import jax
import jax.numpy as jnp
import numpy as np
from jax import lax
from jax.experimental import pallas as pl
from jax.experimental.pallas import tpu as pltpu

F32 = jnp.float32
BF16 = jnp.bfloat16

DEPTH = 4
GRID_W = 64
HEAD_DIM = 64
N_HEADS_A = 8
WIN_ROWS = 8
WIN_COLS = 16
WIDTH_A = N_HEADS_A * HEAD_DIM
N_HEADS_B = 4
WIDTH_B = N_HEADS_B * 2 * HEAD_DIM
CHUNK = 128
GMLP_GROUPS = 8
N_EXPERTS = 16
EC_FACTOR = 2
LN_EPS = 1e-5
RMS_EPS = 1e-6
ALPHA = (2.0 * DEPTH) ** 0.25
QK_SCALE = HEAD_DIM ** -0.5
MASK_VALUE = -1e30
LOG2E = float(np.log2(np.e))
SKIP_LOG2 = 60.0
SKIP_MARGIN = 1.02

LANES = 128
VMEM_LIMIT = 56 * 1024 * 1024

NT_DIMS = (((1,), (1,)), ((), ()))


def _params(*sem):
    return pltpu.CompilerParams(dimension_semantics=sem, vmem_limit_bytes=VMEM_LIMIT)


def _layer_norm(y, g, b):
    mu = jnp.mean(y, axis=-1, keepdims=True)
    yc = y - mu
    var = jnp.mean(yc * yc, axis=-1, keepdims=True)
    return yc * lax.rsqrt(var + LN_EPS) * g + b


def _bf16_split(value, parts=3):
    out, rest = [], np.float64(value)
    for _ in range(parts):
        piece = float(np.asarray(rest, np.float32).astype(jnp.bfloat16).astype(np.float64))
        out.append(piece)
        rest = rest - piece
    return out


def _proj_kernel(x_ref, w_ref, s_ref, o_ref):
    acc = jnp.dot(x_ref[...].astype(BF16), w_ref[...], preferred_element_type=F32)
    o_ref[...] = (acc * s_ref[...]).astype(o_ref.dtype)


def _project(x2, w_bf16, col_scale, tm=1024, tn=512):
    t, k = x2.shape
    n = w_bf16.shape[1]
    return pl.pallas_call(
        _proj_kernel,
        grid=(t // tm, n // tn),
        in_specs=[
            pl.BlockSpec((tm, k), lambda i, j: (i, 0)),
            pl.BlockSpec((k, tn), lambda i, j: (0, j)),
            pl.BlockSpec((1, tn), lambda i, j: (0, j)),
        ],
        out_specs=pl.BlockSpec((tm, tn), lambda i, j: (i, j)),
        out_shape=jax.ShapeDtypeStruct((t, n), BF16),
        compiler_params=_params("parallel", "arbitrary"),
        name="qkv_proj",
    )(x2, w_bf16, col_scale)


def _proj_t_kernel(x_ref, wt_ref, o_ref):
    acc = lax.dot_general(wt_ref[...], x_ref[...].astype(BF16), NT_DIMS, preferred_element_type=F32)
    o_ref[...] = acc.astype(o_ref.dtype)


def _project_t(x2, wt_bf16, tm=1024):
    t, k = x2.shape
    n = wt_bf16.shape[0]
    return pl.pallas_call(
        _proj_t_kernel,
        grid=(t // tm,),
        in_specs=[
            pl.BlockSpec((tm, k), lambda i: (i, 0)),
            pl.BlockSpec((n, k), lambda i: (0, 0)),
        ],
        out_specs=pl.BlockSpec((n, tm), lambda i: (0, i)),
        out_shape=jax.ShapeDtypeStruct((n, t), BF16),
        compiler_params=_params("parallel"),
        name="v_proj_t",
    )(x2, wt_bf16)


NAT_ROWS = 4
NAT_KROWS = NAT_ROWS + WIN_ROWS


def _natten_bias(rpb, n_rows):
    n_heads = rpb.shape[0]
    span = 2 * WIN_COLS - 1
    pad_lo = GRID_W - WIN_COLS
    padded = jnp.pad(rpb.astype(F32) * LOG2E, ((0, 0), (0, 0), (pad_lo, 2 * GRID_W - 1 - pad_lo - span)))
    toep = jnp.stack([padded[:, :, GRID_W - 1 - c:2 * GRID_W - 1 - c] for c in range(GRID_W)], axis=3)
    c = np.arange(GRID_W)[None, :]
    kc = np.arange(GRID_W)[:, None]
    c_start = np.clip(c - WIN_COLS // 2, 0, GRID_W - WIN_COLS)
    valid = (kc >= c_start) & (kc < c_start + WIN_COLS)
    toep = jnp.where(valid[None, None], toep, MASK_VALUE)
    masked = jnp.full((n_heads, GRID_W, GRID_W), MASK_VALUE, F32)
    n_blk = n_rows // NAT_ROWS
    kinds = []
    for blk in (0, 1, n_blk - 1):
        r0 = blk * NAT_ROWS
        ks = int(np.clip(r0 - WIN_ROWS // 2, 0, n_rows - NAT_KROWS))
        cols = []
        for a in range(NAT_ROWS):
            r = r0 + a
            rs = int(np.clip(r - WIN_ROWS // 2, 0, n_rows - WIN_ROWS))
            rows = []
            for i in range(NAT_KROWS):
                kr = ks + i
                rows.append(toep[:, kr - r + WIN_ROWS - 1] if rs <= kr < rs + WIN_ROWS else masked)
            cols.append(jnp.concatenate(rows, axis=1))
        kinds.append(jnp.concatenate(cols, axis=2))
    b = jnp.stack(kinds, axis=0)
    nk, nq = b.shape[2], b.shape[3]
    b = b.reshape(3, n_heads // 2, 2, nk, nq).transpose(0, 1, 3, 2, 4)
    return b.reshape(3, n_heads // 2, nk, 2 * nq)


def _natten_kernel(q_ref, k_ref, vt_ref, bias_ref, o_ref, raw_ref, cmax_ref, *, n_rows):
    nq = NAT_ROWS * GRID_W
    nk = NAT_KROWS * GRID_W
    n_blk = n_rows // NAT_ROWS
    lane = lax.broadcasted_iota(jnp.int32, (nq, LANES), 1)
    low = lane < HEAD_DIM

    def key_start(blk):
        ks = jnp.clip(blk * NAT_ROWS - WIN_ROWS // 2, 0, n_rows - NAT_KROWS)
        return pl.multiple_of(ks * GRID_W, NAT_ROWS * GRID_W)

    def scores(blk, slot):
        q = q_ref[pl.ds(pl.multiple_of(blk * nq, nq), nq), :]
        zero = jnp.zeros_like(q)
        qx = jnp.concatenate([jnp.where(low, q, zero), jnp.where(low, zero, q)], axis=0)
        k = k_ref[pl.ds(key_start(blk), nk), :]
        kind = jnp.where(blk == 0, 0, jnp.where(blk == n_blk - 1, 2, 1))
        raw = lax.dot_general(k, qx, NT_DIMS, preferred_element_type=F32) + bias_ref[kind, 0]
        raw_ref[slot] = raw
        cmax_ref[slot] = jnp.max(raw, axis=0, keepdims=True)

    def absorb(blk, slot):
        vt = vt_ref[:, pl.ds(key_start(blk), nk)]
        p = jnp.exp2(raw_ref[slot] - cmax_ref[slot])
        l = jnp.sum(p, axis=0, keepdims=True)
        res = jnp.dot(vt, p.astype(BF16), preferred_element_type=F32) / l
        o_t = jnp.concatenate([res[:HEAD_DIM, :nq], res[HEAD_DIM:, nq:]], axis=0)
        o_ref[pl.ds(pl.multiple_of(blk * nq, nq), nq), :] = o_t.T.astype(o_ref.dtype)

    scores(0, 0)

    def body(u, carry):
        blk = 2 * u
        scores(blk + 1, 1)
        absorb(blk, 0)
        scores(blk + 2, 0)
        absorb(blk + 1, 1)
        return carry

    assert n_blk % 2 == 0
    lax.fori_loop(0, n_blk // 2 - 1, body, 0)
    scores(n_blk - 1, 1)
    absorb(n_blk - 2, 0)
    absorb(n_blk - 1, 1)


def _natten(qk, vt, bias, batch, seq):
    n_rows = seq // GRID_W
    hp = N_HEADS_A // 2
    nq = NAT_ROWS * GRID_W
    nk = NAT_KROWS * GRID_W
    return pl.pallas_call(
        functools.partial(_natten_kernel, n_rows=n_rows),
        grid=(batch, hp),
        in_specs=[
            pl.BlockSpec((seq, LANES), lambda b, h: (b, h)),
            pl.BlockSpec((seq, LANES), lambda b, h: (b, hp + h)),
            pl.BlockSpec((LANES, seq), lambda b, h: (h, b)),
            pl.BlockSpec((3, 1, nk, 2 * nq), lambda b, h: (0, h, 0, 0)),
        ],
        out_specs=pl.BlockSpec((seq, LANES), lambda b, h: (b, h)),
        out_shape=jax.ShapeDtypeStruct((batch * seq, WIDTH_A), BF16),
        scratch_shapes=[pltpu.VMEM((2, nk, 2 * nq), F32), pltpu.VMEM((2, 1, 2 * nq), F32)],
        compiler_params=_params("parallel", "parallel"),
        name="natten",
    )(qk, qk, vt, bias)


def _diff_kernel(sl_ref, lam_ref, q_ref, k_ref, vt_ref, f_ref, pen_ref, g_ref, o_ref,
                 qop_ref, raw_ref, cmax_ref, coff_ref, acc_ref, knorm_ref, lfin_ref, *, tq, tk, out_scale):
    h = pl.program_id(1)
    qi = pl.program_id(2)
    n_kt = k_ref.shape[0] // tk
    sl = sl_ref[h]
    a1, a2, a3 = _bf16_split(LOG2E)

    q = q_ref[...]
    lane_q = lax.broadcasted_iota(jnp.int32, q.shape, 1)
    low_q = lane_q < HEAD_DIM
    zero = jnp.zeros_like(q)
    q1 = jnp.where(low_q, q, zero)
    q2 = jnp.where(low_q, zero, q)

    def q_ops(sign):
        if sign == 0:
            f = zero
        else:
            f = jnp.where(lane_q == 0, sign * a1,
                          jnp.where(lane_q == 1, sign * a2, jnp.where(lane_q == 2, sign * a3, 0.0))).astype(BF16)
        return jnp.concatenate([jnp.concatenate([q1, f], axis=1), jnp.concatenate([q2, f], axis=1)], axis=0)

    qop_ref[0] = q_ops(1.0)
    qop_ref[1] = q_ops(-1.0)
    fk = f_ref[0]
    pos = lax.broadcasted_iota(jnp.int32, (1, 2 * tq), 1)
    iq = (qi * tq + jnp.where(pos < tq, pos, pos - tq)).astype(F32)

    kt_mix = (qi * tq) // tk
    sel = qi - kt_mix * (tk // tq)

    @pl.when(qi == 0)
    def _():
        def tile_norm(kt, best):
            kk = k_ref[pl.ds(pl.multiple_of(kt * tk, tk), tk), :].astype(F32)
            kk = kk * kk
            low_k = lax.broadcasted_iota(jnp.int32, kk.shape, 1) < HEAD_DIM
            n1 = jnp.sum(jnp.where(low_k, kk, 0.0), axis=1, keepdims=True)
            n2 = jnp.sum(jnp.where(low_k, 0.0, kk), axis=1, keepdims=True)
            return jnp.maximum(best, jnp.max(jnp.maximum(n1, n2), axis=0, keepdims=True))

        best = lax.fori_loop(0, n_kt, tile_norm, jnp.zeros((1, 1), F32))
        knorm_ref[...] = jnp.sqrt(best)

    qq = q.astype(F32)
    qq = qq * qq
    qn1 = jnp.sum(jnp.where(low_q, qq, 0.0), axis=1, keepdims=True)
    qn2 = jnp.sum(jnp.where(low_q, 0.0, qq), axis=1, keepdims=True)
    qnorm = jnp.sqrt(jnp.max(jnp.maximum(qn1, qn2), axis=0, keepdims=True))
    reach = (SKIP_LOG2 + 2.0 * SKIP_MARGIN * qnorm * knorm_ref[...]) / sl
    q_lo = (qi * tq).astype(F32)
    kt_lo = jnp.maximum(jnp.floor((q_lo - reach - 1.0) / tk), 0.0).astype(jnp.int32)[0, 0]
    kt_hi = jnp.minimum(jnp.floor((q_lo + (tq + reach)) / tk), n_kt - 1.0).astype(jnp.int32)[0, 0]
    n_steps = kt_hi - kt_lo + 1

    def load_k(kt):
        off = pl.multiple_of(kt * tk, tk)
        return jnp.concatenate([k_ref[pl.ds(off, tk), :], fk], axis=1), vt_ref[:, pl.ds(off, tk)]

    def tile_of(s):
        t = kt_lo + jnp.minimum(s, n_steps - 1) - 1
        return jnp.where(s == 0, kt_mix, t + (t >= kt_mix).astype(jnp.int32))

    def scores(s, slot):
        kt = tile_of(s)
        kx, _ = load_k(kt)
        raw = lax.dot_general(kx, qop_ref[(kt > kt_mix).astype(jnp.int32)], NT_DIMS, preferred_element_type=F32)
        c = -sl * jnp.abs(iq - (kt * tk + tk // 2).astype(F32))
        raw_ref[slot] = raw
        cmax_ref[slot] = jnp.max(raw, axis=0, keepdims=True) + c
        coff_ref[slot] = c

    def absorb(s, slot, state):
        m_prev, l_prev = state
        _, vt = load_k(tile_of(s))
        m_new = jnp.maximum(m_prev, cmax_ref[slot])
        a = jnp.exp2(m_prev - m_new)
        p = jnp.exp2(raw_ref[slot] - (m_new - coff_ref[slot]))
        l_new = a * l_prev + jnp.sum(p, axis=0, keepdims=True)
        acc_ref[...] = a * acc_ref[...] + jnp.dot(vt, p.astype(BF16), preferred_element_type=F32)
        return m_new, l_new

    acc_ref[...] = jnp.zeros_like(acc_ref)
    state = (jnp.full((1, 2 * tq), -jnp.inf, F32), jnp.zeros((1, 2 * tq), F32))

    kx, _ = load_k(kt_mix)
    pen = pen_ref[0, sel]
    raw = lax.dot_general(kx, q_ops(0), NT_DIMS, preferred_element_type=F32) - jnp.concatenate([pen, pen], axis=1)
    raw_ref[0] = raw
    cmax_ref[0] = jnp.max(raw, axis=0, keepdims=True)
    coff_ref[0] = jnp.zeros((1, 2 * tq), F32)

    def pair(s, st):
        scores(s + 1, 1)
        st = absorb(s, 0, st)
        scores(s + 2, 0)
        return absorb(s + 1, 1, st)

    n_quads = n_steps // 4
    state = lax.fori_loop(0, n_quads, lambda u, st: pair(4 * u + 2, pair(4 * u, st)), state)
    m_run, l_run = lax.fori_loop(2 * n_quads, n_steps // 2, lambda u, st: pair(2 * u, st), state)
    lfin_ref[...] = l_run

    @pl.when(n_steps % 2 == 1)
    def _():
        _, l_last = absorb(n_steps - 1, 0, (m_run, l_run))
        lfin_ref[...] = l_last

    on = acc_ref[...] / lfin_ref[...]
    o = (on[:, :tq] - lam_ref[0] * on[:, tq:]).T
    o = o * lax.rsqrt(jnp.mean(o * o, axis=-1, keepdims=True) + RMS_EPS)
    o_ref[...] = (o * g_ref[0] * out_scale).astype(o_ref.dtype)


def _diff_tables(slopes, tq, tk):
    j = np.arange(tk)
    feat = slopes[:, None] * (j - tk // 2)[None, :]
    feat = np.broadcast_to(feat[:, :, None], (len(slopes), tk, LANES))
    i = np.arange(tq)
    pos = np.arange(tk // tq)[:, None, None] * tq + i[None, None, :]
    dist = np.abs(pos - j[None, :, None])
    pen = (slopes * LOG2E)[:, None, None, None] * dist[None]
    return jnp.asarray(feat, BF16), jnp.asarray(pen, F32)


def _diff_attention(qk, vt, lam, subln_g, lam_init, batch, seq, tq=512, tk=512):
    nq = seq // tq
    base = 2 * WIDTH_A // LANES
    hb = N_HEADS_B
    slopes = 2.0 ** (-8.0 * (np.arange(hb) + 1) / hb)
    feat, pen = _diff_tables(slopes, tq, tk)
    sl = jnp.asarray(slopes * LOG2E, F32)
    smem = pl.BlockSpec(memory_space=pltpu.SMEM)
    return pl.pallas_call(
        functools.partial(_diff_kernel, tq=tq, tk=tk, out_scale=1.0 - lam_init),
        grid=(batch, hb, nq),
        in_specs=[
            smem, smem,
            pl.BlockSpec((tq, LANES), lambda b, h, i: (b * nq + i, base + h)),
            pl.BlockSpec((seq, LANES), lambda b, h, i: (b, base + hb + h)),
            pl.BlockSpec((LANES, seq), lambda b, h, i: (WIDTH_A // LANES + h, b)),
            pl.BlockSpec((1, tk, LANES), lambda b, h, i: (h, 0, 0)),
            pl.BlockSpec((1, tk // tq, tk, tq), lambda b, h, i: (h, 0, 0, 0)),
            pl.BlockSpec((1, 1, LANES), lambda b, h, i: (h, 0, 0)),
        ],
        out_specs=pl.BlockSpec((tq, LANES), lambda b, h, i: (b * nq + i, h)),
        out_shape=jax.ShapeDtypeStruct((batch * seq, WIDTH_B), BF16),
        scratch_shapes=[
            pltpu.VMEM((2, 2 * tq, 2 * LANES), BF16),
            pltpu.VMEM((2, tk, 2 * tq), F32),
            pltpu.VMEM((2, 1, 2 * tq), F32),
            pltpu.VMEM((2, 1, 2 * tq), F32),
            pltpu.VMEM((LANES, 2 * tq), F32),
            pltpu.VMEM((1, 1), F32),
            pltpu.VMEM((1, 2 * tq), F32),
        ],
        compiler_params=_params("arbitrary", "arbitrary", "arbitrary"),
        name="diff_attn",
    )(sl, lam, qk, qk, vt, feat, pen, subln_g.astype(F32).reshape(hb, 1, LANES))


def _attn_out_kernel(oa_ref, ob_ref, wa_ref, wb_ref, x_ref, g_ref, b_ref, o_ref):
    h = jnp.dot(oa_ref[...], wa_ref[...], preferred_element_type=F32)
    h = h + jnp.dot(ob_ref[...], wb_ref[...], preferred_element_type=F32)
    o_ref[...] = _layer_norm(ALPHA * x_ref[...] + h, g_ref[...], b_ref[...])


def _attn_out(oa, ob, w_out_bf16, x2, g, b, tm=512):
    t, d = x2.shape
    wa, wb = w_out_bf16[:WIDTH_A], w_out_bf16[WIDTH_A:]
    row = lambda w: pl.BlockSpec((tm, w), lambda i: (i, 0))
    full = lambda a: pl.BlockSpec(a.shape, lambda i: (0,) * a.ndim)
    return pl.pallas_call(
        _attn_out_kernel,
        grid=(t // tm,),
        in_specs=[row(WIDTH_A), row(WIDTH_B), full(wa), full(wb), row(d), full(g), full(b)],
        out_specs=row(d),
        out_shape=jax.ShapeDtypeStruct((t, d), F32),
        compiler_params=_params("parallel"),
        name="attn_out_ln",
    )(oa, ob, wa, wb, x2, g, b)


def _gmlp_kernel(x_ref, win_ref, lg_ref, lb_ref, ws_ref, bs_ref, wout_ref, g_ref, b_ref, o_ref, *, tm):
    x = x_ref[...]
    width = wout_ref.shape[0]
    gw = width // GMLP_GROUPS
    z = jnp.dot(x.astype(BF16), win_ref[...], preferred_element_type=F32)
    z = 0.5 * z * (1.0 + lax.erf(z * (2.0 ** -0.5)))
    u = z[:, :width]
    v = _layer_norm(z[:, width:], lg_ref[...], lb_ref[...]).astype(BF16)
    chunks = []
    for n in range(tm // CHUNK):
        groups = []
        for gi in range(GMLP_GROUPS):
            vg = v[n * CHUNK:(n + 1) * CHUNK, gi * gw:(gi + 1) * gw]
            groups.append(jnp.dot(ws_ref[gi], vg, preferred_element_type=F32))
        chunks.append(jnp.concatenate(groups, axis=1) + bs_ref[...])
    sv = jnp.concatenate(chunks, axis=0)
    h = jnp.dot((u * sv).astype(BF16), wout_ref[...], preferred_element_type=F32)
    o_ref[...] = _layer_norm(ALPHA * x + h, g_ref[...], b_ref[...])


def _gmlp_layer(x2, w_in, ln_g, ln_b, w_s, bs_full, w_out, g, b, tm=256):
    t, d = x2.shape
    row = pl.BlockSpec((tm, d), lambda i: (i, 0))
    full = lambda a: pl.BlockSpec(a.shape, lambda i: (0,) * a.ndim)
    args = (w_in, ln_g, ln_b, w_s, bs_full, w_out, g, b)
    return pl.pallas_call(
        functools.partial(_gmlp_kernel, tm=tm),
        grid=(t // tm,),
        in_specs=[row] + [full(a) for a in args],
        out_specs=row,
        out_shape=jax.ShapeDtypeStruct((t, d), F32),
        compiler_params=_params("parallel"),
        name="gmlp_layer",
    )(x2, *args)


def _router_kernel(x_ref, wr_ref, o_ref):
    logits = lax.dot_general(wr_ref[...], x_ref[...], NT_DIMS, preferred_element_type=F32,
                             precision=lax.Precision.HIGHEST)
    m = jnp.max(logits, axis=0, keepdims=True)
    p = jnp.exp(logits - m)
    o_ref[0] = p / jnp.sum(p, axis=0, keepdims=True)


def _router(x2, w_router_t, batch, seq, tm=1024):
    d = x2.shape[1]
    ns = seq // tm
    return pl.pallas_call(
        _router_kernel,
        grid=(batch, ns),
        in_specs=[
            pl.BlockSpec((tm, d), lambda b, i: (b * ns + i, 0)),
            pl.BlockSpec((N_EXPERTS, d), lambda b, i: (0, 0)),
        ],
        out_specs=pl.BlockSpec((1, N_EXPERTS, tm), lambda b, i: (b, 0, i)),
        out_shape=jax.ShapeDtypeStruct((batch, N_EXPERTS, seq), F32),
        compiler_params=_params("parallel", "parallel"),
        name="router",
    )(x2, w_router_t)


def _route_kernel(aff_ref, idx_ref, gate_ref, thr_ref, *, cap):
    ne, nt = aff_ref.shape[1], aff_ref.shape[2]
    a_all = aff_ref[0]
    u_all = pltpu.bitcast(a_all, jnp.int32)

    def add_bit(i, thr):
        cand = thr | jnp.left_shift(1, 30 - i)
        count = jnp.sum(jnp.where(u_all >= cand, 1.0, 0.0), axis=(1, 2), keepdims=True)
        return jnp.where(count >= cap, cand, thr)

    thr_ref[...] = lax.fori_loop(0, 31, add_bit, jnp.zeros((ne, 1, 1), jnp.int32))

    lane = lax.broadcasted_iota(jnp.int32, (LANES, LANES), 1)
    sub = lax.broadcasted_iota(jnp.int32, (LANES, LANES), 0)
    tri_incl = jnp.where(sub <= lane, 1.0, 0.0).astype(BF16)
    r64 = lax.broadcasted_iota(jnp.int32, (nt, nt), 0)
    c64 = lax.broadcasted_iota(jnp.int32, (nt, nt), 1)
    low_strict = jnp.where(c64 < r64, 1.0, 0.0).astype(BF16)
    up_incl = jnp.where(r64 <= c64, 1.0, 0.0).astype(BF16)
    ones_rows = jnp.ones((8, LANES), BF16)
    slot = lax.broadcasted_iota(jnp.int32, (cap, 1), 0).astype(F32)
    lane_nt = lax.broadcasted_iota(jnp.int32, (cap, nt), 1)
    lane_f = lax.broadcasted_iota(jnp.int32, (cap, LANES), 1).astype(F32)

    def cumsum_tokens(mask_f32):
        mb = mask_f32.astype(BF16)
        local = jnp.dot(mb, tri_incl, preferred_element_type=F32)
        n_b = jnp.broadcast_to(jnp.sum(mask_f32, axis=1, keepdims=True), mask_f32.shape)
        before = jnp.dot(low_strict, n_b.astype(BF16), preferred_element_type=F32)
        return before + local, local

    def one_expert(e, carry):
        a = aff_ref[0, e]
        u = pltpu.bitcast(a, jnp.int32)
        t = thr_ref[e]
        gt = u > t
        eq = u == t
        need = cap - jnp.sum(jnp.where(gt, 1.0, 0.0), axis=(0, 1), keepdims=True)
        eq_f = jnp.where(eq, 1.0, 0.0)
        eq_incl, _ = cumsum_tokens(eq_f)
        sel = jnp.logical_or(gt, jnp.logical_and(eq, eq_incl - eq_f < need))
        sel_f = jnp.where(sel, 1.0, 0.0)
        _, local = cumsum_tokens(sel_f)
        n_row = lax.dot_general(ones_rows, sel_f.astype(BF16), NT_DIMS, preferred_element_type=F32)[:1]
        incl_row = jnp.dot(jnp.broadcast_to(n_row, (8, nt)).astype(BF16), up_incl, preferred_element_type=F32)[:1]
        excl_row = incl_row - n_row
        krow = jnp.sum(jnp.where(incl_row <= slot, 1.0, 0.0), axis=1, keepdims=True)
        onehot = jnp.where(lane_nt == krow.astype(jnp.int32), 1.0, 0.0)
        rank = slot - jnp.sum(onehot * excl_row, axis=1, keepdims=True)
        row_counts = jnp.dot(onehot.astype(BF16), local.astype(BF16), preferred_element_type=F32)
        pos = jnp.sum(jnp.where(row_counts <= rank, 1.0, 0.0), axis=1, keepdims=True)
        idx_ref[0, e] = (krow * LANES + pos).astype(jnp.int32)
        row_aff = jnp.dot(onehot, a, preferred_element_type=F32, precision=lax.Precision.HIGHEST)
        gate_ref[0, e] = jnp.sum(jnp.where(lane_f == pos, row_aff, 0.0), axis=1, keepdims=True)
        return carry

    lax.fori_loop(0, ne, one_expert, 0)


def _route(aff_t, cap):
    batch, ne, seq = aff_t.shape
    nt = seq // LANES
    out = jax.ShapeDtypeStruct((batch, ne, cap, 1), jnp.int32), jax.ShapeDtypeStruct((batch, ne, cap, 1), F32)
    spec = pl.BlockSpec((1, ne, cap, 1), lambda b: (b, 0, 0, 0))
    return pl.pallas_call(
        functools.partial(_route_kernel, cap=cap),
        grid=(batch,),
        in_specs=[pl.BlockSpec((1, ne, nt, LANES), lambda b: (b, 0, 0, 0))],
        out_specs=(spec, spec),
        out_shape=out,
        scratch_shapes=[pltpu.VMEM((ne, 1, 1), jnp.int32)],
        compiler_params=_params("parallel"),
        name="route",
    )(aff_t.reshape(batch, ne, nt, LANES))


def _expert_kernel(x_ref, wg_ref, wu_ref, wd_ref, gate_ref, o_ref):
    f = pl.program_id(2)
    x = x_ref[0, 0]
    a = jnp.dot(x, wg_ref[0, 0].astype(BF16), preferred_element_type=F32)
    u = jnp.dot(x, wu_ref[0, 0].astype(BF16), preferred_element_type=F32)
    hmid = (a * jax.nn.sigmoid(a) * u).astype(BF16)
    y = jnp.dot(hmid, wd_ref[0, 0].astype(BF16), preferred_element_type=F32)

    @pl.when(f == 0)
    def _():
        o_ref[0, 0] = y

    @pl.when(f > 0)
    def _():
        o_ref[0, 0] += y

    @pl.when(f == pl.num_programs(2) - 1)
    def _():
        o_ref[0, 0] = o_ref[0, 0] * gate_ref[0, 0]


def _experts(xg, weights, gate, tf=1024):
    wg, wu, wd, layer = weights
    batch, ne, cap, d = xg.shape
    dff = wg.shape[3]
    return pl.pallas_call(
        _expert_kernel,
        grid=(ne, batch, dff // tf),
        in_specs=[
            pl.BlockSpec((1, 1, cap, d), lambda e, b, f: (b, e, 0, 0)),
            pl.BlockSpec((1, 1, d, tf), lambda e, b, f: (layer, e, 0, f)),
            pl.BlockSpec((1, 1, d, tf), lambda e, b, f: (layer, e, 0, f)),
            pl.BlockSpec((1, 1, tf, d), lambda e, b, f: (layer, e, f, 0)),
            pl.BlockSpec((1, 1, cap, 1), lambda e, b, f: (b, e, 0, 0)),
        ],
        out_specs=pl.BlockSpec((1, 1, cap, d), lambda e, b, f: (b, e, 0, 0)),
        out_shape=jax.ShapeDtypeStruct((batch, ne, cap, d), F32),
        compiler_params=_params("parallel", "parallel", "arbitrary"),
        name="experts",
    )(xg, wg, wu, wd, gate)


def _gather_kernel(cur_ref, next_ref, x_hbm, o_ref, buf, sem, *, cap):
    g = pl.program_id(0)
    last = pl.num_programs(0) - 1
    slot = lax.rem(g, 2)

    def issue_all(idx_ref, buf_slot):
        def issue(r, carry):
            pltpu.make_async_copy(x_hbm.at[pl.ds(idx_ref[0, 0, r], 1), :], buf.at[buf_slot, pl.ds(r, 1), :],
                                  sem.at[buf_slot]).start()
            return carry

        lax.fori_loop(0, cap, issue, 0, unroll=8)

    @pl.when(g == 0)
    def _():
        issue_all(cur_ref, 0)

    @pl.when(g < last)
    def _():
        issue_all(next_ref, 1 - slot)

    pltpu.make_async_copy(x_hbm.at[pl.ds(0, cap), :], buf.at[slot], sem.at[slot]).wait()
    o_ref[...] = buf[slot].astype(o_ref.dtype)


def _gather_rows(x2, rows, cap):
    g = rows.shape[0]
    d = x2.shape[1]
    smem = lambda fn: pl.BlockSpec((1, 1, cap), fn, memory_space=pltpu.SMEM)
    return pl.pallas_call(
        functools.partial(_gather_kernel, cap=cap),
        grid=(g,),
        in_specs=[smem(lambda i: (i, 0, 0)), smem(lambda i: (jnp.minimum(i + 1, g - 1), 0, 0)),
                  pl.BlockSpec(memory_space=pl.ANY)],
        out_specs=pl.BlockSpec((cap, d), lambda i: (i, 0)),
        out_shape=jax.ShapeDtypeStruct((g * cap, d), BF16),
        scratch_shapes=[pltpu.VMEM((2, cap, d), F32), pltpu.SemaphoreType.DMA((2,))],
        compiler_params=_params("arbitrary"),
        name="gather_rows",
    )(rows, rows, x2)


COMBINE_ROWS_PER_STEP = 8


def _combine_kernel(idx_ref, y_ref, o_hbm, acc_ref, sem, *, cap, seq):
    b = pl.program_id(0)
    e = pl.program_id(1)

    @pl.when(e == 0)
    def _():
        acc_ref[...] = jnp.zeros_like(acc_ref)

    def step(i, carry):
        r0 = pl.multiple_of(i * COMBINE_ROWS_PER_STEP, COMBINE_ROWS_PER_STEP)
        toks = [idx_ref[0, 0, r0 + j] for j in range(COMBINE_ROWS_PER_STEP)]
        rows = [acc_ref[pl.ds(toks[j], 1), :] + y_ref[0, 0, pl.ds(r0 + j, 1), :] for j in range(COMBINE_ROWS_PER_STEP)]
        for j in range(COMBINE_ROWS_PER_STEP):
            acc_ref[pl.ds(toks[j], 1), :] = rows[j]
        return carry

    lax.fori_loop(0, cap // COMBINE_ROWS_PER_STEP, step, 0)

    @pl.when(e == pl.num_programs(1) - 1)
    def _():
        cp = pltpu.make_async_copy(acc_ref, o_hbm.at[pl.ds(pl.multiple_of(b * seq, seq), seq), :], sem)
        cp.start()
        cp.wait()


def _combine(y, idx, seq):
    batch, ne, cap, d = y.shape
    return pl.pallas_call(
        functools.partial(_combine_kernel, cap=cap, seq=seq),
        grid=(batch, ne),
        in_specs=[
            pl.BlockSpec((1, 1, cap), lambda b, e: (b * ne + e, 0, 0), memory_space=pltpu.SMEM),
            pl.BlockSpec((1, 1, cap, d), lambda b, e: (b, e, 0, 0)),
        ],
        out_specs=pl.BlockSpec(memory_space=pl.ANY),
        out_shape=jax.ShapeDtypeStruct((batch * seq, d), F32),
        scratch_shapes=[pltpu.VMEM((seq, d), F32), pltpu.SemaphoreType.DMA],
        compiler_params=_params("arbitrary", "arbitrary"),
        name="combine",
    )(idx.reshape(batch * ne, 1, cap), y)


def _res_ln_kernel(x_ref, h_ref, g_ref, b_ref, o_ref):
    o_ref[...] = _layer_norm(ALPHA * x_ref[...] + h_ref[...], g_ref[...], b_ref[...])


def _res_ln(x2, h2, g, b, tm=1024):
    t, d = x2.shape
    row = pl.BlockSpec((tm, d), lambda i: (i, 0))
    vec = pl.BlockSpec((1, d), lambda i: (0, 0))
    return pl.pallas_call(
        _res_ln_kernel,
        grid=(t // tm,),
        in_specs=[row, row, vec, vec],
        out_specs=row,
        out_shape=jax.ShapeDtypeStruct((t, d), F32),
        compiler_params=_params("parallel"),
        name="res_ln",
    )(x2, h2, g, b)


def _moe_layer(x2, w_router, weights, g, b, batch, seq):
    d = x2.shape[1]
    cap = EC_FACTOR * seq // N_EXPERTS
    aff_t = _router(x2, w_router.astype(F32).T, batch, seq)
    idx, gate = _route(aff_t, cap)
    idx = idx.reshape(batch, N_EXPERTS, cap)
    rows = idx + (jnp.arange(batch, dtype=idx.dtype) * seq)[:, None, None]
    xg = _gather_rows(x2, rows.reshape(batch * N_EXPERTS, 1, cap), cap).reshape(batch, N_EXPERTS, cap, d)
    y = _experts(xg, weights, gate)
    return _res_ln(x2, _combine(y, idx, seq), g, b)


def _lambda_init(layer_number):
    return 0.8 - 0.6 * float(np.exp(-0.3 * (layer_number - 1)))


def _row(v):
    return v.astype(F32).reshape(1, -1)


def kernel(x, w_in_ab, rpb_a, lambda_qk, subln_g, w_out_ab, w_in_c, ln_v_g, ln_v_b, w_s, b_s,
           w_out_c, ln_mix_g, ln_mix_b, w_router, w_gate, w_up, w_down, ln_ffn_g, ln_ffn_b):
    batch, seq, d = x.shape
    x2 = x.astype(F32).reshape(batch * seq, d)
    v_a = slice(2 * WIDTH_A, 3 * WIDTH_A)
    v_b = slice(3 * WIDTH_A + 2 * WIDTH_B, 3 * WIDTH_A + 3 * WIDTH_B)
    col_scale = np.ones((1, 2 * WIDTH_A + 2 * WIDTH_B), np.float32)
    col_scale[:, :WIDTH_A] = QK_SCALE * LOG2E
    col_scale[:, 2 * WIDTH_A:2 * WIDTH_A + WIDTH_B] = QK_SCALE * LOG2E
    col_scale = jnp.asarray(col_scale)

    for l in range(DEPTH):
        i = l // 2
        if l % 2 == 0:
            lam_init = _lambda_init(l + 1)
            lf = lambda_qk[i].astype(F32)
            lam = jnp.exp(jnp.sum(lf[0] * lf[1])) - jnp.exp(jnp.sum(lf[2] * lf[3])) + lam_init
            w_in = w_in_ab[i].astype(BF16)
            qk = _project(x2, jnp.concatenate([w_in[:, :v_a.start], w_in[:, v_a.stop:v_b.start]], axis=1), col_scale)
            vt = _project_t(x2, jnp.concatenate([w_in[:, v_a], w_in[:, v_b]], axis=1).T)
            oa = _natten(qk, vt, _natten_bias(rpb_a[i], seq // GRID_W), batch, seq)
            ob = _diff_attention(qk, vt, lam.reshape(1), subln_g[i], lam_init, batch, seq)
            x2 = _attn_out(oa, ob, w_out_ab[i].astype(BF16), x2, _row(ln_mix_g[l]), _row(ln_mix_b[l]))
        else:
            gw = w_in_c.shape[2] // 2 // GMLP_GROUPS
            bs_full = jnp.repeat(b_s[i].astype(F32).T, gw, axis=1)
            x2 = _gmlp_layer(x2, w_in_c[i].astype(BF16), _row(ln_v_g[i]), _row(ln_v_b[i]),
                             w_s[i].astype(BF16), bs_full, w_out_c[i].astype(BF16),
                             _row(ln_mix_g[l]), _row(ln_mix_b[l]))
        x2 = _moe_layer(x2, w_router[l], (w_gate, w_up, w_down, l),
                        _row(ln_ffn_g[l]), _row(ln_ffn_b[l]), batch, seq)
    return x2.reshape(batch, seq, d).astype(x.dtype)
```

```python
import functools

import jax
import jax.numpy as jnp
import numpy as np
from jax import lax
from jax.experimental import pallas as pl
from jax.experimental.pallas import tpu as pltpu

F32 = jnp.float32
BF16 = jnp.bfloat16

DEPTH = 4
GRID_W = 64
HEAD_DIM = 64
N_HEADS_A = 8
WIN_ROWS = 8
WIN_COLS = 16
WIDTH_A = N_HEADS_A * HEAD_DIM
N_HEADS_B = 4
WIDTH_B = N_HEADS_B * 2 * HEAD_DIM
CHUNK = 128
GMLP_GROUPS = 8
N_EXPERTS = 16
EC_FACTOR = 2
LN_EPS = 1e-5
RMS_EPS = 1e-6
ALPHA = (2.0 * DEPTH) ** 0.25
QK_SCALE = HEAD_DIM ** -0.5
MASK_VALUE = -1e30
LOG2E = float(np.log2(np.e))
SKIP_LOG2 = 60.0
SKIP_MARGIN = 1.02

LANES = 128
VMEM_LIMIT = 56 * 1024 * 1024

NT_DIMS = (((1,), (1,)), ((), ()))


def _params(*sem):
    return pltpu.CompilerParams(dimension_semantics=sem, vmem_limit_bytes=VMEM_LIMIT)


def _layer_norm(y, g, b):
    mu = jnp.mean(y, axis=-1, keepdims=True)
    yc = y - mu
    var = jnp.mean(yc * yc, axis=-1, keepdims=True)
    return yc * lax.rsqrt(var + LN_EPS) * g + b


def _bf16_split(value, parts=3):
    out, rest = [], np.float64(value)
    for _ in range(parts):
        piece = float(np.asarray(rest, np.float32).astype(jnp.bfloat16).astype(np.float64))
        out.append(piece)
        rest = rest - piece
    return out


def _proj_kernel(x_ref, w_ref, s_ref, o_ref):
    acc = jnp.dot(x_ref[...].astype(BF16), w_ref[...], preferred_element_type=F32)
    o_ref[...] = (acc * s_ref[...]).astype(o_ref.dtype)


def _project(x2, w_bf16, col_scale, tm=1024, tn=512):
    t, k = x2.shape
    n = w_bf16.shape[1]
    return pl.pallas_call(
        _proj_kernel,
        grid=(t // tm, n // tn),
        in_specs=[
            pl.BlockSpec((tm, k), lambda i, j: (i, 0)),
            pl.BlockSpec((k, tn), lambda i, j: (0, j)),
            pl.BlockSpec((1, tn), lambda i, j: (0, j)),
        ],
        out_specs=pl.BlockSpec((tm, tn), lambda i, j: (i, j)),
        out_shape=jax.ShapeDtypeStruct((t, n), BF16),
        compiler_params=_params("parallel", "arbitrary"),
        name="qkv_proj",
    )(x2, w_bf16, col_scale)


def _proj_t_kernel(x_ref, wt_ref, o_ref):
    acc = lax.dot_general(wt_ref[...], x_ref[...].astype(BF16), NT_DIMS, preferred_element_type=F32)
    o_ref[...] = acc.astype(o_ref.dtype)


def _project_t(x2, wt_bf16, tm=1024):
    t, k = x2.shape
    n = wt_bf16.shape[0]
    return pl.pallas_call(
        _proj_t_kernel,
        grid=(t // tm,),
        in_specs=[
            pl.BlockSpec((tm, k), lambda i: (i, 0)),
            pl.BlockSpec((n, k), lambda i: (0, 0)),
        ],
        out_specs=pl.BlockSpec((n, tm), lambda i: (0, i)),
        out_shape=jax.ShapeDtypeStruct((n, t), BF16),
        compiler_params=_params("parallel"),
        name="v_proj_t",
    )(x2, wt_bf16)


NAT_ROWS = 4
NAT_KROWS = NAT_ROWS + WIN_ROWS


def _natten_bias(rpb, n_rows):
    n_heads = rpb.shape[0]
    span = 2 * WIN_COLS - 1
    pad_lo = GRID_W - WIN_COLS
    padded = jnp.pad(rpb.astype(F32) * LOG2E, ((0, 0), (0, 0), (pad_lo, 2 * GRID_W - 1 - pad_lo - span)))
    toep = jnp.stack([padded[:, :, GRID_W - 1 - c:2 * GRID_W - 1 - c] for c in range(GRID_W)], axis=3)
    c = np.arange(GRID_W)[None, :]
    kc = np.arange(GRID_W)[:, None]
    c_start = np.clip(c - WIN_COLS // 2, 0, GRID_W - WIN_COLS)
    valid = (kc >= c_start) & (kc < c_start + WIN_COLS)
    toep = jnp.where(valid[None, None], toep, MASK_VALUE)
    masked = jnp.full((n_heads, GRID_W, GRID_W), MASK_VALUE, F32)
    n_blk = n_rows // NAT_ROWS
    kinds = []
    for blk in (0, 1, n_blk - 1):
        r0 = blk * NAT_ROWS
        ks = int(np.clip(r0 - WIN_ROWS // 2, 0, n_rows - NAT_KROWS))
        cols = []
        for a in range(NAT_ROWS):
            r = r0 + a
            rs = int(np.clip(r - WIN_ROWS // 2, 0, n_rows - WIN_ROWS))
            rows = []
            for i in range(NAT_KROWS):
                kr = ks + i
                rows.append(toep[:, kr - r + WIN_ROWS - 1] if rs <= kr < rs + WIN_ROWS else masked)
            cols.append(jnp.concatenate(rows, axis=1))
        kinds.append(jnp.concatenate(cols, axis=2))
    b = jnp.stack(kinds, axis=0)
    nk, nq = b.shape[2], b.shape[3]
    b = b.reshape(3, n_heads // 2, 2, nk, nq).transpose(0, 1, 3, 2, 4)
    return b.reshape(3, n_heads // 2, nk, 2 * nq)


def _natten_kernel(q_ref, k_ref, vt_ref, bias_ref, o_ref, raw_ref, cmax_ref, *, n_rows):
    nq = NAT_ROWS * GRID_W
    nk = NAT_KROWS * GRID_W
    n_blk = n_rows // NAT_ROWS
    lane = lax.broadcasted_iota(jnp.int32, (nq, LANES), 1)
    low = lane < HEAD_DIM

    def key_start(blk):
        ks = jnp.clip(blk * NAT_ROWS - WIN_ROWS // 2, 0, n_rows - NAT_KROWS)
        return pl.multiple_of(ks * GRID_W, NAT_ROWS * GRID_W)

    def scores(blk, slot):
        q = q_ref[pl.ds(pl.multiple_of(blk * nq, nq), nq), :]
        zero = jnp.zeros_like(q)
        qx = jnp.concatenate([jnp.where(low, q, zero), jnp.where(low, zero, q)], axis=0)
        k = k_ref[pl.ds(key_start(blk), nk), :]
        kind = jnp.where(blk == 0, 0, jnp.where(blk == n_blk - 1, 2, 1))
        raw = lax.dot_general(k, qx, NT_DIMS, preferred_element_type=F32) + bias_ref[kind, 0]
        raw_ref[slot] = raw
        cmax_ref[slot] = jnp.max(raw, axis=0, keepdims=True)

    def absorb(blk, slot):
        vt = vt_ref[:, pl.ds(key_start(blk), nk)]
        p = jnp.exp2(raw_ref[slot] - cmax_ref[slot])
        l = jnp.sum(p, axis=0, keepdims=True)
        res = jnp.dot(vt, p.astype(BF16), preferred_element_type=F32) / l
        o_t = jnp.concatenate([res[:HEAD_DIM, :nq], res[HEAD_DIM:, nq:]], axis=0)
        o_ref[pl.ds(pl.multiple_of(blk * nq, nq), nq), :] = o_t.T.astype(o_ref.dtype)

    scores(0, 0)

    def body(u, carry):
        blk = 2 * u
        scores(blk + 1, 1)
        absorb(blk, 0)
        scores(blk + 2, 0)
        absorb(blk + 1, 1)
        return carry

    assert n_blk % 2 == 0
    lax.fori_loop(0, n_blk // 2 - 1, body, 0)
    scores(n_blk - 1, 1)
    absorb(n_blk - 2, 0)
    absorb(n_blk - 1, 1)


def _natten(qk, vt, bias, batch, seq):
    n_rows = seq // GRID_W
    hp = N_HEADS_A // 2
    nq = NAT_ROWS * GRID_W
    nk = NAT_KROWS * GRID_W
    return pl.pallas_call(
        functools.partial(_natten_kernel, n_rows=n_rows),
        grid=(batch, hp),
        in_specs=[
            pl.BlockSpec((seq, LANES), lambda b, h: (b, h)),
            pl.BlockSpec((seq, LANES), lambda b, h: (b, hp + h)),
            pl.BlockSpec((LANES, seq), lambda b, h: (h, b)),
            pl.BlockSpec((3, 1, nk, 2 * nq), lambda b, h: (0, h, 0, 0)),
        ],
        out_specs=pl.BlockSpec((seq, LANES), lambda b, h: (b, h)),
        out_shape=jax.ShapeDtypeStruct((batch * seq, WIDTH_A), BF16),
        scratch_shapes=[pltpu.VMEM((2, nk, 2 * nq), F32), pltpu.VMEM((2, 1, 2 * nq), F32)],
        compiler_params=_params("parallel", "parallel"),
        name="natten",
    )(qk, qk, vt, bias)


def _diff_kernel(sl_ref, lam_ref, q_ref, k_ref, vt_ref, f_ref, pen_ref, g_ref, o_ref,
                 qop_ref, raw_ref, cmax_ref, coff_ref, acc_ref, knorm_ref, lfin_ref, *, tq, tk, out_scale):
    h = pl.program_id(1)
    qi = pl.program_id(2)
    n_kt = k_ref.shape[0] // tk
    sl = sl_ref[h]
    a1, a2, a3 = _bf16_split(LOG2E)

    q = q_ref[...]
    lane_q = lax.broadcasted_iota(jnp.int32, q.shape, 1)
    low_q = lane_q < HEAD_DIM
    zero = jnp.zeros_like(q)
    q1 = jnp.where(low_q, q, zero)
    q2 = jnp.where(low_q, zero, q)

    def q_ops(sign):
        if sign == 0:
            f = zero
        else:
            f = jnp.where(lane_q == 0, sign * a1,
                          jnp.where(lane_q == 1, sign * a2, jnp.where(lane_q == 2, sign * a3, 0.0))).astype(BF16)
        return jnp.concatenate([jnp.concatenate([q1, f], axis=1), jnp.concatenate([q2, f], axis=1)], axis=0)

    qop_ref[0] = q_ops(1.0)
    qop_ref[1] = q_ops(-1.0)
    fk = f_ref[0]
    pos = lax.broadcasted_iota(jnp.int32, (1, 2 * tq), 1)
    iq = (qi * tq + jnp.where(pos < tq, pos, pos - tq)).astype(F32)

    kt_mix = (qi * tq) // tk
    sel = qi - kt_mix * (tk // tq)

    @pl.when(qi == 0)
    def _():
        def tile_norm(kt, best):
            kk = k_ref[pl.ds(pl.multiple_of(kt * tk, tk), tk), :].astype(F32)
            kk = kk * kk
            low_k = lax.broadcasted_iota(jnp.int32, kk.shape, 1) < HEAD_DIM
            n1 = jnp.sum(jnp.where(low_k, kk, 0.0), axis=1, keepdims=True)
            n2 = jnp.sum(jnp.where(low_k, 0.0, kk), axis=1, keepdims=True)
            return jnp.maximum(best, jnp.max(jnp.maximum(n1, n2), axis=0, keepdims=True))

        best = lax.fori_loop(0, n_kt, tile_norm, jnp.zeros((1, 1), F32))
        knorm_ref[...] = jnp.sqrt(best)

    qq = q.astype(F32)
    qq = qq * qq
    qn1 = jnp.sum(jnp.where(low_q, qq, 0.0), axis=1, keepdims=True)
    qn2 = jnp.sum(jnp.where(low_q, 0.0, qq), axis=1, keepdims=True)
    qnorm = jnp.sqrt(jnp.max(jnp.maximum(qn1, qn2), axis=0, keepdims=True))
    reach = (SKIP_LOG2 + 2.0 * SKIP_MARGIN * qnorm * knorm_ref[...]) / sl
    q_lo = (qi * tq).astype(F32)
    kt_lo = jnp.maximum(jnp.floor((q_lo - reach - 1.0) / tk), 0.0).astype(jnp.int32)[0, 0]
    kt_hi = jnp.minimum(jnp.floor((q_lo + (tq + reach)) / tk), n_kt - 1.0).astype(jnp.int32)[0, 0]
    n_steps = kt_hi - kt_lo + 1

    def load_k(kt):
        off = pl.multiple_of(kt * tk, tk)
        return jnp.concatenate([k_ref[pl.ds(off, tk), :], fk], axis=1), vt_ref[:, pl.ds(off, tk)]

    def tile_of(s):
        t = kt_lo + jnp.minimum(s, n_steps - 1) - 1
        return jnp.where(s == 0, kt_mix, t + (t >= kt_mix).astype(jnp.int32))

    def scores(s, slot):
        kt = tile_of(s)
        kx, _ = load_k(kt)
        raw = lax.dot_general(kx, qop_ref[(kt > kt_mix).astype(jnp.int32)], NT_DIMS, preferred_element_type=F32)
        c = -sl * jnp.abs(iq - (kt * tk + tk // 2).astype(F32))
        raw_ref[slot] = raw
        cmax_ref[slot] = jnp.max(raw, axis=0, keepdims=True) + c
        coff_ref[slot] = c

    def absorb(s, slot, state):
        m_prev, l_prev = state
        _, vt = load_k(tile_of(s))
        m_new = jnp.maximum(m_prev, cmax_ref[slot])
        a = jnp.exp2(m_prev - m_new)
        p = jnp.exp2(raw_ref[slot] - (m_new - coff_ref[slot]))
        l_new = a * l_prev + jnp.sum(p, axis=0, keepdims=True)
        acc_ref[...] = a * acc_ref[...] + jnp.dot(vt, p.astype(BF16), preferred_element_type=F32)
        return m_new, l_new

    acc_ref[...] = jnp.zeros_like(acc_ref)
    state = (jnp.full((1, 2 * tq), -jnp.inf, F32), jnp.zeros((1, 2 * tq), F32))

    kx, _ = load_k(kt_mix)
    pen = pen_ref[0, sel]
    raw = lax.dot_general(kx, q_ops(0), NT_DIMS, preferred_element_type=F32) - jnp.concatenate([pen, pen], axis=1)
    raw_ref[0] = raw
    cmax_ref[0] = jnp.max(raw, axis=0, keepdims=True)
    coff_ref[0] = jnp.zeros((1, 2 * tq), F32)

    def pair(s, st):
        scores(s + 1, 1)
        st = absorb(s, 0, st)
        scores(s + 2, 0)
        return absorb(s + 1, 1, st)

    n_quads = n_steps // 4
    state = lax.fori_loop(0, n_quads, lambda u, st: pair(4 * u + 2, pair(4 * u, st)), state)
    m_run, l_run = lax.fori_loop(2 * n_quads, n_steps // 2, lambda u, st: pair(2 * u, st), state)
    lfin_ref[...] = l_run

    @pl.when(n_steps % 2 == 1)
    def _():
        _, l_last = absorb(n_steps - 1, 0, (m_run, l_run))
        lfin_ref[...] = l_last

    on = acc_ref[...] / lfin_ref[...]
    o = (on[:, :tq] - lam_ref[0] * on[:, tq:]).T
    o = o * lax.rsqrt(jnp.mean(o * o, axis=-1, keepdims=True) + RMS_EPS)
    o_ref[...] = (o * g_ref[0] * out_scale).astype(o_ref.dtype)


def _diff_tables(slopes, tq, tk):
    j = np.arange(tk)
    feat = slopes[:, None] * (j - tk // 2)[None, :]
    feat = np.broadcast_to(feat[:, :, None], (len(slopes), tk, LANES))
    i = np.arange(tq)
    pos = np.arange(tk // tq)[:, None, None] * tq + i[None, None, :]
    dist = np.abs(pos - j[None, :, None])
    pen = (slopes * LOG2E)[:, None, None, None] * dist[None]
    return jnp.asarray(feat, BF16), jnp.asarray(pen, F32)


def _diff_attention(qk, vt, lam, subln_g, lam_init, batch, seq, tq=512, tk=512):
    nq = seq // tq
    base = 2 * WIDTH_A // LANES
    hb = N_HEADS_B
    slopes = 2.0 ** (-8.0 * (np.arange(hb) + 1) / hb)
    feat, pen = _diff_tables(slopes, tq, tk)
    sl = jnp.asarray(slopes * LOG2E, F32)
    smem = pl.BlockSpec(memory_space=pltpu.SMEM)
    return pl.pallas_call(
        functools.partial(_diff_kernel, tq=tq, tk=tk, out_scale=1.0 - lam_init),
        grid=(batch, hb, nq),
        in_specs=[
            smem, smem,
            pl.BlockSpec((tq, LANES), lambda b, h, i: (b * nq + i, base + h)),
            pl.BlockSpec((seq, LANES), lambda b, h, i: (b, base + hb + h)),
            pl.BlockSpec((LANES, seq), lambda b, h, i: (WIDTH_A // LANES + h, b)),
            pl.BlockSpec((1, tk, LANES), lambda b, h, i: (h, 0, 0)),
            pl.BlockSpec((1, tk // tq, tk, tq), lambda b, h, i: (h, 0, 0, 0)),
            pl.BlockSpec((1, 1, LANES), lambda b, h, i: (h, 0, 0)),
        ],
        out_specs=pl.BlockSpec((tq, LANES), lambda b, h, i: (b * nq + i, h)),
        out_shape=jax.ShapeDtypeStruct((batch * seq, WIDTH_B), BF16),
        scratch_shapes=[
            pltpu.VMEM((2, 2 * tq, 2 * LANES), BF16),
            pltpu.VMEM((2, tk, 2 * tq), F32),
            pltpu.VMEM((2, 1, 2 * tq), F32),
            pltpu.VMEM((2, 1, 2 * tq), F32),
            pltpu.VMEM((LANES, 2 * tq), F32),
            pltpu.VMEM((1, 1), F32),
            pltpu.VMEM((1, 2 * tq), F32),
        ],
        compiler_params=_params("arbitrary", "arbitrary", "arbitrary"),
        name="diff_attn",
    )(sl, lam, qk, qk, vt, feat, pen, subln_g.astype(F32).reshape(hb, 1, LANES))


def _attn_out_kernel(oa_ref, ob_ref, wa_ref, wb_ref, x_ref, g_ref, b_ref, o_ref):
    h = jnp.dot(oa_ref[...], wa_ref[...], preferred_element_type=F32)
    h = h + jnp.dot(ob_ref[...], wb_ref[...], preferred_element_type=F32)
    o_ref[...] = _layer_norm(ALPHA * x_ref[...] + h, g_ref[...], b_ref[...])


def _attn_out(oa, ob, w_out_bf16, x2, g, b, tm=512):
    t, d = x2.shape
    wa, wb = w_out_bf16[:WIDTH_A], w_out_bf16[WIDTH_A:]
    row = lambda w: pl.BlockSpec((tm, w), lambda i: (i, 0))
    full = lambda a: pl.BlockSpec(a.shape, lambda i: (0,) * a.ndim)
    return pl.pallas_call(
        _attn_out_kernel,
        grid=(t // tm,),
        in_specs=[row(WIDTH_A), row(WIDTH_B), full(wa), full(wb), row(d), full(g), full(b)],
        out_specs=row(d),
        out_shape=jax.ShapeDtypeStruct((t, d), F32),
        compiler_params=_params("parallel"),
        name="attn_out_ln",
    )(oa, ob, wa, wb, x2, g, b)


def _gmlp_kernel(x_ref, win_ref, lg_ref, lb_ref, ws_ref, bs_ref, wout_ref, g_ref, b_ref, o_ref, *, tm):
    x = x_ref[...]
    width = wout_ref.shape[0]
    gw = width // GMLP_GROUPS
    z = jnp.dot(x.astype(BF16), win_ref[...], preferred_element_type=F32)
    z = 0.5 * z * (1.0 + lax.erf(z * (2.0 ** -0.5)))
    u = z[:, :width]
    v = _layer_norm(z[:, width:], lg_ref[...], lb_ref[...]).astype(BF16)
    chunks = []
    for n in range(tm // CHUNK):
        groups = []
        for gi in range(GMLP_GROUPS):
            vg = v[n * CHUNK:(n + 1) * CHUNK, gi * gw:(gi + 1) * gw]
            groups.append(jnp.dot(ws_ref[gi], vg, preferred_element_type=F32))
        chunks.append(jnp.concatenate(groups, axis=1) + bs_ref[...])
    sv = jnp.concatenate(chunks, axis=0)
    h = jnp.dot((u * sv).astype(BF16), wout_ref[...], preferred_element_type=F32)
    o_ref[...] = _layer_norm(ALPHA * x + h, g_ref[...], b_ref[...])


def _gmlp_layer(x2, w_in, ln_g, ln_b, w_s, bs_full, w_out, g, b, tm=256):
    t, d = x2.shape
    row = pl.BlockSpec((tm, d), lambda i: (i, 0))
    full = lambda a: pl.BlockSpec(a.shape, lambda i: (0,) * a.ndim)
    args = (w_in, ln_g, ln_b, w_s, bs_full, w_out, g, b)
    return pl.pallas_call(
        functools.partial(_gmlp_kernel, tm=tm),
        grid=(t // tm,),
        in_specs=[row] + [full(a) for a in args],
        out_specs=row,
        out_shape=jax.ShapeDtypeStruct((t, d), F32),
        compiler_params=_params("parallel"),
        name="gmlp_layer",
    )(x2, *args)


def _router_kernel(x_ref, wr_ref, o_ref):
    logits = lax.dot_general(wr_ref[...], x_ref[...], NT_DIMS, preferred_element_type=F32,
                             precision=lax.Precision.HIGHEST)
    m = jnp.max(logits, axis=0, keepdims=True)
    p = jnp.exp(logits - m)
    o_ref[0] = p / jnp.sum(p, axis=0, keepdims=True)


def _router(x2, w_router_t, batch, seq, tm=1024):
    d = x2.shape[1]
    ns = seq // tm
    return pl.pallas_call(
        _router_kernel,
        grid=(batch, ns),
        in_specs=[
            pl.BlockSpec((tm, d), lambda b, i: (b * ns + i, 0)),
            pl.BlockSpec((N_EXPERTS, d), lambda b, i: (0, 0)),
        ],
        out_specs=pl.BlockSpec((1, N_EXPERTS, tm), lambda b, i: (b, 0, i)),
        out_shape=jax.ShapeDtypeStruct((batch, N_EXPERTS, seq), F32),
        compiler_params=_params("parallel", "parallel"),
        name="router",
    )(x2, w_router_t)


def _route_kernel(aff_ref, idx_ref, gate_ref, thr_ref, *, cap):
    ne, nt = aff_ref.shape[1], aff_ref.shape[2]
    a_all = aff_ref[0]
    u_all = pltpu.bitcast(a_all, jnp.int32)

    def add_bit(i, thr):
        cand = thr | jnp.left_shift(1, 30 - i)
        count = jnp.sum(jnp.where(u_all >= cand, 1.0, 0.0), axis=(1, 2), keepdims=True)
        return jnp.where(count >= cap, cand, thr)

    thr_ref[...] = lax.fori_loop(0, 31, add_bit, jnp.zeros((ne, 1, 1), jnp.int32))

    lane = lax.broadcasted_iota(jnp.int32, (LANES, LANES), 1)
    sub = lax.broadcasted_iota(jnp.int32, (LANES, LANES), 0)
    tri_incl = jnp.where(sub <= lane, 1.0, 0.0).astype(BF16)
    r64 = lax.broadcasted_iota(jnp.int32, (nt, nt), 0)
    c64 = lax.broadcasted_iota(jnp.int32, (nt, nt), 1)
    low_strict = jnp.where(c64 < r64, 1.0, 0.0).astype(BF16)
    up_incl = jnp.where(r64 <= c64, 1.0, 0.0).astype(BF16)
    ones_rows = jnp.ones((8, LANES), BF16)
    slot = lax.broadcasted_iota(jnp.int32, (cap, 1), 0).astype(F32)
    lane_nt = lax.broadcasted_iota(jnp.int32, (cap, nt), 1)
    lane_f = lax.broadcasted_iota(jnp.int32, (cap, LANES), 1).astype(F32)

    def cumsum_tokens(mask_f32):
        mb = mask_f32.astype(BF16)
        local = jnp.dot(mb, tri_incl, preferred_element_type=F32)
        n_b = jnp.broadcast_to(jnp.sum(mask_f32, axis=1, keepdims=True), mask_f32.shape)
        before = jnp.dot(low_strict, n_b.astype(BF16), preferred_element_type=F32)
        return before + local, local

    def one_expert(e, carry):
        a = aff_ref[0, e]
        u = pltpu.bitcast(a, jnp.int32)
        t = thr_ref[e]
        gt = u > t
        eq = u == t
        need = cap - jnp.sum(jnp.where(gt, 1.0, 0.0), axis=(0, 1), keepdims=True)
        eq_f = jnp.where(eq, 1.0, 0.0)
        eq_incl, _ = cumsum_tokens(eq_f)
        sel = jnp.logical_or(gt, jnp.logical_and(eq, eq_incl - eq_f < need))
        sel_f = jnp.where(sel, 1.0, 0.0)
        _, local = cumsum_tokens(sel_f)
        n_row = lax.dot_general(ones_rows, sel_f.astype(BF16), NT_DIMS, preferred_element_type=F32)[:1]
        incl_row = jnp.dot(jnp.broadcast_to(n_row, (8, nt)).astype(BF16), up_incl, preferred_element_type=F32)[:1]
        excl_row = incl_row - n_row
        krow = jnp.sum(jnp.where(incl_row <= slot, 1.0, 0.0), axis=1, keepdims=True)
        onehot = jnp.where(lane_nt == krow.astype(jnp.int32), 1.0, 0.0)
        rank = slot - jnp.sum(onehot * excl_row, axis=1, keepdims=True)
        row_counts = jnp.dot(onehot.astype(BF16), local.astype(BF16), preferred_element_type=F32)
        pos = jnp.sum(jnp.where(row_counts <= rank, 1.0, 0.0), axis=1, keepdims=True)
        idx_ref[0, e] = (krow * LANES + pos).astype(jnp.int32)
        row_aff = jnp.dot(onehot, a, preferred_element_type=F32, precision=lax.Precision.HIGHEST)
        gate_ref[0, e] = jnp.sum(jnp.where(lane_f == pos, row_aff, 0.0), axis=1, keepdims=True)
        return carry

    lax.fori_loop(0, ne, one_expert, 0)


def _route(aff_t, cap):
    batch, ne, seq = aff_t.shape
    nt = seq // LANES
    out = jax.ShapeDtypeStruct((batch, ne, cap, 1), jnp.int32), jax.ShapeDtypeStruct((batch, ne, cap, 1), F32)
    spec = pl.BlockSpec((1, ne, cap, 1), lambda b: (b, 0, 0, 0))
    return pl.pallas_call(
        functools.partial(_route_kernel, cap=cap),
        grid=(batch,),
        in_specs=[pl.BlockSpec((1, ne, nt, LANES), lambda b: (b, 0, 0, 0))],
        out_specs=(spec, spec),
        out_shape=out,
        scratch_shapes=[pltpu.VMEM((ne, 1, 1), jnp.int32)],
        compiler_params=_params("parallel"),
        name="route",
    )(aff_t.reshape(batch, ne, nt, LANES))


def _expert_kernel(x_ref, wg_ref, wu_ref, wd_ref, gate_ref, o_ref):
    f = pl.program_id(2)
    x = x_ref[0, 0]
    a = jnp.dot(x, wg_ref[0, 0].astype(BF16), preferred_element_type=F32)
    u = jnp.dot(x, wu_ref[0, 0].astype(BF16), preferred_element_type=F32)
    hmid = (a * jax.nn.sigmoid(a) * u).astype(BF16)
    y = jnp.dot(hmid, wd_ref[0, 0].astype(BF16), preferred_element_type=F32)

    @pl.when(f == 0)
    def _():
        o_ref[0, 0] = y

    @pl.when(f > 0)
    def _():
        o_ref[0, 0] += y

    @pl.when(f == pl.num_programs(2) - 1)
    def _():
        o_ref[0, 0] = o_ref[0, 0] * gate_ref[0, 0]


def _experts(xg, weights, gate, tf=1024):
    wg, wu, wd, layer = weights
    batch, ne, cap, d = xg.shape
    dff = wg.shape[3]
    return pl.pallas_call(
        _expert_kernel,
        grid=(ne, batch, dff // tf),
        in_specs=[
            pl.BlockSpec((1, 1, cap, d), lambda e, b, f: (b, e, 0, 0)),
            pl.BlockSpec((1, 1, d, tf), lambda e, b, f: (layer, e, 0, f)),
            pl.BlockSpec((1, 1, d, tf), lambda e, b, f: (layer, e, 0, f)),
            pl.BlockSpec((1, 1, tf, d), lambda e, b, f: (layer, e, f, 0)),
            pl.BlockSpec((1, 1, cap, 1), lambda e, b, f: (b, e, 0, 0)),
        ],
        out_specs=pl.BlockSpec((1, 1, cap, d), lambda e, b, f: (b, e, 0, 0)),
        out_shape=jax.ShapeDtypeStruct((batch, ne, cap, d), F32),
        compiler_params=_params("parallel", "parallel", "arbitrary"),
        name="experts",
    )(xg, wg, wu, wd, gate)


def _gather_kernel(cur_ref, next_ref, x_hbm, o_ref, buf, sem, *, cap):
    g = pl.program_id(0)
    last = pl.num_programs(0) - 1
    slot = lax.rem(g, 2)

    def issue_all(idx_ref, buf_slot):
        def issue(r, carry):
            pltpu.make_async_copy(x_hbm.at[pl.ds(idx_ref[0, 0, r], 1), :], buf.at[buf_slot, pl.ds(r, 1), :],
                                  sem.at[buf_slot]).start()
            return carry

        lax.fori_loop(0, cap, issue, 0, unroll=8)

    @pl.when(g == 0)
    def _():
        issue_all(cur_ref, 0)

    @pl.when(g < last)
    def _():
        issue_all(next_ref, 1 - slot)

    pltpu.make_async_copy(x_hbm.at[pl.ds(0, cap), :], buf.at[slot], sem.at[slot]).wait()
    o_ref[...] = buf[slot].astype(o_ref.dtype)


def _gather_rows(x2, rows, cap):
    g = rows.shape[0]
    d = x2.shape[1]
    smem = lambda fn: pl.BlockSpec((1, 1, cap), fn, memory_space=pltpu.SMEM)
    return pl.pallas_call(
        functools.partial(_gather_kernel, cap=cap),
        grid=(g,),
        in_specs=[smem(lambda i: (i, 0, 0)), smem(lambda i: (jnp.minimum(i + 1, g - 1), 0, 0)),
                  pl.BlockSpec(memory_space=pl.ANY)],
        out_specs=pl.BlockSpec((cap, d), lambda i: (i, 0)),
        out_shape=jax.ShapeDtypeStruct((g * cap, d), BF16),
        scratch_shapes=[pltpu.VMEM((2, cap, d), F32), pltpu.SemaphoreType.DMA((2,))],
        compiler_params=_params("arbitrary"),
        name="gather_rows",
    )(rows, rows, x2)


COMBINE_ROWS_PER_STEP = 8
COMBINE_LN_ROWS = 512


def _combine_kernel(idx_ref, y_ref, x_hbm, g_ref, b_ref, o_hbm, acc_ref, xbuf, sem_x, sem_o, *, cap, seq):
    b = pl.program_id(0)
    e = pl.program_id(1)

    @pl.when(e == 0)
    def _():
        acc_ref[...] = jnp.zeros_like(acc_ref)

    def step(i, carry):
        r0 = pl.multiple_of(i * COMBINE_ROWS_PER_STEP, COMBINE_ROWS_PER_STEP)
        toks = [idx_ref[0, 0, r0 + j] for j in range(COMBINE_ROWS_PER_STEP)]
        rows = [acc_ref[pl.ds(toks[j], 1), :] + y_ref[0, 0, pl.ds(r0 + j, 1), :] for j in range(COMBINE_ROWS_PER_STEP)]
        for j in range(COMBINE_ROWS_PER_STEP):
            acc_ref[pl.ds(toks[j], 1), :] = rows[j]
        return carry

    lax.fori_loop(0, cap // COMBINE_ROWS_PER_STEP, step, 0)

    @pl.when(e == pl.num_programs(1) - 1)
    def _():
        tl = COMBINE_LN_ROWS
        n_tiles = seq // tl
        row0 = b * seq

        def x_copy(i, slot):
            return pltpu.make_async_copy(x_hbm.at[pl.ds(pl.multiple_of(row0 + i * tl, tl), tl), :], xbuf.at[slot],
                                         sem_x.at[slot])

        x_copy(0, 0).start()

        def tile(i, carry):
            slot = lax.rem(i, 2)

            @pl.when(i + 1 < n_tiles)
            def _():
                x_copy(i + 1, 1 - slot).start()

            x_copy(i, slot).wait()
            rows = pl.ds(pl.multiple_of(i * tl, tl), tl)
            acc_ref[rows, :] = _layer_norm(ALPHA * xbuf[slot] + acc_ref[rows, :], g_ref[...], b_ref[...])
            pltpu.make_async_copy(acc_ref.at[rows, :], o_hbm.at[pl.ds(pl.multiple_of(row0 + i * tl, tl), tl), :],
                                  sem_o).start()
            return carry

        lax.fori_loop(0, n_tiles, tile, 0)
        pltpu.make_async_copy(acc_ref, o_hbm.at[pl.ds(pl.multiple_of(row0, seq), seq), :], sem_o).wait()


def _combine_ln(y, idx, x2, g, b, seq):
    batch, ne, cap, d = y.shape
    vec = pl.BlockSpec((1, d), lambda bb, ee: (0, 0))
    return pl.pallas_call(
        functools.partial(_combine_kernel, cap=cap, seq=seq),
        grid=(batch, ne),
        in_specs=[
            pl.BlockSpec((1, 1, cap), lambda bb, ee: (bb * ne + ee, 0, 0), memory_space=pltpu.SMEM),
            pl.BlockSpec((1, 1, cap, d), lambda bb, ee: (bb, ee, 0, 0)),
            pl.BlockSpec(memory_space=pl.ANY), vec, vec,
        ],
        out_specs=pl.BlockSpec(memory_space=pl.ANY),
        out_shape=jax.ShapeDtypeStruct((batch * seq, d), F32),
        scratch_shapes=[pltpu.VMEM((seq, d), F32), pltpu.VMEM((2, COMBINE_LN_ROWS, d), F32),
                        pltpu.SemaphoreType.DMA((2,)), pltpu.SemaphoreType.DMA],
        compiler_params=_params("arbitrary", "arbitrary"),
        name="combine_ln",
    )(idx.reshape(batch * ne, 1, cap), y, x2, g, b)


def _moe_layer(x2, w_router, weights, g, b, batch, seq):
    d = x2.shape[1]
    cap = EC_FACTOR * seq // N_EXPERTS
    aff_t = _router(x2, w_router.astype(F32).T, batch, seq)
    idx, gate = _route(aff_t, cap)
    idx = idx.reshape(batch, N_EXPERTS, cap)
    rows = idx + (jnp.arange(batch, dtype=idx.dtype) * seq)[:, None, None]
    xg = _gather_rows(x2, rows.reshape(batch * N_EXPERTS, 1, cap), cap).reshape(batch, N_EXPERTS, cap, d)
    y = _experts(xg, weights, gate)
    return _combine_ln(y, idx, x2, g, b, seq)


def _lambda_init(layer_number):
    return 0.8 - 0.6 * float(np.exp(-0.3 * (layer_number - 1)))


def _row(v):
    return v.astype(F32).reshape(1, -1)


def kernel(x, w_in_ab, rpb_a, lambda_qk, subln_g, w_out_ab, w_in_c, ln_v_g, ln_v_b, w_s, b_s,
           w_out_c, ln_mix_g, ln_mix_b, w_router, w_gate, w_up, w_down, ln_ffn_g, ln_ffn_b):
    batch, seq, d = x.shape
    x2 = x.astype(F32).reshape(batch * seq, d)
    v_a = slice(2 * WIDTH_A, 3 * WIDTH_A)
    v_b = slice(3 * WIDTH_A + 2 * WIDTH_B, 3 * WIDTH_A + 3 * WIDTH_B)
    col_scale = np.ones((1, 2 * WIDTH_A + 2 * WIDTH_B), np.float32)
    col_scale[:, :WIDTH_A] = QK_SCALE * LOG2E
    col_scale[:, 2 * WIDTH_A:2 * WIDTH_A + WIDTH_B] = QK_SCALE * LOG2E
    col_scale = jnp.asarray(col_scale)

    for l in range(DEPTH):
        i = l // 2
        if l % 2 == 0:
            lam_init = _lambda_init(l + 1)
            lf = lambda_qk[i].astype(F32)
            lam = jnp.exp(jnp.sum(lf[0] * lf[1])) - jnp.exp(jnp.sum(lf[2] * lf[3])) + lam_init
            w_in = w_in_ab[i].astype(BF16)
            qk = _project(x2, jnp.concatenate([w_in[:, :v_a.start], w_in[:, v_a.stop:v_b.start]], axis=1), col_scale)
            vt = _project_t(x2, jnp.concatenate([w_in[:, v_a], w_in[:, v_b]], axis=1).T)
            oa = _natten(qk, vt, _natten_bias(rpb_a[i], seq // GRID_W), batch, seq)
            ob = _diff_attention(qk, vt, lam.reshape(1), subln_g[i], lam_init, batch, seq)
            x2 = _attn_out(oa, ob, w_out_ab[i].astype(BF16), x2, _row(ln_mix_g[l]), _row(ln_mix_b[l]))
        else:
            gw = w_in_c.shape[2] // 2 // GMLP_GROUPS
            bs_full = jnp.repeat(b_s[i].astype(F32).T, gw, axis=1)
            x2 = _gmlp_layer(x2, w_in_c[i].astype(BF16), _row(ln_v_g[i]), _row(ln_v_b[i]),
                             w_s[i].astype(BF16), bs_full, w_out_c[i].astype(BF16),
                             _row(ln_mix_g[l]), _row(ln_mix_b[l]))
        x2 = _moe_layer(x2, w_router[l], (w_gate, w_up, w_down, l),
                        _row(ln_ffn_g[l]), _row(ln_ffn_b[l]), batch, seq)
    return x2.reshape(batch, seq, d).astype(x.dtype)
```

```python
import functools

import jax
import jax.numpy as jnp
import numpy as np
from jax import lax
from jax.experimental import pallas as pl
from jax.experimental.pallas import tpu as pltpu

F32 = jnp.float32
BF16 = jnp.bfloat16

DEPTH = 4
GRID_W = 64
HEAD_DIM = 64
N_HEADS_A = 8
WIN_ROWS = 8
WIN_COLS = 16
WIDTH_A = N_HEADS_A * HEAD_DIM
N_HEADS_B = 4
WIDTH_B = N_HEADS_B * 2 * HEAD_DIM
CHUNK = 128
GMLP_GROUPS = 8
N_EXPERTS = 16
EC_FACTOR = 2
LN_EPS = 1e-5
RMS_EPS = 1e-6
ALPHA = (2.0 * DEPTH) ** 0.25
QK_SCALE = HEAD_DIM ** -0.5
MASK_VALUE = -1e30
LOG2E = float(np.log2(np.e))
SKIP_LOG2 = 48.0
SKIP_MARGIN = 1.02

LANES = 128
VMEM_LIMIT = 56 * 1024 * 1024

NT_DIMS = (((1,), (1,)), ((), ()))


def _params(*sem):
    return pltpu.CompilerParams(dimension_semantics=sem, vmem_limit_bytes=VMEM_LIMIT)


def _layer_norm(y, g, b):
    mu = jnp.mean(y, axis=-1, keepdims=True)
    yc = y - mu
    var = jnp.mean(yc * yc, axis=-1, keepdims=True)
    return yc * lax.rsqrt(var + LN_EPS) * g + b


def _bf16_split(value, parts=3):
    out, rest = [], np.float64(value)
    for _ in range(parts):
        piece = float(np.asarray(rest, np.float32).astype(jnp.bfloat16).astype(np.float64))
        out.append(piece)
        rest = rest - piece
    return out


def _proj_kernel(x_ref, w_ref, s_ref, o_ref):
    acc = jnp.dot(x_ref[...].astype(BF16), w_ref[...], preferred_element_type=F32)
    o_ref[...] = (acc * s_ref[...]).astype(o_ref.dtype)


def _project(x2, w_bf16, col_scale, tm=1024, tn=512):
    t, k = x2.shape
    n = w_bf16.shape[1]
    return pl.pallas_call(
        _proj_kernel,
        grid=(t // tm, n // tn),
        in_specs=[
            pl.BlockSpec((tm, k), lambda i, j: (i, 0)),
            pl.BlockSpec((k, tn), lambda i, j: (0, j)),
            pl.BlockSpec((1, tn), lambda i, j: (0, j)),
        ],
        out_specs=pl.BlockSpec((tm, tn), lambda i, j: (i, j)),
        out_shape=jax.ShapeDtypeStruct((t, n), BF16),
        compiler_params=_params("parallel", "arbitrary"),
        name="qkv_proj",
    )(x2, w_bf16, col_scale)


def _proj_t_kernel(x_ref, wt_ref, o_ref):
    acc = lax.dot_general(wt_ref[...], x_ref[...].astype(BF16), NT_DIMS, preferred_element_type=F32)
    o_ref[...] = acc.astype(o_ref.dtype)


def _project_t(x2, wt_bf16, tm=1024):
    t, k = x2.shape
    n = wt_bf16.shape[0]
    return pl.pallas_call(
        _proj_t_kernel,
        grid=(t // tm,),
        in_specs=[
            pl.BlockSpec((tm, k), lambda i: (i, 0)),
            pl.BlockSpec((n, k), lambda i: (0, 0)),
        ],
        out_specs=pl.BlockSpec((n, tm), lambda i: (0, i)),
        out_shape=jax.ShapeDtypeStruct((n, t), BF16),
        compiler_params=_params("parallel"),
        name="v_proj_t",
    )(x2, wt_bf16)


NAT_ROWS = 4
NAT_KROWS = NAT_ROWS + WIN_ROWS


def _natten_bias(rpb, n_rows):
    n_heads = rpb.shape[0]
    span = 2 * WIN_COLS - 1
    pad_lo = GRID_W - WIN_COLS
    padded = jnp.pad(rpb.astype(F32) * LOG2E, ((0, 0), (0, 0), (pad_lo, 2 * GRID_W - 1 - pad_lo - span)))
    toep = jnp.stack([padded[:, :, GRID_W - 1 - c:2 * GRID_W - 1 - c] for c in range(GRID_W)], axis=3)
    c = np.arange(GRID_W)[None, :]
    kc = np.arange(GRID_W)[:, None]
    c_start = np.clip(c - WIN_COLS // 2, 0, GRID_W - WIN_COLS)
    valid = (kc >= c_start) & (kc < c_start + WIN_COLS)
    toep = jnp.where(valid[None, None], toep, MASK_VALUE)
    masked = jnp.full((n_heads, GRID_W, GRID_W), MASK_VALUE, F32)
    n_blk = n_rows // NAT_ROWS
    kinds = []
    for blk in (0, 1, n_blk - 1):
        r0 = blk * NAT_ROWS
        ks = int(np.clip(r0 - WIN_ROWS // 2, 0, n_rows - NAT_KROWS))
        cols = []
        for a in range(NAT_ROWS):
            r = r0 + a
            rs = int(np.clip(r - WIN_ROWS // 2, 0, n_rows - WIN_ROWS))
            rows = []
            for i in range(NAT_KROWS):
                kr = ks + i
                rows.append(toep[:, kr - r + WIN_ROWS - 1] if rs <= kr < rs + WIN_ROWS else masked)
            cols.append(jnp.concatenate(rows, axis=1))
        kinds.append(jnp.concatenate(cols, axis=2))
    b = jnp.stack(kinds, axis=0)
    nk, nq = b.shape[2], b.shape[3]
    b = b.reshape(3, n_heads // 2, 2, nk, nq).transpose(0, 1, 3, 2, 4)
    return b.reshape(3, n_heads // 2, nk, 2 * nq)


def _natten_kernel(q_ref, k_ref, vt_ref, bias_ref, o_ref, raw_ref, cmax_ref, *, n_rows):
    nq = NAT_ROWS * GRID_W
    nk = NAT_KROWS * GRID_W
    n_blk = n_rows // NAT_ROWS
    lane = lax.broadcasted_iota(jnp.int32, (nq, LANES), 1)
    low = lane < HEAD_DIM

    def key_start(blk):
        ks = jnp.clip(blk * NAT_ROWS - WIN_ROWS // 2, 0, n_rows - NAT_KROWS)
        return pl.multiple_of(ks * GRID_W, NAT_ROWS * GRID_W)

    def scores(blk, slot):
        q = q_ref[pl.ds(pl.multiple_of(blk * nq, nq), nq), :]
        zero = jnp.zeros_like(q)
        qx = jnp.concatenate([jnp.where(low, q, zero), jnp.where(low, zero, q)], axis=0)
        k = k_ref[pl.ds(key_start(blk), nk), :]
        kind = jnp.where(blk == 0, 0, jnp.where(blk == n_blk - 1, 2, 1))
        raw = lax.dot_general(k, qx, NT_DIMS, preferred_element_type=F32) + bias_ref[kind, 0]
        raw_ref[slot] = raw
        cmax_ref[slot] = jnp.max(raw, axis=0, keepdims=True)

    def absorb(blk, slot):
        vt = vt_ref[:, pl.ds(key_start(blk), nk)]
        p = jnp.exp2(raw_ref[slot] - cmax_ref[slot])
        l = jnp.sum(p, axis=0, keepdims=True)
        res = jnp.dot(vt, p.astype(BF16), preferred_element_type=F32) / l
        o_t = jnp.concatenate([res[:HEAD_DIM, :nq], res[HEAD_DIM:, nq:]], axis=0)
        o_ref[pl.ds(pl.multiple_of(blk * nq, nq), nq), :] = o_t.T.astype(o_ref.dtype)

    scores(0, 0)

    def body(u, carry):
        blk = 2 * u
        scores(blk + 1, 1)
        absorb(blk, 0)
        scores(blk + 2, 0)
        absorb(blk + 1, 1)
        return carry

    assert n_blk % 2 == 0
    lax.fori_loop(0, n_blk // 2 - 1, body, 0)
    scores(n_blk - 1, 1)
    absorb(n_blk - 2, 0)
    absorb(n_blk - 1, 1)


def _natten(qk, vt, bias, batch, seq):
    n_rows = seq // GRID_W
    hp = N_HEADS_A // 2
    nq = NAT_ROWS * GRID_W
    nk = NAT_KROWS * GRID_W
    return pl.pallas_call(
        functools.partial(_natten_kernel, n_rows=n_rows),
        grid=(batch, hp),
        in_specs=[
            pl.BlockSpec((seq, LANES), lambda b, h: (b, h)),
            pl.BlockSpec((seq, LANES), lambda b, h: (b, hp + h)),
            pl.BlockSpec((LANES, seq), lambda b, h: (h, b)),
            pl.BlockSpec((3, 1, nk, 2 * nq), lambda b, h: (0, h, 0, 0)),
        ],
        out_specs=pl.BlockSpec((seq, LANES), lambda b, h: (b, h)),
        out_shape=jax.ShapeDtypeStruct((batch * seq, WIDTH_A), BF16),
        scratch_shapes=[pltpu.VMEM((2, nk, 2 * nq), F32), pltpu.VMEM((2, 1, 2 * nq), F32)],
        compiler_params=_params("parallel", "parallel"),
        name="natten",
    )(qk, qk, vt, bias)


def _diff_kernel(sl_ref, lam_ref, q_ref, k_ref, vt_ref, f_ref, pen_ref, g_ref, o_ref,
                 qop_ref, raw_ref, cmax_ref, coff_ref, acc_ref, knorm_ref, *, tq, tk, out_scale):
    h = pl.program_id(1)
    qi = pl.program_id(2)
    n_kt = k_ref.shape[0] // tk
    assert n_kt % 2 == 0
    sl = sl_ref[h]
    a1, a2, a3 = _bf16_split(LOG2E)

    q = q_ref[...]
    lane_q = lax.broadcasted_iota(jnp.int32, q.shape, 1)
    low_q = lane_q < HEAD_DIM
    zero = jnp.zeros_like(q)
    q1 = jnp.where(low_q, q, zero)
    q2 = jnp.where(low_q, zero, q)

    def q_ops(sign):
        if sign == 0:
            f = zero
        else:
            f = jnp.where(lane_q == 0, sign * a1,
                          jnp.where(lane_q == 1, sign * a2, jnp.where(lane_q == 2, sign * a3, 0.0))).astype(BF16)
        return jnp.concatenate([jnp.concatenate([q1, f], axis=1), jnp.concatenate([q2, f], axis=1)], axis=0)

    qop_ref[0] = q_ops(1.0)
    qop_ref[1] = q_ops(-1.0)
    fk = f_ref[0]
    pos = lax.broadcasted_iota(jnp.int32, (1, 2 * tq), 1)
    iq = (qi * tq + jnp.where(pos < tq, pos, pos - tq)).astype(F32)

    kt_mix = (qi * tq) // tk
    sel = qi - kt_mix * (tk // tq)

    @pl.when(qi == 0)
    def _():
        def tile_norm(kt, best):
            kk = k_ref[pl.ds(pl.multiple_of(kt * tk, tk), tk), :].astype(F32)
            kk = kk * kk
            low_k = lax.broadcasted_iota(jnp.int32, kk.shape, 1) < HEAD_DIM
            n1 = jnp.sum(jnp.where(low_k, kk, 0.0), axis=1, keepdims=True)
            n2 = jnp.sum(jnp.where(low_k, 0.0, kk), axis=1, keepdims=True)
            return jnp.maximum(best, jnp.max(jnp.maximum(n1, n2), axis=0, keepdims=True))

        best = lax.fori_loop(0, n_kt, tile_norm, jnp.zeros((1, 1), F32))
        knorm_ref[...] = jnp.sqrt(best)

    qq = q.astype(F32)
    qq = qq * qq
    qn1 = jnp.sum(jnp.where(low_q, qq, 0.0), axis=1, keepdims=True)
    qn2 = jnp.sum(jnp.where(low_q, 0.0, qq), axis=1, keepdims=True)
    qnorm = jnp.sqrt(jnp.max(jnp.maximum(qn1, qn2), axis=0, keepdims=True))
    reach = (SKIP_LOG2 + 2.0 * SKIP_MARGIN * qnorm * knorm_ref[...]) / sl
    q_lo = (qi * tq).astype(F32)
    kt_lo = jnp.maximum(jnp.floor((q_lo - reach - 1.0) / tk), 0.0).astype(jnp.int32)[0, 0]
    kt_hi = jnp.minimum(jnp.floor((q_lo + (tq + reach)) / tk), n_kt - 1.0).astype(jnp.int32)[0, 0]
    odd = lax.rem(kt_hi - kt_lo + 1, 2)
    widen_lo = jnp.where(kt_lo > 0, odd, 0)
    kt_lo = kt_lo - widen_lo
    kt_hi = kt_hi + (odd - widen_lo)
    n_steps = kt_hi - kt_lo + 1

    def load_k(kt):
        off = pl.multiple_of(kt * tk, tk)
        return jnp.concatenate([k_ref[pl.ds(off, tk), :], fk], axis=1), vt_ref[:, pl.ds(off, tk)]

    def tile_of(s):
        t = kt_lo + jnp.minimum(s, n_steps - 1) - 1
        return jnp.where(s == 0, kt_mix, t + (t >= kt_mix).astype(jnp.int32))

    def scores(s, slot):
        kt = tile_of(s)
        kx, _ = load_k(kt)
        raw = lax.dot_general(kx, qop_ref[(kt > kt_mix).astype(jnp.int32)], NT_DIMS, preferred_element_type=F32)
        c = -sl * jnp.abs(iq - (kt * tk + tk // 2).astype(F32))
        raw_ref[slot] = raw
        cmax_ref[slot] = jnp.max(raw, axis=0, keepdims=True) + c
        coff_ref[slot] = c

    def absorb(s, slot, state):
        m_prev, l_prev = state
        _, vt = load_k(tile_of(s))
        m_new = jnp.maximum(m_prev, cmax_ref[slot])
        a = jnp.exp2(m_prev - m_new)
        p = jnp.exp2(raw_ref[slot] - (m_new - coff_ref[slot]))
        l_new = a * l_prev + jnp.sum(p, axis=0, keepdims=True)
        acc_ref[...] = a * acc_ref[...] + jnp.dot(vt, p.astype(BF16), preferred_element_type=F32)
        return m_new, l_new

    acc_ref[...] = jnp.zeros_like(acc_ref)
    state = (jnp.full((1, 2 * tq), -jnp.inf, F32), jnp.zeros((1, 2 * tq), F32))

    kx, _ = load_k(kt_mix)
    pen = pen_ref[0, sel]
    raw = lax.dot_general(kx, q_ops(0), NT_DIMS, preferred_element_type=F32) - jnp.concatenate([pen, pen], axis=1)
    raw_ref[0] = raw
    cmax_ref[0] = jnp.max(raw, axis=0, keepdims=True)
    coff_ref[0] = jnp.zeros((1, 2 * tq), F32)

    def pair(s, st):
        scores(s + 1, 1)
        st = absorb(s, 0, st)
        scores(s + 2, 0)
        return absorb(s + 1, 1, st)

    n_quads = n_steps // 4
    state = lax.fori_loop(0, n_quads, lambda u, st: pair(4 * u + 2, pair(4 * u, st)), state)
    _, l_run = lax.fori_loop(2 * n_quads, n_steps // 2, lambda u, st: pair(2 * u, st), state)

    on = acc_ref[...] / l_run
    o = (on[:, :tq] - lam_ref[0] * on[:, tq:]).T
    o = o * lax.rsqrt(jnp.mean(o * o, axis=-1, keepdims=True) + RMS_EPS)
    o_ref[...] = (o * g_ref[0] * out_scale).astype(o_ref.dtype)


def _diff_tables(slopes, tq, tk):
    j = np.arange(tk)
    feat = slopes[:, None] * (j - tk // 2)[None, :]
    feat = np.broadcast_to(feat[:, :, None], (len(slopes), tk, LANES))
    i = np.arange(tq)
    pos = np.arange(tk // tq)[:, None, None] * tq + i[None, None, :]
    dist = np.abs(pos - j[None, :, None])
    pen = (slopes * LOG2E)[:, None, None, None] * dist[None]
    return jnp.asarray(feat, BF16), jnp.asarray(pen, F32)


def _diff_attention(qk, vt, lam, subln_g, lam_init, batch, seq, tq=512, tk=512):
    nq = seq // tq
    base = 2 * WIDTH_A // LANES
    hb = N_HEADS_B
    slopes = 2.0 ** (-8.0 * (np.arange(hb) + 1) / hb)
    feat, pen = _diff_tables(slopes, tq, tk)
    sl = jnp.asarray(slopes * LOG2E, F32)
    smem = pl.BlockSpec(memory_space=pltpu.SMEM)
    return pl.pallas_call(
        functools.partial(_diff_kernel, tq=tq, tk=tk, out_scale=1.0 - lam_init),
        grid=(batch, hb, nq),
        in_specs=[
            smem, smem,
            pl.BlockSpec((tq, LANES), lambda b, h, i: (b * nq + i, base + h)),
            pl.BlockSpec((seq, LANES), lambda b, h, i: (b, base + hb + h)),
            pl.BlockSpec((LANES, seq), lambda b, h, i: (WIDTH_A // LANES + h, b)),
            pl.BlockSpec((1, tk, LANES), lambda b, h, i: (h, 0, 0)),
            pl.BlockSpec((1, tk // tq, tk, tq), lambda b, h, i: (h, 0, 0, 0)),
            pl.BlockSpec((1, 1, LANES), lambda b, h, i: (h, 0, 0)),
        ],
        out_specs=pl.BlockSpec((tq, LANES), lambda b, h, i: (b * nq + i, h)),
        out_shape=jax.ShapeDtypeStruct((batch * seq, WIDTH_B), BF16),
        scratch_shapes=[
            pltpu.VMEM((2, 2 * tq, 2 * LANES), BF16),
            pltpu.VMEM((2, tk, 2 * tq), F32),
            pltpu.VMEM((2, 1, 2 * tq), F32),
            pltpu.VMEM((2, 1, 2 * tq), F32),
            pltpu.VMEM((LANES, 2 * tq), F32),
            pltpu.VMEM((1, 1), F32),
        ],
        compiler_params=_params("arbitrary", "arbitrary", "arbitrary"),
        name="diff_attn",
    )(sl, lam, qk, qk, vt, feat, pen, subln_g.astype(F32).reshape(hb, 1, LANES))


def _attn_out_kernel(oa_ref, ob_ref, wa_ref, wb_ref, x_ref, g_ref, b_ref, o_ref):
    h = jnp.dot(oa_ref[...], wa_ref[...], preferred_element_type=F32)
    h = h + jnp.dot(ob_ref[...], wb_ref[...], preferred_element_type=F32)
    o_ref[...] = _layer_norm(ALPHA * x_ref[...] + h, g_ref[...], b_ref[...])


def _attn_out(oa, ob, w_out_bf16, x2, g, b, tm=512):
    t, d = x2.shape
    wa, wb = w_out_bf16[:WIDTH_A], w_out_bf16[WIDTH_A:]
    row = lambda w: pl.BlockSpec((tm, w), lambda i: (i, 0))
    full = lambda a: pl.BlockSpec(a.shape, lambda i: (0,) * a.ndim)
    return pl.pallas_call(
        _attn_out_kernel,
        grid=(t // tm,),
        in_specs=[row(WIDTH_A), row(WIDTH_B), full(wa), full(wb), row(d), full(g), full(b)],
        out_specs=row(d),
        out_shape=jax.ShapeDtypeStruct((t, d), F32),
        compiler_params=_params("parallel"),
        name="attn_out_ln",
    )(oa, ob, wa, wb, x2, g, b)


def _gmlp_kernel(x_ref, win_ref, lg_ref, lb_ref, ws_ref, bs_ref, wout_ref, g_ref, b_ref, o_ref, *, tm):
    x = x_ref[...]
    width = wout_ref.shape[0]
    gw = width // GMLP_GROUPS
    z = jnp.dot(x.astype(BF16), win_ref[...], preferred_element_type=F32)
    z = 0.5 * z * (1.0 + lax.erf(z * (2.0 ** -0.5)))
    u = z[:, :width]
    v = _layer_norm(z[:, width:], lg_ref[...], lb_ref[...]).astype(BF16)
    chunks = []
    for n in range(tm // CHUNK):
        groups = []
        for gi in range(GMLP_GROUPS):
            vg = v[n * CHUNK:(n + 1) * CHUNK, gi * gw:(gi + 1) * gw]
            groups.append(jnp.dot(ws_ref[gi], vg, preferred_element_type=F32))
        chunks.append(jnp.concatenate(groups, axis=1) + bs_ref[...])
    sv = jnp.concatenate(chunks, axis=0)
    h = jnp.dot((u * sv).astype(BF16), wout_ref[...], preferred_element_type=F32)
    o_ref[...] = _layer_norm(ALPHA * x + h, g_ref[...], b_ref[...])


def _gmlp_layer(x2, w_in, ln_g, ln_b, w_s, bs_full, w_out, g, b, tm=256):
    t, d = x2.shape
    row = pl.BlockSpec((tm, d), lambda i: (i, 0))
    full = lambda a: pl.BlockSpec(a.shape, lambda i: (0,) * a.ndim)
    args = (w_in, ln_g, ln_b, w_s, bs_full, w_out, g, b)
    return pl.pallas_call(
        functools.partial(_gmlp_kernel, tm=tm),
        grid=(t // tm,),
        in_specs=[row] + [full(a) for a in args],
        out_specs=row,
        out_shape=jax.ShapeDtypeStruct((t, d), F32),
        compiler_params=_params("parallel"),
        name="gmlp_layer",
    )(x2, *args)


def _router_kernel(x_ref, wr_ref, o_ref):
    logits = lax.dot_general(wr_ref[...], x_ref[...], NT_DIMS, preferred_element_type=F32,
                             precision=lax.Precision.HIGHEST)
    m = jnp.max(logits, axis=0, keepdims=True)
    p = jnp.exp(logits - m)
    o_ref[0] = p / jnp.sum(p, axis=0, keepdims=True)


def _router(x2, w_router_t, batch, seq, tm=1024):
    d = x2.shape[1]
    ns = seq // tm
    return pl.pallas_call(
        _router_kernel,
        grid=(batch, ns),
        in_specs=[
            pl.BlockSpec((tm, d), lambda b, i: (b * ns + i, 0)),
            pl.BlockSpec((N_EXPERTS, d), lambda b, i: (0, 0)),
        ],
        out_specs=pl.BlockSpec((1, N_EXPERTS, tm), lambda b, i: (b, 0, i)),
        out_shape=jax.ShapeDtypeStruct((batch, N_EXPERTS, seq), F32),
        compiler_params=_params("parallel", "parallel"),
        name="router",
    )(x2, w_router_t)


def _route_kernel(aff_ref, idx_ref, gate_ref, thr_ref, *, cap):
    ne, nt = aff_ref.shape[1], aff_ref.shape[2]
    a_all = aff_ref[0]
    u_all = pltpu.bitcast(a_all, jnp.int32)

    def add_bit(i, thr):
        cand = thr | jnp.left_shift(1, 30 - i)
        count = jnp.sum(jnp.where(u_all >= cand, 1.0, 0.0), axis=(1, 2), keepdims=True)
        return jnp.where(count >= cap, cand, thr)

    thr_ref[...] = lax.fori_loop(0, 31, add_bit, jnp.zeros((ne, 1, 1), jnp.int32))

    lane = lax.broadcasted_iota(jnp.int32, (LANES, LANES), 1)
    sub = lax.broadcasted_iota(jnp.int32, (LANES, LANES), 0)
    tri_incl = jnp.where(sub <= lane, 1.0, 0.0).astype(BF16)
    r64 = lax.broadcasted_iota(jnp.int32, (nt, nt), 0)
    c64 = lax.broadcasted_iota(jnp.int32, (nt, nt), 1)
    low_strict = jnp.where(c64 < r64, 1.0, 0.0).astype(BF16)
    up_incl = jnp.where(r64 <= c64, 1.0, 0.0).astype(BF16)
    ones_rows = jnp.ones((8, LANES), BF16)
    slot = lax.broadcasted_iota(jnp.int32, (cap, 1), 0).astype(F32)
    lane_nt = lax.broadcasted_iota(jnp.int32, (cap, nt), 1)
    lane_f = lax.broadcasted_iota(jnp.int32, (cap, LANES), 1).astype(F32)

    def cumsum_tokens(mask_f32):
        mb = mask_f32.astype(BF16)
        local = jnp.dot(mb, tri_incl, preferred_element_type=F32)
        n_b = jnp.broadcast_to(jnp.sum(mask_f32, axis=1, keepdims=True), mask_f32.shape)
        before = jnp.dot(low_strict, n_b.astype(BF16), preferred_element_type=F32)
        return before + local, local

    def one_expert(e, carry):
        a = aff_ref[0, e]
        u = pltpu.bitcast(a, jnp.int32)
        t = thr_ref[e]
        gt = u > t
        eq = u == t
        need = cap - jnp.sum(jnp.where(gt, 1.0, 0.0), axis=(0, 1), keepdims=True)
        eq_f = jnp.where(eq, 1.0, 0.0)
        eq_incl, _ = cumsum_tokens(eq_f)
        sel = jnp.logical_or(gt, jnp.logical_and(eq, eq_incl - eq_f < need))
        sel_f = jnp.where(sel, 1.0, 0.0)
        _, local = cumsum_tokens(sel_f)
        n_row = lax.dot_general(ones_rows, sel_f.astype(BF16), NT_DIMS, preferred_element_type=F32)[:1]
        incl_row = jnp.dot(jnp.broadcast_to(n_row, (8, nt)).astype(BF16), up_incl, preferred_element_type=F32)[:1]
        excl_row = incl_row - n_row
        krow = jnp.sum(jnp.where(incl_row <= slot, 1.0, 0.0), axis=1, keepdims=True)
        onehot = jnp.where(lane_nt == krow.astype(jnp.int32), 1.0, 0.0)
        rank = slot - jnp.sum(onehot * excl_row, axis=1, keepdims=True)
        row_counts = jnp.dot(onehot.astype(BF16), local.astype(BF16), preferred_element_type=F32)
        pos = jnp.sum(jnp.where(row_counts <= rank, 1.0, 0.0), axis=1, keepdims=True)
        idx_ref[0, e] = (krow * LANES + pos).astype(jnp.int32)
        row_aff = jnp.dot(onehot, a, preferred_element_type=F32, precision=lax.Precision.HIGHEST)
        gate_ref[0, e] = jnp.sum(jnp.where(lane_f == pos, row_aff, 0.0), axis=1, keepdims=True)
        return carry

    lax.fori_loop(0, ne, one_expert, 0)


def _route(aff_t, cap):
    batch, ne, seq = aff_t.shape
    nt = seq // LANES
    out = jax.ShapeDtypeStruct((batch, ne, cap, 1), jnp.int32), jax.ShapeDtypeStruct((batch, ne, cap, 1), F32)
    spec = pl.BlockSpec((1, ne, cap, 1), lambda b: (b, 0, 0, 0))
    return pl.pallas_call(
        functools.partial(_route_kernel, cap=cap),
        grid=(batch,),
        in_specs=[pl.BlockSpec((1, ne, nt, LANES), lambda b: (b, 0, 0, 0))],
        out_specs=(spec, spec),
        out_shape=out,
        scratch_shapes=[pltpu.VMEM((ne, 1, 1), jnp.int32)],
        compiler_params=_params("parallel"),
        name="route",
    )(aff_t.reshape(batch, ne, nt, LANES))


def _expert_kernel(x_ref, wg_ref, wu_ref, wd_ref, gate_ref, o_ref):
    f = pl.program_id(2)
    x = x_ref[0, 0]
    a = jnp.dot(x, wg_ref[0, 0].astype(BF16), preferred_element_type=F32)
    u = jnp.dot(x, wu_ref[0, 0].astype(BF16), preferred_element_type=F32)
    hmid = (a * jax.nn.sigmoid(a) * u).astype(BF16)
    y = jnp.dot(hmid, wd_ref[0, 0].astype(BF16), preferred_element_type=F32)

    @pl.when(f == 0)
    def _():
        o_ref[0, 0] = y

    @pl.when(f > 0)
    def _():
        o_ref[0, 0] += y

    @pl.when(f == pl.num_programs(2) - 1)
    def _():
        o_ref[0, 0] = o_ref[0, 0] * gate_ref[0, 0]


def _experts(xg, weights, gate, tf=1024):
    wg, wu, wd, layer = weights
    batch, ne, cap, d = xg.shape
    dff = wg.shape[3]
    return pl.pallas_call(
        _expert_kernel,
        grid=(ne, batch, dff // tf),
        in_specs=[
            pl.BlockSpec((1, 1, cap, d), lambda e, b, f: (b, e, 0, 0)),
            pl.BlockSpec((1, 1, d, tf), lambda e, b, f: (layer, e, 0, f)),
            pl.BlockSpec((1, 1, d, tf), lambda e, b, f: (layer, e, 0, f)),
            pl.BlockSpec((1, 1, tf, d), lambda e, b, f: (layer, e, f, 0)),
            pl.BlockSpec((1, 1, cap, 1), lambda e, b, f: (b, e, 0, 0)),
        ],
        out_specs=pl.BlockSpec((1, 1, cap, d), lambda e, b, f: (b, e, 0, 0)),
        out_shape=jax.ShapeDtypeStruct((batch, ne, cap, d), F32),
        compiler_params=_params("parallel", "parallel", "arbitrary"),
        name="experts",
    )(xg, wg, wu, wd, gate)


def _gather_kernel(cur_ref, next_ref, x_hbm, o_ref, buf, sem, *, cap):
    g = pl.program_id(0)
    last = pl.num_programs(0) - 1
    slot = lax.rem(g, 2)

    def issue_all(idx_ref, buf_slot):
        def issue(r, carry):
            pltpu.make_async_copy(x_hbm.at[pl.ds(idx_ref[0, 0, r], 1), :], buf.at[buf_slot, pl.ds(r, 1), :],
                                  sem.at[buf_slot]).start()
            return carry

        lax.fori_loop(0, cap, issue, 0, unroll=8)

    @pl.when(g == 0)
    def _():
        issue_all(cur_ref, 0)

    @pl.when(g < last)
    def _():
        issue_all(next_ref, 1 - slot)

    pltpu.make_async_copy(x_hbm.at[pl.ds(0, cap), :], buf.at[slot], sem.at[slot]).wait()
    o_ref[...] = buf[slot].astype(o_ref.dtype)


def _gather_rows(x2, rows, cap):
    g = rows.shape[0]
    d = x2.shape[1]
    smem = lambda fn: pl.BlockSpec((1, 1, cap), fn, memory_space=pltpu.SMEM)
    return pl.pallas_call(
        functools.partial(_gather_kernel, cap=cap),
        grid=(g,),
        in_specs=[smem(lambda i: (i, 0, 0)), smem(lambda i: (jnp.minimum(i + 1, g - 1), 0, 0)),
                  pl.BlockSpec(memory_space=pl.ANY)],
        out_specs=pl.BlockSpec((cap, d), lambda i: (i, 0)),
        out_shape=jax.ShapeDtypeStruct((g * cap, d), BF16),
        scratch_shapes=[pltpu.VMEM((2, cap, d), F32), pltpu.SemaphoreType.DMA((2,))],
        compiler_params=_params("arbitrary"),
        name="gather_rows",
    )(rows, rows, x2)


COMBINE_ROWS_PER_STEP = 8
COMBINE_LN_ROWS = 512


def _combine_kernel(idx_ref, y_ref, x_hbm, g_ref, b_ref, o_hbm, acc_ref, xbuf, sem_x, sem_o, *, cap, seq):
    b = pl.program_id(0)
    e = pl.program_id(1)

    @pl.when(e == 0)
    def _():
        acc_ref[...] = jnp.zeros_like(acc_ref)

    def step(i, carry):
        r0 = pl.multiple_of(i * COMBINE_ROWS_PER_STEP, COMBINE_ROWS_PER_STEP)
        toks = [idx_ref[0, 0, r0 + j] for j in range(COMBINE_ROWS_PER_STEP)]
        rows = [acc_ref[pl.ds(toks[j], 1), :] + y_ref[0, 0, pl.ds(r0 + j, 1), :] for j in range(COMBINE_ROWS_PER_STEP)]
        for j in range(COMBINE_ROWS_PER_STEP):
            acc_ref[pl.ds(toks[j], 1), :] = rows[j]
        return carry

    lax.fori_loop(0, cap // COMBINE_ROWS_PER_STEP, step, 0)

    @pl.when(e == pl.num_programs(1) - 1)
    def _():
        tl = COMBINE_LN_ROWS
        n_tiles = seq // tl
        row0 = b * seq

        def x_copy(i, slot):
            return pltpu.make_async_copy(x_hbm.at[pl.ds(pl.multiple_of(row0 + i * tl, tl), tl), :], xbuf.at[slot],
                                         sem_x.at[slot])

        x_copy(0, 0).start()

        def tile(i, carry):
            slot = lax.rem(i, 2)

            @pl.when(i + 1 < n_tiles)
            def _():
                x_copy(i + 1, 1 - slot).start()

            x_copy(i, slot).wait()
            rows = pl.ds(pl.multiple_of(i * tl, tl), tl)
            acc_ref[rows, :] = _layer_norm(ALPHA * xbuf[slot] + acc_ref[rows, :], g_ref[...], b_ref[...])
            pltpu.make_async_copy(acc_ref.at[rows, :], o_hbm.at[pl.ds(pl.multiple_of(row0 + i * tl, tl), tl), :],
                                  sem_o).start()
            return carry

        lax.fori_loop(0, n_tiles, tile, 0)
        pltpu.make_async_copy(acc_ref, o_hbm.at[pl.ds(pl.multiple_of(row0, seq), seq), :], sem_o).wait()


def _combine_ln(y, idx, x2, g, b, seq):
    batch, ne, cap, d = y.shape
    vec = pl.BlockSpec((1, d), lambda bb, ee: (0, 0))
    return pl.pallas_call(
        functools.partial(_combine_kernel, cap=cap, seq=seq),
        grid=(batch, ne),
        in_specs=[
            pl.BlockSpec((1, 1, cap), lambda bb, ee: (bb * ne + ee, 0, 0), memory_space=pltpu.SMEM),
            pl.BlockSpec((1, 1, cap, d), lambda bb, ee: (bb, ee, 0, 0)),
            pl.BlockSpec(memory_space=pl.ANY), vec, vec,
        ],
        out_specs=pl.BlockSpec(memory_space=pl.ANY),
        out_shape=jax.ShapeDtypeStruct((batch * seq, d), F32),
        scratch_shapes=[pltpu.VMEM((seq, d), F32), pltpu.VMEM((2, COMBINE_LN_ROWS, d), F32),
                        pltpu.SemaphoreType.DMA((2,)), pltpu.SemaphoreType.DMA],
        compiler_params=_params("arbitrary", "arbitrary"),
        name="combine_ln",
    )(idx.reshape(batch * ne, 1, cap), y, x2, g, b)


def _moe_layer(x2, w_router, weights, g, b, batch, seq):
    d = x2.shape[1]
    cap = EC_FACTOR * seq // N_EXPERTS
    aff_t = _router(x2, w_router.astype(F32).T, batch, seq)
    idx, gate = _route(aff_t, cap)
    idx = idx.reshape(batch, N_EXPERTS, cap)
    rows = idx + (jnp.arange(batch, dtype=idx.dtype) * seq)[:, None, None]
    xg = _gather_rows(x2, rows.reshape(batch * N_EXPERTS, 1, cap), cap).reshape(batch, N_EXPERTS, cap, d)
    y = _experts(xg, weights, gate)
    return _combine_ln(y, idx, x2, g, b, seq)


def _lambda_init(layer_number):
    return 0.8 - 0.6 * float(np.exp(-0.3 * (layer_number - 1)))


def _row(v):
    return v.astype(F32).reshape(1, -1)


def kernel(x, w_in_ab, rpb_a, lambda_qk, subln_g, w_out_ab, w_in_c, ln_v_g, ln_v_b, w_s, b_s,
           w_out_c, ln_mix_g, ln_mix_b, w_router, w_gate, w_up, w_down, ln_ffn_g, ln_ffn_b):
    batch, seq, d = x.shape
    x2 = x.astype(F32).reshape(batch * seq, d)
    v_a = slice(2 * WIDTH_A, 3 * WIDTH_A)
    v_b = slice(3 * WIDTH_A + 2 * WIDTH_B, 3 * WIDTH_A + 3 * WIDTH_B)
    col_scale = np.ones((1, 2 * WIDTH_A + 2 * WIDTH_B), np.float32)
    col_scale[:, :WIDTH_A] = QK_SCALE * LOG2E
    col_scale[:, 2 * WIDTH_A:2 * WIDTH_A + WIDTH_B] = QK_SCALE * LOG2E
    col_scale = jnp.asarray(col_scale)

    for l in range(DEPTH):
        i = l // 2
        if l % 2 == 0:
            lam_init = _lambda_init(l + 1)
            lf = lambda_qk[i].astype(F32)
            lam = jnp.exp(jnp.sum(lf[0] * lf[1])) - jnp.exp(jnp.sum(lf[2] * lf[3])) + lam_init
            w_in = w_in_ab[i].astype(BF16)
            qk = _project(x2, jnp.concatenate([w_in[:, :v_a.start], w_in[:, v_a.stop:v_b.start]], axis=1), col_scale)
            vt = _project_t(x2, jnp.concatenate([w_in[:, v_a], w_in[:, v_b]], axis=1).T)
            oa = _natten(qk, vt, _natten_bias(rpb_a[i], seq // GRID_W), batch, seq)
            ob = _diff_attention(qk, vt, lam.reshape(1), subln_g[i], lam_init, batch, seq)
            x2 = _attn_out(oa, ob, w_out_ab[i].astype(BF16), x2, _row(ln_mix_g[l]), _row(ln_mix_b[l]))
        else:
            gw = w_in_c.shape[2] // 2 // GMLP_GROUPS
            bs_full = jnp.repeat(b_s[i].astype(F32).T, gw, axis=1)
            x2 = _gmlp_layer(x2, w_in_c[i].astype(BF16), _row(ln_v_g[i]), _row(ln_v_b[i]),
                             w_s[i].astype(BF16), bs_full, w_out_c[i].astype(BF16),
                             _row(ln_mix_g[l]), _row(ln_mix_b[l]))
        x2 = _moe_layer(x2, w_router[l], (w_gate, w_up, w_down, l),
                        _row(ln_ffn_g[l]), _row(ln_ffn_b[l]), batch, seq)
    return x2.reshape(batch, seq, d).astype(x.dtype)
```

```python
import functools

import jax
import jax.numpy as jnp
import numpy as np
from jax import lax
from jax.experimental import pallas as pl
from jax.experimental.pallas import tpu as pltpu

F32 = jnp.float32
BF16 = jnp.bfloat16

DEPTH = 4
GRID_W = 64
HEAD_DIM = 64
N_HEADS_A = 8
WIN_ROWS = 8
WIN_COLS = 16
WIDTH_A = N_HEADS_A * HEAD_DIM
N_HEADS_B = 4
WIDTH_B = N_HEADS_B * 2 * HEAD_DIM
CHUNK = 128
GMLP_GROUPS = 8
N_EXPERTS = 16
EC_FACTOR = 2
LN_EPS = 1e-5
RMS_EPS = 1e-6
ALPHA = (2.0 * DEPTH) ** 0.25
QK_SCALE = HEAD_DIM ** -0.5
MASK_VALUE = -1e30
LOG2E = float(np.log2(np.e))
SKIP_LOG2 = 48.0
SKIP_MARGIN = 1.02

LANES = 128
SUBLANES = 8
VMEM_LIMIT = 56 * 1024 * 1024

NT_DIMS = (((1,), (1,)), ((), ()))


def _params(*sem):
    return pltpu.CompilerParams(dimension_semantics=sem, vmem_limit_bytes=VMEM_LIMIT)


def _layer_norm(y, g, b):
    mu = jnp.mean(y, axis=-1, keepdims=True)
    yc = y - mu
    var = jnp.mean(yc * yc, axis=-1, keepdims=True)
    return yc * lax.rsqrt(var + LN_EPS) * g + b


def _bf16_split(value, parts=3):
    out, rest = [], np.float64(value)
    for _ in range(parts):
        piece = float(np.asarray(rest, np.float32).astype(jnp.bfloat16).astype(np.float64))
        out.append(piece)
        rest = rest - piece
    return out


def _proj_kernel(x_ref, w_ref, s_ref, o_ref):
    acc = jnp.dot(x_ref[...].astype(BF16), w_ref[...], preferred_element_type=F32)
    o_ref[...] = (acc * s_ref[...]).astype(o_ref.dtype)


def _project(x2, w_bf16, col_scale, tm=1024, tn=512):
    t, k = x2.shape
    n = w_bf16.shape[1]
    return pl.pallas_call(
        _proj_kernel,
        grid=(t // tm, n // tn),
        in_specs=[
            pl.BlockSpec((tm, k), lambda i, j: (i, 0)),
            pl.BlockSpec((k, tn), lambda i, j: (0, j)),
            pl.BlockSpec((1, tn), lambda i, j: (0, j)),
        ],
        out_specs=pl.BlockSpec((tm, tn), lambda i, j: (i, j)),
        out_shape=jax.ShapeDtypeStruct((t, n), BF16),
        compiler_params=_params("parallel", "arbitrary"),
        name="qkv_proj",
    )(x2, w_bf16, col_scale)


def _proj_t_kernel(x_ref, wt_ref, o_ref):
    acc = lax.dot_general(wt_ref[...], x_ref[...].astype(BF16), NT_DIMS, preferred_element_type=F32)
    o_ref[...] = acc.astype(o_ref.dtype)


def _project_t(x2, wt_bf16, tm=1024):
    t, k = x2.shape
    n = wt_bf16.shape[0]
    return pl.pallas_call(
        _proj_t_kernel,
        grid=(t // tm,),
        in_specs=[
            pl.BlockSpec((tm, k), lambda i: (i, 0)),
            pl.BlockSpec((n, k), lambda i: (0, 0)),
        ],
        out_specs=pl.BlockSpec((n, tm), lambda i: (0, i)),
        out_shape=jax.ShapeDtypeStruct((n, t), BF16),
        compiler_params=_params("parallel"),
        name="v_proj_t",
    )(x2, wt_bf16)


NAT_ROWS = 4
NAT_KROWS = NAT_ROWS + WIN_ROWS


def _natten_bias(rpb, n_rows):
    n_heads = rpb.shape[0]
    span = 2 * WIN_COLS - 1
    pad_lo = GRID_W - WIN_COLS
    padded = jnp.pad(rpb.astype(F32) * LOG2E, ((0, 0), (0, 0), (pad_lo, 2 * GRID_W - 1 - pad_lo - span)))
    toep = jnp.stack([padded[:, :, GRID_W - 1 - c:2 * GRID_W - 1 - c] for c in range(GRID_W)], axis=3)
    c = np.arange(GRID_W)[None, :]
    kc = np.arange(GRID_W)[:, None]
    c_start = np.clip(c - WIN_COLS // 2, 0, GRID_W - WIN_COLS)
    valid = (kc >= c_start) & (kc < c_start + WIN_COLS)
    toep = jnp.where(valid[None, None], toep, MASK_VALUE)
    masked = jnp.full((n_heads, GRID_W, GRID_W), MASK_VALUE, F32)
    n_blk = n_rows // NAT_ROWS
    kinds = []
    for blk in (0, 1, n_blk - 1):
        r0 = blk * NAT_ROWS
        ks = int(np.clip(r0 - WIN_ROWS // 2, 0, n_rows - NAT_KROWS))
        cols = []
        for a in range(NAT_ROWS):
            r = r0 + a
            rs = int(np.clip(r - WIN_ROWS // 2, 0, n_rows - WIN_ROWS))
            rows = []
            for i in range(NAT_KROWS):
                kr = ks + i
                rows.append(toep[:, kr - r + WIN_ROWS - 1] if rs <= kr < rs + WIN_ROWS else masked)
            cols.append(jnp.concatenate(rows, axis=1))
        kinds.append(jnp.concatenate(cols, axis=2))
    b = jnp.stack(kinds, axis=0)
    nk, nq = b.shape[2], b.shape[3]
    b = b.reshape(3, n_heads // 2, 2, nk, nq).transpose(0, 1, 3, 2, 4)
    return b.reshape(3, n_heads // 2, nk, 2 * nq)


def _natten_kernel(q_ref, k_ref, vt_ref, bias_ref, o_ref, raw_ref, cmax_ref, *, n_rows):
    nq = NAT_ROWS * GRID_W
    nk = NAT_KROWS * GRID_W
    n_blk = n_rows // NAT_ROWS
    lane = lax.broadcasted_iota(jnp.int32, (nq, LANES), 1)
    low = lane < HEAD_DIM

    def key_start(blk):
        ks = jnp.clip(blk * NAT_ROWS - WIN_ROWS // 2, 0, n_rows - NAT_KROWS)
        return pl.multiple_of(ks * GRID_W, NAT_ROWS * GRID_W)

    def scores(blk, slot):
        q = q_ref[pl.ds(pl.multiple_of(blk * nq, nq), nq), :]
        zero = jnp.zeros_like(q)
        qx = jnp.concatenate([jnp.where(low, q, zero), jnp.where(low, zero, q)], axis=0)
        k = k_ref[pl.ds(key_start(blk), nk), :]
        kind = jnp.where(blk == 0, 0, jnp.where(blk == n_blk - 1, 2, 1))
        raw = lax.dot_general(k, qx, NT_DIMS, preferred_element_type=F32) + bias_ref[kind, 0]
        raw_ref[slot] = raw
        cmax_ref[slot] = jnp.max(raw, axis=0, keepdims=True)

    def absorb(blk, slot):
        vt = vt_ref[:, pl.ds(key_start(blk), nk)]
        p = jnp.exp2(raw_ref[slot] - cmax_ref[slot])
        l = jnp.sum(p, axis=0, keepdims=True)
        res = jnp.dot(vt, p.astype(BF16), preferred_element_type=F32) / l
        o_t = jnp.concatenate([res[:HEAD_DIM, :nq], res[HEAD_DIM:, nq:]], axis=0)
        o_ref[pl.ds(pl.multiple_of(blk * nq, nq), nq), :] = o_t.T.astype(o_ref.dtype)

    scores(0, 0)

    def body(u, carry):
        blk = 2 * u
        scores(blk + 1, 1)
        absorb(blk, 0)
        scores(blk + 2, 0)
        absorb(blk + 1, 1)
        return carry

    assert n_blk % 2 == 0
    lax.fori_loop(0, n_blk // 2 - 1, body, 0)
    scores(n_blk - 1, 1)
    absorb(n_blk - 2, 0)
    absorb(n_blk - 1, 1)


def _natten(qk, vt, bias, batch, seq):
    n_rows = seq // GRID_W
    hp = N_HEADS_A // 2
    nq = NAT_ROWS * GRID_W
    nk = NAT_KROWS * GRID_W
    return pl.pallas_call(
        functools.partial(_natten_kernel, n_rows=n_rows),
        grid=(batch, hp),
        in_specs=[
            pl.BlockSpec((seq, LANES), lambda b, h: (b, h)),
            pl.BlockSpec((seq, LANES), lambda b, h: (b, hp + h)),
            pl.BlockSpec((LANES, seq), lambda b, h: (h, b)),
            pl.BlockSpec((3, 1, nk, 2 * nq), lambda b, h: (0, h, 0, 0)),
        ],
        out_specs=pl.BlockSpec((seq, LANES), lambda b, h: (b, h)),
        out_shape=jax.ShapeDtypeStruct((batch * seq, WIDTH_A), BF16),
        scratch_shapes=[pltpu.VMEM((2, nk, 2 * nq), F32), pltpu.VMEM((2, 1, 2 * nq), F32)],
        compiler_params=_params("parallel", "parallel"),
        name="natten",
    )(qk, qk, vt, bias)


def _diff_kernel(sl_ref, lam_ref, q_ref, k_ref, vt_ref, f_ref, pen_ref, g_ref, o_ref,
                 qop_ref, raw_ref, cmax_ref, coff_ref, acc_ref, knorm_ref, *, tq, tk, out_scale):
    h = pl.program_id(1)
    qi = pl.program_id(2)
    n_kt = k_ref.shape[0] // tk
    assert n_kt % 2 == 0
    sl = sl_ref[h]
    a1, a2, a3 = _bf16_split(LOG2E)

    q = q_ref[...]
    lane_q = lax.broadcasted_iota(jnp.int32, q.shape, 1)
    low_q = lane_q < HEAD_DIM
    zero = jnp.zeros_like(q)
    q1 = jnp.where(low_q, q, zero)
    q2 = jnp.where(low_q, zero, q)

    def q_ops(sign):
        if sign == 0:
            f = zero
        else:
            f = jnp.where(lane_q == 0, sign * a1,
                          jnp.where(lane_q == 1, sign * a2, jnp.where(lane_q == 2, sign * a3, 0.0))).astype(BF16)
        return jnp.concatenate([jnp.concatenate([q1, f], axis=1), jnp.concatenate([q2, f], axis=1)], axis=0)

    qop_ref[0] = q_ops(1.0)
    qop_ref[1] = q_ops(-1.0)
    fk = f_ref[0]
    pos = lax.broadcasted_iota(jnp.int32, (1, 2 * tq), 1)
    iq = (qi * tq + jnp.where(pos < tq, pos, pos - tq)).astype(F32)

    kt_mix = (qi * tq) // tk
    sel = qi - kt_mix * (tk // tq)

    @pl.when(qi == 0)
    def _():
        def tile_norm(kt, best):
            kk = k_ref[pl.ds(pl.multiple_of(kt * tk, tk), tk), :].astype(F32)
            kk = kk * kk
            low_k = lax.broadcasted_iota(jnp.int32, kk.shape, 1) < HEAD_DIM
            n1 = jnp.sum(jnp.where(low_k, kk, 0.0), axis=1, keepdims=True)
            n2 = jnp.sum(jnp.where(low_k, 0.0, kk), axis=1, keepdims=True)
            return jnp.maximum(best, jnp.max(jnp.maximum(n1, n2), axis=0, keepdims=True))

        best = lax.fori_loop(0, n_kt, tile_norm, jnp.zeros((1, 1), F32))
        knorm_ref[...] = jnp.sqrt(best)

    qq = q.astype(F32)
    qq = qq * qq
    qn1 = jnp.sum(jnp.where(low_q, qq, 0.0), axis=1, keepdims=True)
    qn2 = jnp.sum(jnp.where(low_q, 0.0, qq), axis=1, keepdims=True)
    qnorm = jnp.sqrt(jnp.max(jnp.maximum(qn1, qn2), axis=0, keepdims=True))
    reach = (SKIP_LOG2 + 2.0 * SKIP_MARGIN * qnorm * knorm_ref[...]) / sl
    q_lo = (qi * tq).astype(F32)
    kt_lo = jnp.maximum(jnp.floor((q_lo - reach - 1.0) / tk), 0.0).astype(jnp.int32)[0, 0]
    kt_hi = jnp.minimum(jnp.floor((q_lo + (tq + reach)) / tk), n_kt - 1.0).astype(jnp.int32)[0, 0]
    odd = lax.rem(kt_hi - kt_lo + 1, 2)
    widen_lo = jnp.where(kt_lo > 0, odd, 0)
    kt_lo = kt_lo - widen_lo
    kt_hi = kt_hi + (odd - widen_lo)
    n_steps = kt_hi - kt_lo + 1

    def load_k(kt):
        off = pl.multiple_of(kt * tk, tk)
        return jnp.concatenate([k_ref[pl.ds(off, tk), :], fk], axis=1), vt_ref[:, pl.ds(off, tk)]

    def tile_of(s):
        t = kt_lo + jnp.minimum(s, n_steps - 1) - 1
        return jnp.where(s == 0, kt_mix, t + (t >= kt_mix).astype(jnp.int32))

    def scores(s, slot):
        kt = tile_of(s)
        kx, _ = load_k(kt)
        raw = lax.dot_general(kx, qop_ref[(kt > kt_mix).astype(jnp.int32)], NT_DIMS, preferred_element_type=F32)
        c = -sl * jnp.abs(iq - (kt * tk + tk // 2).astype(F32))
        raw_ref[slot] = raw
        cmax_ref[slot] = jnp.max(raw, axis=0, keepdims=True) + c
        coff_ref[slot] = c

    def absorb(s, slot, state):
        m_prev, l_prev = state
        _, vt = load_k(tile_of(s))
        m_new = jnp.maximum(m_prev, cmax_ref[slot])
        a = jnp.exp2(m_prev - m_new)
        p = jnp.exp2(raw_ref[slot] - (m_new - coff_ref[slot]))
        l_new = a * l_prev + jnp.sum(p, axis=0, keepdims=True)
        acc_ref[...] = a * acc_ref[...] + jnp.dot(vt, p.astype(BF16), preferred_element_type=F32)
        return m_new, l_new

    acc_ref[...] = jnp.zeros_like(acc_ref)
    state = (jnp.full((1, 2 * tq), -jnp.inf, F32), jnp.zeros((1, 2 * tq), F32))

    kx, _ = load_k(kt_mix)
    pen = pen_ref[0, sel]
    raw = lax.dot_general(kx, q_ops(0), NT_DIMS, preferred_element_type=F32) - jnp.concatenate([pen, pen], axis=1)
    raw_ref[0] = raw
    cmax_ref[0] = jnp.max(raw, axis=0, keepdims=True)
    coff_ref[0] = jnp.zeros((1, 2 * tq), F32)

    def pair(s, st):
        scores(s + 1, 1)
        st = absorb(s, 0, st)
        scores(s + 2, 0)
        return absorb(s + 1, 1, st)

    n_quads = n_steps // 4
    state = lax.fori_loop(0, n_quads, lambda u, st: pair(4 * u + 2, pair(4 * u, st)), state)
    _, l_run = lax.fori_loop(2 * n_quads, n_steps // 2, lambda u, st: pair(2 * u, st), state)

    on = acc_ref[...] / l_run
    o = (on[:, :tq] - lam_ref[0] * on[:, tq:]).T
    o = o * lax.rsqrt(jnp.mean(o * o, axis=-1, keepdims=True) + RMS_EPS)
    o_ref[...] = (o * g_ref[0] * out_scale).astype(o_ref.dtype)


def _diff_tables(slopes, tq, tk):
    j = np.arange(tk)
    feat = slopes[:, None] * (j - tk // 2)[None, :]
    feat = np.broadcast_to(feat[:, :, None], (len(slopes), tk, LANES))
    i = np.arange(tq)
    pos = np.arange(tk // tq)[:, None, None] * tq + i[None, None, :]
    dist = np.abs(pos - j[None, :, None])
    pen = (slopes * LOG2E)[:, None, None, None] * dist[None]
    return jnp.asarray(feat, BF16), jnp.asarray(pen, F32)


def _diff_attention(qk, vt, lam, subln_g, lam_init, batch, seq, tq=512, tk=512):
    nq = seq // tq
    base = 2 * WIDTH_A // LANES
    hb = N_HEADS_B
    slopes = 2.0 ** (-8.0 * (np.arange(hb) + 1) / hb)
    feat, pen = _diff_tables(slopes, tq, tk)
    sl = jnp.asarray(slopes * LOG2E, F32)
    smem = pl.BlockSpec(memory_space=pltpu.SMEM)
    return pl.pallas_call(
        functools.partial(_diff_kernel, tq=tq, tk=tk, out_scale=1.0 - lam_init),
        grid=(batch, hb, nq),
        in_specs=[
            smem, smem,
            pl.BlockSpec((tq, LANES), lambda b, h, i: (b * nq + i, base + h)),
            pl.BlockSpec((seq, LANES), lambda b, h, i: (b, base + hb + h)),
            pl.BlockSpec((LANES, seq), lambda b, h, i: (WIDTH_A // LANES + h, b)),
            pl.BlockSpec((1, tk, LANES), lambda b, h, i: (h, 0, 0)),
            pl.BlockSpec((1, tk // tq, tk, tq), lambda b, h, i: (h, 0, 0, 0)),
            pl.BlockSpec((1, 1, LANES), lambda b, h, i: (h, 0, 0)),
        ],
        out_specs=pl.BlockSpec((tq, LANES), lambda b, h, i: (b * nq + i, h)),
        out_shape=jax.ShapeDtypeStruct((batch * seq, WIDTH_B), BF16),
        scratch_shapes=[
            pltpu.VMEM((2, 2 * tq, 2 * LANES), BF16),
            pltpu.VMEM((2, tk, 2 * tq), F32),
            pltpu.VMEM((2, 1, 2 * tq), F32),
            pltpu.VMEM((2, 1, 2 * tq), F32),
            pltpu.VMEM((LANES, 2 * tq), F32),
            pltpu.VMEM((1, 1), F32),
        ],
        compiler_params=_params("arbitrary", "arbitrary", "arbitrary"),
        name="diff_attn",
    )(sl, lam, qk, qk, vt, feat, pen, subln_g.astype(F32).reshape(hb, 1, LANES))


def _attn_out_kernel(oa_ref, ob_ref, wa_ref, wb_ref, x_ref, g_ref, b_ref, o_ref):
    h = jnp.dot(oa_ref[...], wa_ref[...], preferred_element_type=F32)
    h = h + jnp.dot(ob_ref[...], wb_ref[...], preferred_element_type=F32)
    o_ref[...] = _layer_norm(ALPHA * x_ref[...] + h, g_ref[...], b_ref[...])


def _attn_out(oa, ob, w_out_bf16, x2, g, b, tm=512):
    t, d = x2.shape
    wa, wb = w_out_bf16[:WIDTH_A], w_out_bf16[WIDTH_A:]
    row = lambda w: pl.BlockSpec((tm, w), lambda i: (i, 0))
    full = lambda a: pl.BlockSpec(a.shape, lambda i: (0,) * a.ndim)
    return pl.pallas_call(
        _attn_out_kernel,
        grid=(t // tm,),
        in_specs=[row(WIDTH_A), row(WIDTH_B), full(wa), full(wb), row(d), full(g), full(b)],
        out_specs=row(d),
        out_shape=jax.ShapeDtypeStruct((t, d), F32),
        compiler_params=_params("parallel"),
        name="attn_out_ln",
    )(oa, ob, wa, wb, x2, g, b)


def _gmlp_kernel(x_ref, win_ref, lg_ref, lb_ref, ws_ref, bs_ref, wout_ref, g_ref, b_ref, o_ref, *, tm):
    x = x_ref[...]
    width = wout_ref.shape[0]
    gw = width // GMLP_GROUPS
    z = jnp.dot(x.astype(BF16), win_ref[...], preferred_element_type=F32)
    z = 0.5 * z * (1.0 + lax.erf(z * (2.0 ** -0.5)))
    u = z[:, :width]
    v = _layer_norm(z[:, width:], lg_ref[...], lb_ref[...]).astype(BF16)
    chunks = []
    for n in range(tm // CHUNK):
        groups = []
        for gi in range(GMLP_GROUPS):
            vg = v[n * CHUNK:(n + 1) * CHUNK, gi * gw:(gi + 1) * gw]
            groups.append(jnp.dot(ws_ref[gi], vg, preferred_element_type=F32))
        chunks.append(jnp.concatenate(groups, axis=1) + bs_ref[...])
    sv = jnp.concatenate(chunks, axis=0)
    h = jnp.dot((u * sv).astype(BF16), wout_ref[...], preferred_element_type=F32)
    o_ref[...] = _layer_norm(ALPHA * x + h, g_ref[...], b_ref[...])


def _gmlp_layer(x2, w_in, ln_g, ln_b, w_s, bs_full, w_out, g, b, tm=256):
    t, d = x2.shape
    row = pl.BlockSpec((tm, d), lambda i: (i, 0))
    full = lambda a: pl.BlockSpec(a.shape, lambda i: (0,) * a.ndim)
    args = (w_in, ln_g, ln_b, w_s, bs_full, w_out, g, b)
    return pl.pallas_call(
        functools.partial(_gmlp_kernel, tm=tm),
        grid=(t // tm,),
        in_specs=[row] + [full(a) for a in args],
        out_specs=row,
        out_shape=jax.ShapeDtypeStruct((t, d), F32),
        compiler_params=_params("parallel"),
        name="gmlp_layer",
    )(x2, *args)


def _router_kernel(x_ref, wr_ref, o_ref):
    logits = lax.dot_general(wr_ref[...], x_ref[...], NT_DIMS, preferred_element_type=F32,
                             precision=lax.Precision.HIGHEST)
    m = jnp.max(logits, axis=0, keepdims=True)
    p = jnp.exp(logits - m)
    o_ref[0] = p / jnp.sum(p, axis=0, keepdims=True)


def _router(x2, w_router_t, batch, seq, tm=1024):
    d = x2.shape[1]
    ns = seq // tm
    return pl.pallas_call(
        _router_kernel,
        grid=(batch, ns),
        in_specs=[
            pl.BlockSpec((tm, d), lambda b, i: (b * ns + i, 0)),
            pl.BlockSpec((N_EXPERTS, d), lambda b, i: (0, 0)),
        ],
        out_specs=pl.BlockSpec((1, N_EXPERTS, tm), lambda b, i: (b, 0, i)),
        out_shape=jax.ShapeDtypeStruct((batch, N_EXPERTS, seq), F32),
        compiler_params=_params("parallel", "parallel"),
        name="router",
    )(x2, w_router_t)


def _route_kernel(aff_ref, idx_ref, gate_ref, thr_ref, *, cap):
    ne, nt = aff_ref.shape[1], aff_ref.shape[2]
    a_all = aff_ref[0]
    u_all = pltpu.bitcast(a_all, jnp.int32)

    def add_bit(i, thr):
        cand = thr | jnp.left_shift(1, 30 - i)
        count = jnp.sum(jnp.where(u_all >= cand, 1.0, 0.0), axis=(1, 2), keepdims=True)
        return jnp.where(count >= cap, cand, thr)

    thr_ref[...] = lax.fori_loop(0, 31, add_bit, jnp.zeros((ne, 1, 1), jnp.int32))

    lane = lax.broadcasted_iota(jnp.int32, (LANES, LANES), 1)
    sub = lax.broadcasted_iota(jnp.int32, (LANES, LANES), 0)
    tri_incl = jnp.where(sub <= lane, 1.0, 0.0).astype(BF16)
    r64 = lax.broadcasted_iota(jnp.int32, (nt, nt), 0)
    c64 = lax.broadcasted_iota(jnp.int32, (nt, nt), 1)
    low_strict = jnp.where(c64 < r64, 1.0, 0.0).astype(BF16)
    up_incl = jnp.where(r64 <= c64, 1.0, 0.0).astype(BF16)
    ones_rows = jnp.ones((8, LANES), BF16)
    slot = lax.broadcasted_iota(jnp.int32, (cap, 1), 0).astype(F32)
    lane_nt = lax.broadcasted_iota(jnp.int32, (cap, nt), 1)
    lane_f = lax.broadcasted_iota(jnp.int32, (cap, LANES), 1).astype(F32)

    def cumsum_tokens(mask_f32):
        mb = mask_f32.astype(BF16)
        local = jnp.dot(mb, tri_incl, preferred_element_type=F32)
        n_b = jnp.broadcast_to(jnp.sum(mask_f32, axis=1, keepdims=True), mask_f32.shape)
        before = jnp.dot(low_strict, n_b.astype(BF16), preferred_element_type=F32)
        return before + local, local

    def one_expert(e, carry):
        a = aff_ref[0, e]
        u = pltpu.bitcast(a, jnp.int32)
        t = thr_ref[e]
        gt = u > t
        eq = u == t
        need = cap - jnp.sum(jnp.where(gt, 1.0, 0.0), axis=(0, 1), keepdims=True)
        eq_f = jnp.where(eq, 1.0, 0.0)
        eq_incl, _ = cumsum_tokens(eq_f)
        sel = jnp.logical_or(gt, jnp.logical_and(eq, eq_incl - eq_f < need))
        sel_f = jnp.where(sel, 1.0, 0.0)
        _, local = cumsum_tokens(sel_f)
        n_row = lax.dot_general(ones_rows, sel_f.astype(BF16), NT_DIMS, preferred_element_type=F32)[:1]
        incl_row = jnp.dot(jnp.broadcast_to(n_row, (8, nt)).astype(BF16), up_incl, preferred_element_type=F32)[:1]
        excl_row = incl_row - n_row
        krow = jnp.sum(jnp.where(incl_row <= slot, 1.0, 0.0), axis=1, keepdims=True)
        onehot = jnp.where(lane_nt == krow.astype(jnp.int32), 1.0, 0.0)
        rank = slot - jnp.sum(onehot * excl_row, axis=1, keepdims=True)
        row_counts = jnp.dot(onehot.astype(BF16), local.astype(BF16), preferred_element_type=F32)
        pos = jnp.sum(jnp.where(row_counts <= rank, 1.0, 0.0), axis=1, keepdims=True)
        idx_ref[0, e] = (krow * LANES + pos).astype(jnp.int32)
        row_aff = jnp.dot(onehot, a, preferred_element_type=F32, precision=lax.Precision.HIGHEST)
        gate_ref[0, e] = jnp.sum(jnp.where(lane_f == pos, row_aff, 0.0), axis=1, keepdims=True)
        return carry

    lax.fori_loop(0, ne, one_expert, 0)


def _route(aff_t, cap):
    batch, ne, seq = aff_t.shape
    nt = seq // LANES
    out = jax.ShapeDtypeStruct((batch, ne, cap, 1), jnp.int32), jax.ShapeDtypeStruct((batch, ne, cap, 1), F32)
    spec = pl.BlockSpec((1, ne, cap, 1), lambda b: (b, 0, 0, 0))
    return pl.pallas_call(
        functools.partial(_route_kernel, cap=cap),
        grid=(batch,),
        in_specs=[pl.BlockSpec((1, ne, nt, LANES), lambda b: (b, 0, 0, 0))],
        out_specs=(spec, spec),
        out_shape=out,
        scratch_shapes=[pltpu.VMEM((ne, 1, 1), jnp.int32)],
        compiler_params=_params("parallel"),
        name="route",
    )(aff_t.reshape(batch, ne, nt, LANES))


def _expert_kernel(x_ref, wg_ref, wu_ref, wd_ref, gate_ref, o_ref):
    f = pl.program_id(2)
    x = x_ref[0, 0]
    a = jnp.dot(x, wg_ref[0, 0].astype(BF16), preferred_element_type=F32)
    u = jnp.dot(x, wu_ref[0, 0].astype(BF16), preferred_element_type=F32)
    hmid = (a * jax.nn.sigmoid(a) * u).astype(BF16)
    y = jnp.dot(hmid, wd_ref[0, 0].astype(BF16), preferred_element_type=F32)
    cap = y.shape[0]
    last = pl.num_programs(2) - 1

    def put(update):
        for s in range(y.shape[1] // LANES):
            dst = (0, 0, slice(None), slice(s * SUBLANES, (s + 1) * SUBLANES), slice(None))
            o_ref[dst] = update(o_ref[dst], y[:, s * LANES:(s + 1) * LANES].reshape(cap // SUBLANES, SUBLANES, LANES))

    @pl.when(f == 0)
    def _():
        put(lambda old, new: new)

    @pl.when(jnp.logical_and(f > 0, f < last))
    def _():
        put(lambda old, new: old + new)

    @pl.when(jnp.logical_and(f > 0, f == last))
    def _():
        gate = gate_ref[0, 0].reshape(cap // SUBLANES, SUBLANES, 1)
        put(lambda old, new: (old + new) * gate)


def _experts(xg, weights, gate, tf=1024):
    wg, wu, wd, layer = weights
    batch, ne, cap, d = xg.shape
    dff = wg.shape[3]
    assert dff // tf >= 2
    out_tail = (cap // SUBLANES, SUBLANES * (d // LANES), LANES)
    return pl.pallas_call(
        _expert_kernel,
        grid=(ne, batch, dff // tf),
        in_specs=[
            pl.BlockSpec((1, 1, cap, d), lambda e, b, f: (b, e, 0, 0)),
            pl.BlockSpec((1, 1, d, tf), lambda e, b, f: (layer, e, 0, f)),
            pl.BlockSpec((1, 1, d, tf), lambda e, b, f: (layer, e, 0, f)),
            pl.BlockSpec((1, 1, tf, d), lambda e, b, f: (layer, e, f, 0)),
            pl.BlockSpec((1, 1, cap, 1), lambda e, b, f: (b, e, 0, 0)),
        ],
        out_specs=pl.BlockSpec((1, 1) + out_tail, lambda e, b, f: (b, e, 0, 0, 0)),
        out_shape=jax.ShapeDtypeStruct((batch, ne) + out_tail, F32),
        compiler_params=_params("parallel", "parallel", "arbitrary"),
        name="experts",
    )(xg, wg, wu, wd, gate)


def _gather_kernel(cur_ref, next_ref, x_hbm, o_ref, buf, sem, *, cap):
    g = pl.program_id(0)
    last = pl.num_programs(0) - 1
    slot = lax.rem(g, 2)

    def issue_all(idx_ref, buf_slot):
        def issue(r, carry):
            pltpu.make_async_copy(x_hbm.at[pl.ds(idx_ref[0, 0, r], 1), :], buf.at[buf_slot, pl.ds(r, 1), :],
                                  sem.at[buf_slot]).start()
            return carry

        lax.fori_loop(0, cap, issue, 0, unroll=8)

    @pl.when(g == 0)
    def _():
        issue_all(cur_ref, 0)

    @pl.when(g < last)
    def _():
        issue_all(next_ref, 1 - slot)

    pltpu.make_async_copy(x_hbm.at[pl.ds(0, cap), :], buf.at[slot], sem.at[slot]).wait()
    o_ref[...] = buf[slot].astype(o_ref.dtype)


def _gather_rows(x2, rows, cap):
    g = rows.shape[0]
    d = x2.shape[1]
    smem = lambda fn: pl.BlockSpec((1, 1, cap), fn, memory_space=pltpu.SMEM)
    return pl.pallas_call(
        functools.partial(_gather_kernel, cap=cap),
        grid=(g,),
        in_specs=[smem(lambda i: (i, 0, 0)), smem(lambda i: (jnp.minimum(i + 1, g - 1), 0, 0)),
                  pl.BlockSpec(memory_space=pl.ANY)],
        out_specs=pl.BlockSpec((cap, d), lambda i: (i, 0)),
        out_shape=jax.ShapeDtypeStruct((g * cap, d), BF16),
        scratch_shapes=[pltpu.VMEM((2, cap, d), F32), pltpu.SemaphoreType.DMA((2,))],
        compiler_params=_params("arbitrary"),
        name="gather_rows",
    )(rows, rows, x2)


COMBINE_LN_ROWS = 512


def _combine_kernel(idx_ref, y_ref, x_hbm, g_ref, b_ref, o_hbm, acc_ref, xbuf, obuf, sem_x, sem_o, *, cap, seq):
    b = pl.program_id(0)
    e = pl.program_id(1)

    @pl.when(e == 0)
    def _():
        acc_ref[...] = jnp.zeros_like(acc_ref)

    def step(g, carry):
        r0 = pl.multiple_of(g * SUBLANES, SUBLANES)
        toks = [pl.multiple_of(idx_ref[0, 0, r0 + j] * SUBLANES, SUBLANES) for j in range(SUBLANES)]
        tiles = [acc_ref[pl.ds(toks[j], SUBLANES), :] + y_ref[0, 0, g, pl.ds(j, SUBLANES, stride=SUBLANES), :]
                 for j in range(SUBLANES)]
        for j in range(SUBLANES):
            acc_ref[pl.ds(toks[j], SUBLANES), :] = tiles[j]
        return carry

    lax.fori_loop(0, cap // SUBLANES, step, 0)

    @pl.when(e == pl.num_programs(1) - 1)
    def _():
        tl = COMBINE_LN_ROWS
        n_tiles = seq // tl
        row0 = b * seq

        def x_copy(i, slot):
            return pltpu.make_async_copy(x_hbm.at[pl.ds(pl.multiple_of(row0 + i * tl, tl), tl), :], xbuf.at[slot],
                                         sem_x.at[slot])

        def o_copy(i, slot):
            return pltpu.make_async_copy(obuf.at[slot], o_hbm.at[pl.ds(pl.multiple_of(row0 + i * tl, tl), tl), :],
                                         sem_o.at[slot])

        x_copy(0, 0).start()

        def tile(i, carry):
            slot = lax.rem(i, 2)

            @pl.when(i + 1 < n_tiles)
            def _():
                x_copy(i + 1, 1 - slot).start()

            x_copy(i, slot).wait()

            @pl.when(i >= 2)
            def _():
                o_copy(i - 2, slot).wait()

            base = pl.multiple_of(i * tl * SUBLANES, tl * SUBLANES)
            h = jnp.concatenate([acc_ref[pl.ds(base + s, tl, stride=SUBLANES), :] for s in range(SUBLANES)], axis=1)
            obuf[slot] = _layer_norm(ALPHA * xbuf[slot] + h, g_ref[...], b_ref[...])
            o_copy(i, slot).start()
            return carry

        lax.fori_loop(0, n_tiles, tile, 0)
        o_copy(n_tiles - 2, (n_tiles - 2) % 2).wait()
        o_copy(n_tiles - 1, (n_tiles - 1) % 2).wait()


def _combine_ln(y, idx, x2, g, b, seq):
    batch, ne, ng = y.shape[:3]
    cap = ng * SUBLANES
    d = x2.shape[1]
    assert d == SUBLANES * LANES and seq // COMBINE_LN_ROWS >= 2
    vec = pl.BlockSpec((1, d), lambda bb, ee: (0, 0))
    return pl.pallas_call(
        functools.partial(_combine_kernel, cap=cap, seq=seq),
        grid=(batch, ne),
        in_specs=[
            pl.BlockSpec((1, 1, cap), lambda bb, ee: (bb * ne + ee, 0, 0), memory_space=pltpu.SMEM),
            pl.BlockSpec((1, 1) + y.shape[2:], lambda bb, ee: (bb, ee, 0, 0, 0)),
            pl.BlockSpec(memory_space=pl.ANY), vec, vec,
        ],
        out_specs=pl.BlockSpec(memory_space=pl.ANY),
        out_shape=jax.ShapeDtypeStruct((batch * seq, d), F32),
        scratch_shapes=[pltpu.VMEM((seq * SUBLANES, LANES), F32), pltpu.VMEM((2, COMBINE_LN_ROWS, d), F32),
                        pltpu.VMEM((2, COMBINE_LN_ROWS, d), F32),
                        pltpu.SemaphoreType.DMA((2,)), pltpu.SemaphoreType.DMA((2,))],
        compiler_params=_params("arbitrary", "arbitrary"),
        name="combine_ln",
    )(idx.reshape(batch * ne, 1, cap), y, x2, g, b)


def _moe_layer(x2, w_router, weights, g, b, batch, seq):
    d = x2.shape[1]
    cap = EC_FACTOR * seq // N_EXPERTS
    aff_t = _router(x2, w_router.astype(F32).T, batch, seq)
    idx, gate = _route(aff_t, cap)
    idx = idx.reshape(batch, N_EXPERTS, cap)
    rows = idx + (jnp.arange(batch, dtype=idx.dtype) * seq)[:, None, None]
    xg = _gather_rows(x2, rows.reshape(batch * N_EXPERTS, 1, cap), cap).reshape(batch, N_EXPERTS, cap, d)
    y = _experts(xg, weights, gate)
    return _combine_ln(y, idx, x2, g, b, seq)


def _lambda_init(layer_number):
    return 0.8 - 0.6 * float(np.exp(-0.3 * (layer_number - 1)))


def _row(v):
    return v.astype(F32).reshape(1, -1)


def kernel(x, w_in_ab, rpb_a, lambda_qk, subln_g, w_out_ab, w_in_c, ln_v_g, ln_v_b, w_s, b_s,
           w_out_c, ln_mix_g, ln_mix_b, w_router, w_gate, w_up, w_down, ln_ffn_g, ln_ffn_b):
    batch, seq, d = x.shape
    x2 = x.astype(F32).reshape(batch * seq, d)
    v_a = slice(2 * WIDTH_A, 3 * WIDTH_A)
    v_b = slice(3 * WIDTH_A + 2 * WIDTH_B, 3 * WIDTH_A + 3 * WIDTH_B)
    col_scale = np.ones((1, 2 * WIDTH_A + 2 * WIDTH_B), np.float32)
    col_scale[:, :WIDTH_A] = QK_SCALE * LOG2E
    col_scale[:, 2 * WIDTH_A:2 * WIDTH_A + WIDTH_B] = QK_SCALE * LOG2E
    col_scale = jnp.asarray(col_scale)

    for l in range(DEPTH):
        i = l // 2
        if l % 2 == 0:
            lam_init = _lambda_init(l + 1)
            lf = lambda_qk[i].astype(F32)
            lam = jnp.exp(jnp.sum(lf[0] * lf[1])) - jnp.exp(jnp.sum(lf[2] * lf[3])) + lam_init
            w_in = w_in_ab[i].astype(BF16)
            qk = _project(x2, jnp.concatenate([w_in[:, :v_a.start], w_in[:, v_a.stop:v_b.start]], axis=1), col_scale)
            vt = _project_t(x2, jnp.concatenate([w_in[:, v_a], w_in[:, v_b]], axis=1).T)
            oa = _natten(qk, vt, _natten_bias(rpb_a[i], seq // GRID_W), batch, seq)
            ob = _diff_attention(qk, vt, lam.reshape(1), subln_g[i], lam_init, batch, seq)
            x2 = _attn_out(oa, ob, w_out_ab[i].astype(BF16), x2, _row(ln_mix_g[l]), _row(ln_mix_b[l]))
        else:
            gw = w_in_c.shape[2] // 2 // GMLP_GROUPS
            bs_full = jnp.repeat(b_s[i].astype(F32).T, gw, axis=1)
            x2 = _gmlp_layer(x2, w_in_c[i].astype(BF16), _row(ln_v_g[i]), _row(ln_v_b[i]),
                             w_s[i].astype(BF16), bs_full, w_out_c[i].astype(BF16),
                             _row(ln_mix_g[l]), _row(ln_mix_b[l]))
        x2 = _moe_layer(x2, w_router[l], (w_gate, w_up, w_down, l),
                        _row(ln_ffn_g[l]), _row(ln_ffn_b[l]), batch, seq)
    return x2.reshape(batch, seq, d).astype(x.dtype)
```

```python
import functools

import jax
import jax.numpy as jnp
import numpy as np
from jax import lax
from jax.experimental import pallas as pl
from jax.experimental.pallas import tpu as pltpu

F32 = jnp.float32
BF16 = jnp.bfloat16

DEPTH = 4
GRID_W = 64
HEAD_DIM = 64
N_HEADS_A = 8
WIN_ROWS = 8
WIN_COLS = 16
WIDTH_A = N_HEADS_A * HEAD_DIM
N_HEADS_B = 4
WIDTH_B = N_HEADS_B * 2 * HEAD_DIM
CHUNK = 128
GMLP_GROUPS = 8
N_EXPERTS = 16
EC_FACTOR = 2
LN_EPS = 1e-5
RMS_EPS = 1e-6
ALPHA = (2.0 * DEPTH) ** 0.25
QK_SCALE = HEAD_DIM ** -0.5
MASK_VALUE = -1e30
LOG2E = float(np.log2(np.e))
SKIP_LOG2 = 48.0
SKIP_MARGIN = 1.02

LANES = 128
SUBLANES = 8
VMEM_LIMIT = 56 * 1024 * 1024

NT_DIMS = (((1,), (1,)), ((), ()))


def _params(*sem):
    return pltpu.CompilerParams(dimension_semantics=sem, vmem_limit_bytes=VMEM_LIMIT)


def _layer_norm(y, g, b):
    mu = jnp.mean(y, axis=-1, keepdims=True)
    yc = y - mu
    var = jnp.mean(yc * yc, axis=-1, keepdims=True)
    return yc * lax.rsqrt(var + LN_EPS) * g + b


def _bf16_split(value, parts=3):
    out, rest = [], np.float64(value)
    for _ in range(parts):
        piece = float(np.asarray(rest, np.float32).astype(jnp.bfloat16).astype(np.float64))
        out.append(piece)
        rest = rest - piece
    return out


def _proj_kernel(x_ref, w_ref, s_ref, o_ref):
    acc = jnp.dot(x_ref[...].astype(BF16), w_ref[...], preferred_element_type=F32)
    o_ref[...] = (acc * s_ref[...]).astype(o_ref.dtype)


def _project(x2, w_bf16, col_scale, tm=1024, tn=512):
    t, k = x2.shape
    n = w_bf16.shape[1]
    return pl.pallas_call(
        _proj_kernel,
        grid=(t // tm, n // tn),
        in_specs=[
            pl.BlockSpec((tm, k), lambda i, j: (i, 0)),
            pl.BlockSpec((k, tn), lambda i, j: (0, j)),
            pl.BlockSpec((1, tn), lambda i, j: (0, j)),
        ],
        out_specs=pl.BlockSpec((tm, tn), lambda i, j: (i, j)),
        out_shape=jax.ShapeDtypeStruct((t, n), BF16),
        compiler_params=_params("parallel", "arbitrary"),
        name="qkv_proj",
    )(x2, w_bf16, col_scale)


def _proj_t_kernel(x_ref, wt_ref, o_ref):
    acc = lax.dot_general(wt_ref[...], x_ref[...].astype(BF16), NT_DIMS, preferred_element_type=F32)
    o_ref[...] = acc.astype(o_ref.dtype)


def _project_t(x2, wt_bf16, tm=1024):
    t, k = x2.shape
    n = wt_bf16.shape[0]
    return pl.pallas_call(
        _proj_t_kernel,
        grid=(t // tm,),
        in_specs=[
            pl.BlockSpec((tm, k), lambda i: (i, 0)),
            pl.BlockSpec((n, k), lambda i: (0, 0)),
        ],
        out_specs=pl.BlockSpec((n, tm), lambda i: (0, i)),
        out_shape=jax.ShapeDtypeStruct((n, t), BF16),
        compiler_params=_params("parallel"),
        name="v_proj_t",
    )(x2, wt_bf16)


NAT_ROWS = 4
NAT_KROWS = NAT_ROWS + WIN_ROWS


def _natten_bias(rpb, n_rows):
    n_heads = rpb.shape[0]
    span = 2 * WIN_COLS - 1
    pad_lo = GRID_W - WIN_COLS
    padded = jnp.pad(rpb.astype(F32) * LOG2E, ((0, 0), (0, 0), (pad_lo, 2 * GRID_W - 1 - pad_lo - span)))
    toep = jnp.stack([padded[:, :, GRID_W - 1 - c:2 * GRID_W - 1 - c] for c in range(GRID_W)], axis=3)
    c = np.arange(GRID_W)[None, :]
    kc = np.arange(GRID_W)[:, None]
    c_start = np.clip(c - WIN_COLS // 2, 0, GRID_W - WIN_COLS)
    valid = (kc >= c_start) & (kc < c_start + WIN_COLS)
    toep = jnp.where(valid[None, None], toep, MASK_VALUE)
    masked = jnp.full((n_heads, GRID_W, GRID_W), MASK_VALUE, F32)
    n_blk = n_rows // NAT_ROWS
    kinds = []
    for blk in (0, 1, n_blk - 1):
        r0 = blk * NAT_ROWS
        ks = int(np.clip(r0 - WIN_ROWS // 2, 0, n_rows - NAT_KROWS))
        cols = []
        for a in range(NAT_ROWS):
            r = r0 + a
            rs = int(np.clip(r - WIN_ROWS // 2, 0, n_rows - WIN_ROWS))
            rows = []
            for i in range(NAT_KROWS):
                kr = ks + i
                rows.append(toep[:, kr - r + WIN_ROWS - 1] if rs <= kr < rs + WIN_ROWS else masked)
            cols.append(jnp.concatenate(rows, axis=1))
        kinds.append(jnp.concatenate(cols, axis=2))
    b = jnp.stack(kinds, axis=0)
    nk, nq = b.shape[2], b.shape[3]
    b = b.reshape(3, n_heads // 2, 2, nk, nq).transpose(0, 1, 3, 2, 4)
    return b.reshape(3, n_heads // 2, nk, 2 * nq)


def _natten_kernel(q_ref, k_ref, vt_ref, bias_ref, o_ref, raw_ref, cmax_ref, *, n_rows):
    nq = NAT_ROWS * GRID_W
    nk = NAT_KROWS * GRID_W
    n_blk = n_rows // NAT_ROWS
    lane = lax.broadcasted_iota(jnp.int32, (nq, LANES), 1)
    low = lane < HEAD_DIM

    def key_start(blk):
        ks = jnp.clip(blk * NAT_ROWS - WIN_ROWS // 2, 0, n_rows - NAT_KROWS)
        return pl.multiple_of(ks * GRID_W, NAT_ROWS * GRID_W)

    def scores(blk, slot):
        q = q_ref[pl.ds(pl.multiple_of(blk * nq, nq), nq), :]
        zero = jnp.zeros_like(q)
        qx = jnp.concatenate([jnp.where(low, q, zero), jnp.where(low, zero, q)], axis=0)
        k = k_ref[pl.ds(key_start(blk), nk), :]
        kind = jnp.where(blk == 0, 0, jnp.where(blk == n_blk - 1, 2, 1))
        raw = lax.dot_general(k, qx, NT_DIMS, preferred_element_type=F32) + bias_ref[kind, 0]
        raw_ref[slot] = raw
        cmax_ref[slot] = jnp.max(raw, axis=0, keepdims=True)

    def absorb(blk, slot):
        vt = vt_ref[:, pl.ds(key_start(blk), nk)]
        p = jnp.exp2(raw_ref[slot] - cmax_ref[slot])
        l = jnp.sum(p, axis=0, keepdims=True)
        res = jnp.dot(vt, p.astype(BF16), preferred_element_type=F32) / l
        o_t = jnp.concatenate([res[:HEAD_DIM, :nq], res[HEAD_DIM:, nq:]], axis=0)
        o_ref[pl.ds(pl.multiple_of(blk * nq, nq), nq), :] = o_t.T.astype(o_ref.dtype)

    scores(0, 0)

    def body(u, carry):
        blk = 2 * u
        scores(blk + 1, 1)
        absorb(blk, 0)
        scores(blk + 2, 0)
        absorb(blk + 1, 1)
        return carry

    assert n_blk % 2 == 0
    lax.fori_loop(0, n_blk // 2 - 1, body, 0)
    scores(n_blk - 1, 1)
    absorb(n_blk - 2, 0)
    absorb(n_blk - 1, 1)


def _natten(qk, vt, bias, batch, seq):
    n_rows = seq // GRID_W
    hp = N_HEADS_A // 2
    nq = NAT_ROWS * GRID_W
    nk = NAT_KROWS * GRID_W
    return pl.pallas_call(
        functools.partial(_natten_kernel, n_rows=n_rows),
        grid=(batch, hp),
        in_specs=[
            pl.BlockSpec((seq, LANES), lambda b, h: (b, h)),
            pl.BlockSpec((seq, LANES), lambda b, h: (b, hp + h)),
            pl.BlockSpec((LANES, seq), lambda b, h: (h, b)),
            pl.BlockSpec((3, 1, nk, 2 * nq), lambda b, h: (0, h, 0, 0)),
        ],
        out_specs=pl.BlockSpec((seq, LANES), lambda b, h: (b, h)),
        out_shape=jax.ShapeDtypeStruct((batch * seq, WIDTH_A), BF16),
        scratch_shapes=[pltpu.VMEM((2, nk, 2 * nq), F32), pltpu.VMEM((2, 1, 2 * nq), F32)],
        compiler_params=_params("parallel", "parallel"),
        name="natten",
    )(qk, qk, vt, bias)


def _diff_kernel(sl_ref, lam_ref, q_ref, k_ref, vt_ref, f_ref, pen_ref, g_ref, o_ref,
                 qop_ref, raw_ref, cmax_ref, coff_ref, acc_ref, knorm_ref, *, tq, tk, out_scale):
    h = pl.program_id(1)
    qi = pl.program_id(2)
    n_kt = k_ref.shape[0] // tk
    assert n_kt % 2 == 0
    sl = sl_ref[h]
    a1, a2, a3 = _bf16_split(LOG2E)

    q = q_ref[...]
    lane_q = lax.broadcasted_iota(jnp.int32, q.shape, 1)
    low_q = lane_q < HEAD_DIM
    zero = jnp.zeros_like(q)
    q1 = jnp.where(low_q, q, zero)
    q2 = jnp.where(low_q, zero, q)

    def q_ops(sign):
        if sign == 0:
            f = zero
        else:
            f = jnp.where(lane_q == 0, sign * a1,
                          jnp.where(lane_q == 1, sign * a2, jnp.where(lane_q == 2, sign * a3, 0.0))).astype(BF16)
        return jnp.concatenate([jnp.concatenate([q1, f], axis=1), jnp.concatenate([q2, f], axis=1)], axis=0)

    qop_ref[0] = q_ops(1.0)
    qop_ref[1] = q_ops(-1.0)
    fk = f_ref[0]
    pos = lax.broadcasted_iota(jnp.int32, (1, 2 * tq), 1)
    iq = (qi * tq + jnp.where(pos < tq, pos, pos - tq)).astype(F32)

    kt_mix = (qi * tq) // tk
    sel = qi - kt_mix * (tk // tq)

    @pl.when(qi == 0)
    def _():
        def tile_norm(kt, best):
            kk = k_ref[pl.ds(pl.multiple_of(kt * tk, tk), tk), :].astype(F32)
            kk = kk * kk
            low_k = lax.broadcasted_iota(jnp.int32, kk.shape, 1) < HEAD_DIM
            n1 = jnp.sum(jnp.where(low_k, kk, 0.0), axis=1, keepdims=True)
            n2 = jnp.sum(jnp.where(low_k, 0.0, kk), axis=1, keepdims=True)
            return jnp.maximum(best, jnp.max(jnp.maximum(n1, n2), axis=0, keepdims=True))

        best = lax.fori_loop(0, n_kt, tile_norm, jnp.zeros((1, 1), F32))
        knorm_ref[...] = jnp.sqrt(best)

    qq = q.astype(F32)
    qq = qq * qq
    qn1 = jnp.sum(jnp.where(low_q, qq, 0.0), axis=1, keepdims=True)
    qn2 = jnp.sum(jnp.where(low_q, 0.0, qq), axis=1, keepdims=True)
    qnorm = jnp.sqrt(jnp.max(jnp.maximum(qn1, qn2), axis=0, keepdims=True))
    reach = (SKIP_LOG2 + 2.0 * SKIP_MARGIN * qnorm * knorm_ref[...]) / sl
    q_lo = (qi * tq).astype(F32)
    kt_lo = jnp.maximum(jnp.floor((q_lo - reach - 1.0) / tk), 0.0).astype(jnp.int32)[0, 0]
    kt_hi = jnp.minimum(jnp.floor((q_lo + (tq + reach)) / tk), n_kt - 1.0).astype(jnp.int32)[0, 0]
    odd = lax.rem(kt_hi - kt_lo + 1, 2)
    widen_lo = jnp.where(kt_lo > 0, odd, 0)
    kt_lo = kt_lo - widen_lo
    kt_hi = kt_hi + (odd - widen_lo)
    n_steps = kt_hi - kt_lo + 1

    def load_k(kt):
        off = pl.multiple_of(kt * tk, tk)
        return jnp.concatenate([k_ref[pl.ds(off, tk), :], fk], axis=1), vt_ref[:, pl.ds(off, tk)]

    def tile_of(s):
        t = kt_lo + jnp.minimum(s, n_steps - 1) - 1
        return jnp.where(s == 0, kt_mix, t + (t >= kt_mix).astype(jnp.int32))

    def scores(s, slot):
        kt = tile_of(s)
        kx, _ = load_k(kt)
        raw = lax.dot_general(kx, qop_ref[(kt > kt_mix).astype(jnp.int32)], NT_DIMS, preferred_element_type=F32)
        c = -sl * jnp.abs(iq - (kt * tk + tk // 2).astype(F32))
        raw_ref[slot] = raw
        cmax_ref[slot] = jnp.max(raw, axis=0, keepdims=True) + c
        coff_ref[slot] = c

    def absorb(s, slot, state):
        m_prev, l_prev = state
        _, vt = load_k(tile_of(s))
        m_new = jnp.maximum(m_prev, cmax_ref[slot])
        a = jnp.exp2(m_prev - m_new)
        p = jnp.exp2(raw_ref[slot] - (m_new - coff_ref[slot]))
        l_new = a * l_prev + jnp.sum(p, axis=0, keepdims=True)
        acc_ref[...] = a * acc_ref[...] + jnp.dot(vt, p.astype(BF16), preferred_element_type=F32)
        return m_new, l_new

    acc_ref[...] = jnp.zeros_like(acc_ref)
    state = (jnp.full((1, 2 * tq), -jnp.inf, F32), jnp.zeros((1, 2 * tq), F32))

    kx, _ = load_k(kt_mix)
    pen = pen_ref[0, sel]
    raw = lax.dot_general(kx, q_ops(0), NT_DIMS, preferred_element_type=F32) - jnp.concatenate([pen, pen], axis=1)
    raw_ref[0] = raw
    cmax_ref[0] = jnp.max(raw, axis=0, keepdims=True)
    coff_ref[0] = jnp.zeros((1, 2 * tq), F32)

    def pair(s, st):
        scores(s + 1, 1)
        st = absorb(s, 0, st)
        scores(s + 2, 0)
        return absorb(s + 1, 1, st)

    n_quads = n_steps // 4
    state = lax.fori_loop(0, n_quads, lambda u, st: pair(4 * u + 2, pair(4 * u, st)), state)
    _, l_run = lax.fori_loop(2 * n_quads, n_steps // 2, lambda u, st: pair(2 * u, st), state)

    on = acc_ref[...] / l_run
    o = (on[:, :tq] - lam_ref[0] * on[:, tq:]).T
    o = o * lax.rsqrt(jnp.mean(o * o, axis=-1, keepdims=True) + RMS_EPS)
    o_ref[...] = (o * g_ref[0] * out_scale).astype(o_ref.dtype)


def _diff_tables(slopes, tq, tk):
    j = np.arange(tk)
    feat = slopes[:, None] * (j - tk // 2)[None, :]
    feat = np.broadcast_to(feat[:, :, None], (len(slopes), tk, LANES))
    i = np.arange(tq)
    pos = np.arange(tk // tq)[:, None, None] * tq + i[None, None, :]
    dist = np.abs(pos - j[None, :, None])
    pen = (slopes * LOG2E)[:, None, None, None] * dist[None]
    return jnp.asarray(feat, BF16), jnp.asarray(pen, F32)


def _diff_attention(qk, vt, lam, subln_g, lam_init, batch, seq, tq=512, tk=512):
    nq = seq // tq
    base = 2 * WIDTH_A // LANES
    hb = N_HEADS_B
    slopes = 2.0 ** (-8.0 * (np.arange(hb) + 1) / hb)
    feat, pen = _diff_tables(slopes, tq, tk)
    sl = jnp.asarray(slopes * LOG2E, F32)
    smem = pl.BlockSpec(memory_space=pltpu.SMEM)
    return pl.pallas_call(
        functools.partial(_diff_kernel, tq=tq, tk=tk, out_scale=1.0 - lam_init),
        grid=(batch, hb, nq),
        in_specs=[
            smem, smem,
            pl.BlockSpec((tq, LANES), lambda b, h, i: (b * nq + i, base + h)),
            pl.BlockSpec((seq, LANES), lambda b, h, i: (b, base + hb + h)),
            pl.BlockSpec((LANES, seq), lambda b, h, i: (WIDTH_A // LANES + h, b)),
            pl.BlockSpec((1, tk, LANES), lambda b, h, i: (h, 0, 0)),
            pl.BlockSpec((1, tk // tq, tk, tq), lambda b, h, i: (h, 0, 0, 0)),
            pl.BlockSpec((1, 1, LANES), lambda b, h, i: (h, 0, 0)),
        ],
        out_specs=pl.BlockSpec((tq, LANES), lambda b, h, i: (b * nq + i, h)),
        out_shape=jax.ShapeDtypeStruct((batch * seq, WIDTH_B), BF16),
        scratch_shapes=[
            pltpu.VMEM((2, 2 * tq, 2 * LANES), BF16),
            pltpu.VMEM((2, tk, 2 * tq), F32),
            pltpu.VMEM((2, 1, 2 * tq), F32),
            pltpu.VMEM((2, 1, 2 * tq), F32),
            pltpu.VMEM((LANES, 2 * tq), F32),
            pltpu.VMEM((1, 1), F32),
        ],
        compiler_params=_params("arbitrary", "arbitrary", "arbitrary"),
        name="diff_attn",
    )(sl, lam, qk, qk, vt, feat, pen, subln_g.astype(F32).reshape(hb, 1, LANES))


def _attn_out_kernel(oa_ref, ob_ref, wa_ref, wb_ref, x_ref, g_ref, b_ref, o_ref):
    h = jnp.dot(oa_ref[...], wa_ref[...], preferred_element_type=F32)
    h = h + jnp.dot(ob_ref[...], wb_ref[...], preferred_element_type=F32)
    o_ref[...] = _layer_norm(ALPHA * x_ref[...] + h, g_ref[...], b_ref[...])


def _attn_out(oa, ob, w_out_bf16, x2, g, b, tm=512):
    t, d = x2.shape
    wa, wb = w_out_bf16[:WIDTH_A], w_out_bf16[WIDTH_A:]
    row = lambda w: pl.BlockSpec((tm, w), lambda i: (i, 0))
    full = lambda a: pl.BlockSpec(a.shape, lambda i: (0,) * a.ndim)
    return pl.pallas_call(
        _attn_out_kernel,
        grid=(t // tm,),
        in_specs=[row(WIDTH_A), row(WIDTH_B), full(wa), full(wb), row(d), full(g), full(b)],
        out_specs=row(d),
        out_shape=jax.ShapeDtypeStruct((t, d), F32),
        compiler_params=_params("parallel"),
        name="attn_out_ln",
    )(oa, ob, wa, wb, x2, g, b)


def _gmlp_kernel(x_ref, win_ref, lg_ref, lb_ref, ws_ref, bs_ref, wout_ref, g_ref, b_ref, o_ref, *, tm):
    x = x_ref[...]
    width = wout_ref.shape[0]
    gw = width // GMLP_GROUPS
    z = jnp.dot(x.astype(BF16), win_ref[...], preferred_element_type=F32)
    z = 0.5 * z * (1.0 + lax.erf(z * (2.0 ** -0.5)))
    u = z[:, :width]
    v = _layer_norm(z[:, width:], lg_ref[...], lb_ref[...]).astype(BF16)
    chunks = []
    for n in range(tm // CHUNK):
        groups = []
        for gi in range(GMLP_GROUPS):
            vg = v[n * CHUNK:(n + 1) * CHUNK, gi * gw:(gi + 1) * gw]
            groups.append(jnp.dot(ws_ref[gi], vg, preferred_element_type=F32))
        chunks.append(jnp.concatenate(groups, axis=1) + bs_ref[...])
    sv = jnp.concatenate(chunks, axis=0)
    h = jnp.dot((u * sv).astype(BF16), wout_ref[...], preferred_element_type=F32)
    o_ref[...] = _layer_norm(ALPHA * x + h, g_ref[...], b_ref[...])


def _gmlp_layer(x2, w_in, ln_g, ln_b, w_s, bs_full, w_out, g, b, tm=256):
    t, d = x2.shape
    row = pl.BlockSpec((tm, d), lambda i: (i, 0))
    full = lambda a: pl.BlockSpec(a.shape, lambda i: (0,) * a.ndim)
    args = (w_in, ln_g, ln_b, w_s, bs_full, w_out, g, b)
    return pl.pallas_call(
        functools.partial(_gmlp_kernel, tm=tm),
        grid=(t // tm,),
        in_specs=[row] + [full(a) for a in args],
        out_specs=row,
        out_shape=jax.ShapeDtypeStruct((t, d), F32),
        compiler_params=_params("parallel"),
        name="gmlp_layer",
    )(x2, *args)


def _router_kernel(x_ref, wr_ref, o_ref):
    logits = lax.dot_general(wr_ref[...], x_ref[...], NT_DIMS, preferred_element_type=F32,
                             precision=lax.Precision.HIGHEST)
    m = jnp.max(logits, axis=0, keepdims=True)
    p = jnp.exp(logits - m)
    o_ref[0] = p / jnp.sum(p, axis=0, keepdims=True)


def _router(x2, w_router_t, batch, seq, tm=1024):
    d = x2.shape[1]
    ns = seq // tm
    return pl.pallas_call(
        _router_kernel,
        grid=(batch, ns),
        in_specs=[
            pl.BlockSpec((tm, d), lambda b, i: (b * ns + i, 0)),
            pl.BlockSpec((N_EXPERTS, d), lambda b, i: (0, 0)),
        ],
        out_specs=pl.BlockSpec((1, N_EXPERTS, tm), lambda b, i: (b, 0, i)),
        out_shape=jax.ShapeDtypeStruct((batch, N_EXPERTS, seq), F32),
        compiler_params=_params("parallel", "parallel"),
        name="router",
    )(x2, w_router_t)


def _route_kernel(aff_ref, idx_ref, gate_ref, thr_ref, *, cap):
    ne, nt = aff_ref.shape[1], aff_ref.shape[2]
    a_all = aff_ref[0]
    u_all = pltpu.bitcast(a_all, jnp.int32)

    def add_bit(i, thr):
        cand = thr | jnp.left_shift(1, 30 - i)
        count = jnp.sum(jnp.where(u_all >= cand, 1.0, 0.0), axis=(1, 2), keepdims=True)
        return jnp.where(count >= cap, cand, thr)

    thr_ref[...] = lax.fori_loop(0, 31, add_bit, jnp.zeros((ne, 1, 1), jnp.int32))

    lane = lax.broadcasted_iota(jnp.int32, (LANES, LANES), 1)
    sub = lax.broadcasted_iota(jnp.int32, (LANES, LANES), 0)
    tri_incl = jnp.where(sub <= lane, 1.0, 0.0).astype(BF16)
    r64 = lax.broadcasted_iota(jnp.int32, (nt, nt), 0)
    c64 = lax.broadcasted_iota(jnp.int32, (nt, nt), 1)
    low_strict = jnp.where(c64 < r64, 1.0, 0.0).astype(BF16)
    up_incl = jnp.where(r64 <= c64, 1.0, 0.0).astype(BF16)
    ones_rows = jnp.ones((8, LANES), BF16)
    slot = lax.broadcasted_iota(jnp.int32, (cap, 1), 0).astype(F32)
    lane_nt = lax.broadcasted_iota(jnp.int32, (cap, nt), 1)
    lane_f = lax.broadcasted_iota(jnp.int32, (cap, LANES), 1).astype(F32)

    def cumsum_tokens(mask_f32):
        mb = mask_f32.astype(BF16)
        local = jnp.dot(mb, tri_incl, preferred_element_type=F32)
        n_b = jnp.broadcast_to(jnp.sum(mask_f32, axis=1, keepdims=True), mask_f32.shape)
        before = jnp.dot(low_strict, n_b.astype(BF16), preferred_element_type=F32)
        return before + local, local

    def one_expert(e, carry):
        a = aff_ref[0, e]
        u = pltpu.bitcast(a, jnp.int32)
        t = thr_ref[e]
        gt = u > t
        eq = u == t
        need = cap - jnp.sum(jnp.where(gt, 1.0, 0.0), axis=(0, 1), keepdims=True)
        eq_f = jnp.where(eq, 1.0, 0.0)
        eq_incl, _ = cumsum_tokens(eq_f)
        sel = jnp.logical_or(gt, jnp.logical_and(eq, eq_incl - eq_f < need))
        sel_f = jnp.where(sel, 1.0, 0.0)
        _, local = cumsum_tokens(sel_f)
        n_row = lax.dot_general(ones_rows, sel_f.astype(BF16), NT_DIMS, preferred_element_type=F32)[:1]
        incl_row = jnp.dot(jnp.broadcast_to(n_row, (8, nt)).astype(BF16), up_incl, preferred_element_type=F32)[:1]
        excl_row = incl_row - n_row
        krow = jnp.sum(jnp.where(incl_row <= slot, 1.0, 0.0), axis=1, keepdims=True)
        onehot = jnp.where(lane_nt == krow.astype(jnp.int32), 1.0, 0.0)
        rank = slot - jnp.sum(onehot * excl_row, axis=1, keepdims=True)
        row_counts = jnp.dot(onehot.astype(BF16), local.astype(BF16), preferred_element_type=F32)
        pos = jnp.sum(jnp.where(row_counts <= rank, 1.0, 0.0), axis=1, keepdims=True)
        idx_ref[0, e] = (krow * LANES + pos).astype(jnp.int32)
        row_aff = jnp.dot(onehot, a, preferred_element_type=F32, precision=lax.Precision.HIGHEST)
        gate_ref[0, e] = jnp.sum(jnp.where(lane_f == pos, row_aff, 0.0), axis=1, keepdims=True)
        return carry

    lax.fori_loop(0, ne, one_expert, 0)


def _route(aff_t, cap):
    batch, ne, seq = aff_t.shape
    nt = seq // LANES
    out = jax.ShapeDtypeStruct((batch, ne, cap, 1), jnp.int32), jax.ShapeDtypeStruct((batch, ne, cap, 1), F32)
    spec = pl.BlockSpec((1, ne, cap, 1), lambda b: (b, 0, 0, 0))
    return pl.pallas_call(
        functools.partial(_route_kernel, cap=cap),
        grid=(batch,),
        in_specs=[pl.BlockSpec((1, ne, nt, LANES), lambda b: (b, 0, 0, 0))],
        out_specs=(spec, spec),
        out_shape=out,
        scratch_shapes=[pltpu.VMEM((ne, 1, 1), jnp.int32)],
        compiler_params=_params("parallel"),
        name="route",
    )(aff_t.reshape(batch, ne, nt, LANES))


def _expert_kernel(x_ref, wg_ref, wu_ref, wd_ref, gate_ref, o_ref):
    f = pl.program_id(2)
    x = x_ref[0, 0]
    a = jnp.dot(x, wg_ref[0, 0].astype(BF16), preferred_element_type=F32)
    u = jnp.dot(x, wu_ref[0, 0].astype(BF16), preferred_element_type=F32)
    hmid = (a * jax.nn.sigmoid(a) * u).astype(BF16)
    y = jnp.dot(hmid, wd_ref[0, 0].astype(BF16), preferred_element_type=F32)
    cap = y.shape[0]
    last = pl.num_programs(2) - 1

    def put(update):
        for s in range(y.shape[1] // LANES):
            dst = (0, 0, slice(None), slice(s * SUBLANES, (s + 1) * SUBLANES), slice(None))
            o_ref[dst] = update(o_ref[dst], y[:, s * LANES:(s + 1) * LANES].reshape(cap // SUBLANES, SUBLANES, LANES))

    @pl.when(f == 0)
    def _():
        put(lambda old, new: new)

    @pl.when(jnp.logical_and(f > 0, f < last))
    def _():
        put(lambda old, new: old + new)

    @pl.when(jnp.logical_and(f > 0, f == last))
    def _():
        gate = gate_ref[0, 0].reshape(cap // SUBLANES, SUBLANES, 1)
        put(lambda old, new: (old + new) * gate)


def _experts(xg, weights, gate, tf=1024):
    wg, wu, wd, layer = weights
    batch, ne, cap, d = xg.shape
    dff = wg.shape[3]
    assert dff // tf >= 2
    out_tail = (cap // SUBLANES, SUBLANES * (d // LANES), LANES)
    return pl.pallas_call(
        _expert_kernel,
        grid=(ne, batch, dff // tf),
        in_specs=[
            pl.BlockSpec((1, 1, cap, d), lambda e, b, f: (b, e, 0, 0)),
            pl.BlockSpec((1, 1, d, tf), lambda e, b, f: (layer, e, 0, f)),
            pl.BlockSpec((1, 1, d, tf), lambda e, b, f: (layer, e, 0, f)),
            pl.BlockSpec((1, 1, tf, d), lambda e, b, f: (layer, e, f, 0)),
            pl.BlockSpec((1, 1, cap, 1), lambda e, b, f: (b, e, 0, 0)),
        ],
        out_specs=pl.BlockSpec((1, 1) + out_tail, lambda e, b, f: (b, e, 0, 0, 0)),
        out_shape=jax.ShapeDtypeStruct((batch, ne) + out_tail, F32),
        compiler_params=_params("parallel", "parallel", "arbitrary"),
        name="experts",
    )(xg, wg, wu, wd, gate)


def _token_major_kernel(x_ref, o_ref):
    x = x_ref[...]
    for s in range(x.shape[1] // LANES):
        o_ref[:, s, :] = x[:, s * LANES:(s + 1) * LANES]


def _token_major(x2, tm=512):
    t, d = x2.shape
    assert d == SUBLANES * LANES
    out = pl.pallas_call(
        _token_major_kernel,
        grid=(t // tm,),
        in_specs=[pl.BlockSpec((tm, d), lambda i: (i, 0))],
        out_specs=pl.BlockSpec((tm, SUBLANES, LANES), lambda i: (i, 0, 0)),
        out_shape=jax.ShapeDtypeStruct((t, SUBLANES, LANES), F32),
        compiler_params=_params("parallel"),
        name="token_major",
    )(x2)
    return out.reshape(t * SUBLANES, LANES)


def _gather_kernel(cur_ref, next_ref, x_hbm, o_ref, buf, sem, *, cap):
    g = pl.program_id(0)
    last = pl.num_programs(0) - 1
    slot = lax.rem(g, 2)

    def issue_all(idx_ref, buf_slot):
        def issue8(i, carry):
            base = pl.multiple_of(i * 8, 8)
            for j in range(8):
                src = pl.ds(pl.multiple_of(idx_ref[0, 0, base + j] * SUBLANES, SUBLANES), SUBLANES)
                dst = pl.ds(pl.multiple_of((base + j) * SUBLANES, SUBLANES), SUBLANES)
                pltpu.make_async_copy(x_hbm.at[src, :], buf.at[buf_slot, dst, :], sem.at[buf_slot]).start()
            return carry

        lax.fori_loop(0, cap // 8, issue8, 0)

    @pl.when(g == 0)
    def _():
        issue_all(cur_ref, 0)

    @pl.when(g < last)
    def _():
        issue_all(next_ref, 1 - slot)

    pltpu.make_async_copy(x_hbm.at[pl.ds(0, cap * SUBLANES), :], buf.at[slot], sem.at[slot]).wait()
    for s in range(SUBLANES):
        o_ref[:, s * LANES:(s + 1) * LANES] = buf[slot, pl.ds(s, cap, stride=SUBLANES), :].astype(o_ref.dtype)


def _gather_rows(x_tm, rows, cap):
    g = rows.shape[0]
    d = SUBLANES * LANES
    smem = lambda fn: pl.BlockSpec((1, 1, cap), fn, memory_space=pltpu.SMEM)
    return pl.pallas_call(
        functools.partial(_gather_kernel, cap=cap),
        grid=(g,),
        in_specs=[smem(lambda i: (i, 0, 0)), smem(lambda i: (jnp.minimum(i + 1, g - 1), 0, 0)),
                  pl.BlockSpec(memory_space=pl.ANY)],
        out_specs=pl.BlockSpec((cap, d), lambda i: (i, 0)),
        out_shape=jax.ShapeDtypeStruct((g * cap, d), BF16),
        scratch_shapes=[pltpu.VMEM((2, cap * SUBLANES, LANES), F32), pltpu.SemaphoreType.DMA((2,))],
        compiler_params=_params("arbitrary"),
        name="gather_rows",
    )(rows, rows, x_tm)


COMBINE_LN_ROWS = 512


def _combine_kernel(idx_ref, y_ref, x_hbm, g_ref, b_ref, o_hbm, acc_ref, xbuf, obuf, sem_x, sem_o, *, cap, seq):
    b = pl.program_id(0)
    e = pl.program_id(1)

    @pl.when(e == 0)
    def _():
        acc_ref[...] = jnp.zeros_like(acc_ref)

    def step(g, carry):
        r0 = pl.multiple_of(g * SUBLANES, SUBLANES)
        toks = [pl.multiple_of(idx_ref[0, 0, r0 + j] * SUBLANES, SUBLANES) for j in range(SUBLANES)]
        tiles = [acc_ref[pl.ds(toks[j], SUBLANES), :] + y_ref[0, 0, g, pl.ds(j, SUBLANES, stride=SUBLANES), :]
                 for j in range(SUBLANES)]
        for j in range(SUBLANES):
            acc_ref[pl.ds(toks[j], SUBLANES), :] = tiles[j]
        return carry

    lax.fori_loop(0, cap // SUBLANES, step, 0)

    @pl.when(e == pl.num_programs(1) - 1)
    def _():
        tl = COMBINE_LN_ROWS
        n_tiles = seq // tl
        row0 = b * seq

        def x_copy(i, slot):
            return pltpu.make_async_copy(x_hbm.at[pl.ds(pl.multiple_of(row0 + i * tl, tl), tl), :], xbuf.at[slot],
                                         sem_x.at[slot])

        def o_copy(i, slot):
            return pltpu.make_async_copy(obuf.at[slot], o_hbm.at[pl.ds(pl.multiple_of(row0 + i * tl, tl), tl), :],
                                         sem_o.at[slot])

        x_copy(0, 0).start()

        def tile(i, carry):
            slot = lax.rem(i, 2)

            @pl.when(i + 1 < n_tiles)
            def _():
                x_copy(i + 1, 1 - slot).start()

            x_copy(i, slot).wait()

            @pl.when(i >= 2)
            def _():
                o_copy(i - 2, slot).wait()

            base = pl.multiple_of(i * tl * SUBLANES, tl * SUBLANES)
            h = jnp.concatenate([acc_ref[pl.ds(base + s, tl, stride=SUBLANES), :] for s in range(SUBLANES)], axis=1)
            obuf[slot] = _layer_norm(ALPHA * xbuf[slot] + h, g_ref[...], b_ref[...])
            o_copy(i, slot).start()
            return carry

        lax.fori_loop(0, n_tiles, tile, 0)
        o_copy(n_tiles - 2, (n_tiles - 2) % 2).wait()
        o_copy(n_tiles - 1, (n_tiles - 1) % 2).wait()


def _combine_ln(y, idx, x2, g, b, seq):
    batch, ne, ng = y.shape[:3]
    cap = ng * SUBLANES
    d = x2.shape[1]
    assert d == SUBLANES * LANES and seq // COMBINE_LN_ROWS >= 2
    vec = pl.BlockSpec((1, d), lambda bb, ee: (0, 0))
    return pl.pallas_call(
        functools.partial(_combine_kernel, cap=cap, seq=seq),
        grid=(batch, ne),
        in_specs=[
            pl.BlockSpec((1, 1, cap), lambda bb, ee: (bb * ne + ee, 0, 0), memory_space=pltpu.SMEM),
            pl.BlockSpec((1, 1) + y.shape[2:], lambda bb, ee: (bb, ee, 0, 0, 0)),
            pl.BlockSpec(memory_space=pl.ANY), vec, vec,
        ],
        out_specs=pl.BlockSpec(memory_space=pl.ANY),
        out_shape=jax.ShapeDtypeStruct((batch * seq, d), F32),
        scratch_shapes=[pltpu.VMEM((seq * SUBLANES, LANES), F32), pltpu.VMEM((2, COMBINE_LN_ROWS, d), F32),
                        pltpu.VMEM((2, COMBINE_LN_ROWS, d), F32),
                        pltpu.SemaphoreType.DMA((2,)), pltpu.SemaphoreType.DMA((2,))],
        compiler_params=_params("arbitrary", "arbitrary"),
        name="combine_ln",
    )(idx.reshape(batch * ne, 1, cap), y, x2, g, b)


def _moe_layer(x2, w_router, weights, g, b, batch, seq):
    d = x2.shape[1]
    cap = EC_FACTOR * seq // N_EXPERTS
    aff_t = _router(x2, w_router.astype(F32).T, batch, seq)
    idx, gate = _route(aff_t, cap)
    idx = idx.reshape(batch, N_EXPERTS, cap)
    rows = idx + (jnp.arange(batch, dtype=idx.dtype) * seq)[:, None, None]
    xg = _gather_rows(_token_major(x2), rows.reshape(batch * N_EXPERTS, 1, cap), cap)
    xg = xg.reshape(batch, N_EXPERTS, cap, d)
    y = _experts(xg, weights, gate)
    return _combine_ln(y, idx, x2, g, b, seq)


def _lambda_init(layer_number):
    return 0.8 - 0.6 * float(np.exp(-0.3 * (layer_number - 1)))


def _row(v):
    return v.astype(F32).reshape(1, -1)


def kernel(x, w_in_ab, rpb_a, lambda_qk, subln_g, w_out_ab, w_in_c, ln_v_g, ln_v_b, w_s, b_s,
           w_out_c, ln_mix_g, ln_mix_b, w_router, w_gate, w_up, w_down, ln_ffn_g, ln_ffn_b):
    batch, seq, d = x.shape
    x2 = x.astype(F32).reshape(batch * seq, d)
    v_a = slice(2 * WIDTH_A, 3 * WIDTH_A)
    v_b = slice(3 * WIDTH_A + 2 * WIDTH_B, 3 * WIDTH_A + 3 * WIDTH_B)
    col_scale = np.ones((1, 2 * WIDTH_A + 2 * WIDTH_B), np.float32)
    col_scale[:, :WIDTH_A] = QK_SCALE * LOG2E
    col_scale[:, 2 * WIDTH_A:2 * WIDTH_A + WIDTH_B] = QK_SCALE * LOG2E
    col_scale = jnp.asarray(col_scale)

    for l in range(DEPTH):
        i = l // 2
        if l % 2 == 0:
            lam_init = _lambda_init(l + 1)
            lf = lambda_qk[i].astype(F32)
            lam = jnp.exp(jnp.sum(lf[0] * lf[1])) - jnp.exp(jnp.sum(lf[2] * lf[3])) + lam_init
            w_in = w_in_ab[i].astype(BF16)
            qk = _project(x2, jnp.concatenate([w_in[:, :v_a.start], w_in[:, v_a.stop:v_b.start]], axis=1), col_scale)
            vt = _project_t(x2, jnp.concatenate([w_in[:, v_a], w_in[:, v_b]], axis=1).T)
            oa = _natten(qk, vt, _natten_bias(rpb_a[i], seq // GRID_W), batch, seq)
            ob = _diff_attention(qk, vt, lam.reshape(1), subln_g[i], lam_init, batch, seq)
            x2 = _attn_out(oa, ob, w_out_ab[i].astype(BF16), x2, _row(ln_mix_g[l]), _row(ln_mix_b[l]))
        else:
            gw = w_in_c.shape[2] // 2 // GMLP_GROUPS
            bs_full = jnp.repeat(b_s[i].astype(F32).T, gw, axis=1)
            x2 = _gmlp_layer(x2, w_in_c[i].astype(BF16), _row(ln_v_g[i]), _row(ln_v_b[i]),
                             w_s[i].astype(BF16), bs_full, w_out_c[i].astype(BF16),
                             _row(ln_mix_g[l]), _row(ln_mix_b[l]))
        x2 = _moe_layer(x2, w_router[l], (w_gate, w_up, w_down, l),
                        _row(ln_ffn_g[l]), _row(ln_ffn_b[l]), batch, seq)
    return x2.reshape(batch, seq, d).astype(x.dtype)
```

```python
import functools

import jax
import jax.numpy as jnp
import numpy as np
from jax import lax
from jax.experimental import pallas as pl
from jax.experimental.pallas import tpu as pltpu

F32 = jnp.float32
BF16 = jnp.bfloat16

DEPTH = 4
GRID_W = 64
HEAD_DIM = 64
N_HEADS_A = 8
WIN_ROWS = 8
WIN_COLS = 16
WIDTH_A = N_HEADS_A * HEAD_DIM
N_HEADS_B = 4
WIDTH_B = N_HEADS_B * 2 * HEAD_DIM
CHUNK = 128
GMLP_GROUPS = 8
N_EXPERTS = 16
EC_FACTOR = 2
LN_EPS = 1e-5
RMS_EPS = 1e-6
ALPHA = (2.0 * DEPTH) ** 0.25
QK_SCALE = HEAD_DIM ** -0.5
MASK_VALUE = -1e30
LOG2E = float(np.log2(np.e))
SKIP_LOG2 = 48.0
SKIP_MARGIN = 1.02

LANES = 128
SUBLANES = 8
VMEM_LIMIT = 56 * 1024 * 1024

NT_DIMS = (((1,), (1,)), ((), ()))


def _params(*sem):
    return pltpu.CompilerParams(dimension_semantics=sem, vmem_limit_bytes=VMEM_LIMIT)


def _layer_norm(y, g, b):
    mu = jnp.mean(y, axis=-1, keepdims=True)
    yc = y - mu
    var = jnp.mean(yc * yc, axis=-1, keepdims=True)
    return yc * lax.rsqrt(var + LN_EPS) * g + b


def _bf16_split(value, parts=3):
    out, rest = [], np.float64(value)
    for _ in range(parts):
        piece = float(np.asarray(rest, np.float32).astype(jnp.bfloat16).astype(np.float64))
        out.append(piece)
        rest = rest - piece
    return out


def _proj_kernel(x_ref, w_ref, s_ref, o_ref):
    acc = jnp.dot(x_ref[...].astype(BF16), w_ref[...], preferred_element_type=F32)
    o_ref[...] = (acc * s_ref[...]).astype(o_ref.dtype)


def _project(x2, w_bf16, col_scale, tm=1024, tn=512):
    t, k = x2.shape
    n = w_bf16.shape[1]
    return pl.pallas_call(
        _proj_kernel,
        grid=(t // tm, n // tn),
        in_specs=[
            pl.BlockSpec((tm, k), lambda i, j: (i, 0)),
            pl.BlockSpec((k, tn), lambda i, j: (0, j)),
            pl.BlockSpec((1, tn), lambda i, j: (0, j)),
        ],
        out_specs=pl.BlockSpec((tm, tn), lambda i, j: (i, j)),
        out_shape=jax.ShapeDtypeStruct((t, n), BF16),
        compiler_params=_params("parallel", "arbitrary"),
        name="qkv_proj",
    )(x2, w_bf16, col_scale)


def _proj_t_kernel(x_ref, wt_ref, o_ref):
    acc = lax.dot_general(wt_ref[...], x_ref[...].astype(BF16), NT_DIMS, preferred_element_type=F32)
    o_ref[...] = acc.astype(o_ref.dtype)


def _project_t(x2, wt_bf16, tm=1024):
    t, k = x2.shape
    n = wt_bf16.shape[0]
    return pl.pallas_call(
        _proj_t_kernel,
        grid=(t // tm,),
        in_specs=[
            pl.BlockSpec((tm, k), lambda i: (i, 0)),
            pl.BlockSpec((n, k), lambda i: (0, 0)),
        ],
        out_specs=pl.BlockSpec((n, tm), lambda i: (0, i)),
        out_shape=jax.ShapeDtypeStruct((n, t), BF16),
        compiler_params=_params("parallel"),
        name="v_proj_t",
    )(x2, wt_bf16)


NAT_ROWS = 4
NAT_KROWS = NAT_ROWS + WIN_ROWS


def _natten_bias(rpb, n_rows):
    n_heads = rpb.shape[0]
    span = 2 * WIN_COLS - 1
    pad_lo = GRID_W - WIN_COLS
    padded = jnp.pad(rpb.astype(F32) * LOG2E, ((0, 0), (0, 0), (pad_lo, 2 * GRID_W - 1 - pad_lo - span)))
    toep = jnp.stack([padded[:, :, GRID_W - 1 - c:2 * GRID_W - 1 - c] for c in range(GRID_W)], axis=3)
    c = np.arange(GRID_W)[None, :]
    kc = np.arange(GRID_W)[:, None]
    c_start = np.clip(c - WIN_COLS // 2, 0, GRID_W - WIN_COLS)
    valid = (kc >= c_start) & (kc < c_start + WIN_COLS)
    toep = jnp.where(valid[None, None], toep, MASK_VALUE)
    masked = jnp.full((n_heads, GRID_W, GRID_W), MASK_VALUE, F32)
    n_blk = n_rows // NAT_ROWS
    kinds = []
    for blk in (0, 1, n_blk - 1):
        r0 = blk * NAT_ROWS
        ks = int(np.clip(r0 - WIN_ROWS // 2, 0, n_rows - NAT_KROWS))
        cols = []
        for a in range(NAT_ROWS):
            r = r0 + a
            rs = int(np.clip(r - WIN_ROWS // 2, 0, n_rows - WIN_ROWS))
            rows = []
            for i in range(NAT_KROWS):
                kr = ks + i
                rows.append(toep[:, kr - r + WIN_ROWS - 1] if rs <= kr < rs + WIN_ROWS else masked)
            cols.append(jnp.concatenate(rows, axis=1))
        kinds.append(jnp.concatenate(cols, axis=2))
    b = jnp.stack(kinds, axis=0)
    nk, nq = b.shape[2], b.shape[3]
    b = b.reshape(3, n_heads // 2, 2, nk, nq).transpose(0, 1, 3, 2, 4)
    return b.reshape(3, n_heads // 2, nk, 2 * nq)


def _natten_kernel(q_ref, k_ref, vt_ref, bias_ref, o_ref, raw_ref, cmax_ref, *, n_rows):
    nq = NAT_ROWS * GRID_W
    nk = NAT_KROWS * GRID_W
    n_blk = n_rows // NAT_ROWS
    lane = lax.broadcasted_iota(jnp.int32, (nq, LANES), 1)
    low = lane < HEAD_DIM

    def key_start(blk):
        ks = jnp.clip(blk * NAT_ROWS - WIN_ROWS // 2, 0, n_rows - NAT_KROWS)
        return pl.multiple_of(ks * GRID_W, NAT_ROWS * GRID_W)

    def scores(blk, slot):
        q = q_ref[pl.ds(pl.multiple_of(blk * nq, nq), nq), :]
        zero = jnp.zeros_like(q)
        qx = jnp.concatenate([jnp.where(low, q, zero), jnp.where(low, zero, q)], axis=0)
        k = k_ref[pl.ds(key_start(blk), nk), :]
        kind = jnp.where(blk == 0, 0, jnp.where(blk == n_blk - 1, 2, 1))
        raw = lax.dot_general(k, qx, NT_DIMS, preferred_element_type=F32) + bias_ref[kind, 0]
        raw_ref[slot] = raw
        cmax_ref[slot] = jnp.max(raw, axis=0, keepdims=True)

    def absorb(blk, slot):
        vt = vt_ref[:, pl.ds(key_start(blk), nk)]
        p = jnp.exp2(raw_ref[slot] - cmax_ref[slot])
        l = jnp.sum(p, axis=0, keepdims=True)
        res = jnp.dot(vt, p.astype(BF16), preferred_element_type=F32) / l
        o_t = jnp.concatenate([res[:HEAD_DIM, :nq], res[HEAD_DIM:, nq:]], axis=0)
        o_ref[pl.ds(pl.multiple_of(blk * nq, nq), nq), :] = o_t.T.astype(o_ref.dtype)

    scores(0, 0)

    def body(u, carry):
        blk = 2 * u
        scores(blk + 1, 1)
        absorb(blk, 0)
        scores(blk + 2, 0)
        absorb(blk + 1, 1)
        return carry

    assert n_blk % 2 == 0
    lax.fori_loop(0, n_blk // 2 - 1, body, 0)
    scores(n_blk - 1, 1)
    absorb(n_blk - 2, 0)
    absorb(n_blk - 1, 1)


def _natten(qk, vt, bias, batch, seq):
    n_rows = seq // GRID_W
    hp = N_HEADS_A // 2
    nq = NAT_ROWS * GRID_W
    nk = NAT_KROWS * GRID_W
    return pl.pallas_call(
        functools.partial(_natten_kernel, n_rows=n_rows),
        grid=(batch, hp),
        in_specs=[
            pl.BlockSpec((seq, LANES), lambda b, h: (b, h)),
            pl.BlockSpec((seq, LANES), lambda b, h: (b, hp + h)),
            pl.BlockSpec((LANES, seq), lambda b, h: (h, b)),
            pl.BlockSpec((3, 1, nk, 2 * nq), lambda b, h: (0, h, 0, 0)),
        ],
        out_specs=pl.BlockSpec((seq, LANES), lambda b, h: (b, h)),
        out_shape=jax.ShapeDtypeStruct((batch * seq, WIDTH_A), BF16),
        scratch_shapes=[pltpu.VMEM((2, nk, 2 * nq), F32), pltpu.VMEM((2, 1, 2 * nq), F32)],
        compiler_params=_params("parallel", "parallel"),
        name="natten",
    )(qk, qk, vt, bias)


def _diff_kernel(sl_ref, lam_ref, q_ref, k_ref, vt_ref, f_ref, pen_ref, g_ref, o_ref,
                 qop_ref, raw_ref, cmax_ref, coff_ref, acc_ref, knorm_ref, *, tq, tk, out_scale):
    h = pl.program_id(1)
    qi = pl.program_id(2)
    n_kt = k_ref.shape[0] // tk
    assert n_kt % 2 == 0
    sl = sl_ref[h]
    a1, a2, a3 = _bf16_split(LOG2E)

    q = q_ref[...]
    lane_q = lax.broadcasted_iota(jnp.int32, q.shape, 1)
    low_q = lane_q < HEAD_DIM
    zero = jnp.zeros_like(q)
    q1 = jnp.where(low_q, q, zero)
    q2 = jnp.where(low_q, zero, q)

    def q_ops(sign):
        if sign == 0:
            f = zero
        else:
            f = jnp.where(lane_q == 0, sign * a1,
                          jnp.where(lane_q == 1, sign * a2, jnp.where(lane_q == 2, sign * a3, 0.0))).astype(BF16)
        return jnp.concatenate([jnp.concatenate([q1, f], axis=1), jnp.concatenate([q2, f], axis=1)], axis=0)

    qop_ref[0] = q_ops(1.0)
    qop_ref[1] = q_ops(-1.0)
    fk = f_ref[0]
    pos = lax.broadcasted_iota(jnp.int32, (1, 2 * tq), 1)
    iq = (qi * tq + jnp.where(pos < tq, pos, pos - tq)).astype(F32)

    kt_mix = (qi * tq) // tk
    sel = qi - kt_mix * (tk // tq)

    @pl.when(qi == 0)
    def _():
        def tile_norm(kt, best):
            kk = k_ref[pl.ds(pl.multiple_of(kt * tk, tk), tk), :].astype(F32)
            kk = kk * kk
            low_k = lax.broadcasted_iota(jnp.int32, kk.shape, 1) < HEAD_DIM
            n1 = jnp.sum(jnp.where(low_k, kk, 0.0), axis=1, keepdims=True)
            n2 = jnp.sum(jnp.where(low_k, 0.0, kk), axis=1, keepdims=True)
            return jnp.maximum(best, jnp.max(jnp.maximum(n1, n2), axis=0, keepdims=True))

        best = lax.fori_loop(0, n_kt, tile_norm, jnp.zeros((1, 1), F32))
        knorm_ref[...] = jnp.sqrt(best)

    qq = q.astype(F32)
    qq = qq * qq
    qn1 = jnp.sum(jnp.where(low_q, qq, 0.0), axis=1, keepdims=True)
    qn2 = jnp.sum(jnp.where(low_q, 0.0, qq), axis=1, keepdims=True)
    qnorm = jnp.sqrt(jnp.max(jnp.maximum(qn1, qn2), axis=0, keepdims=True))
    reach = (SKIP_LOG2 + 2.0 * SKIP_MARGIN * qnorm * knorm_ref[...]) / sl
    q_lo = (qi * tq).astype(F32)
    kt_lo = jnp.maximum(jnp.floor((q_lo - reach - 1.0) / tk), 0.0).astype(jnp.int32)[0, 0]
    kt_hi = jnp.minimum(jnp.floor((q_lo + (tq + reach)) / tk), n_kt - 1.0).astype(jnp.int32)[0, 0]
    odd = lax.rem(kt_hi - kt_lo + 1, 2)
    widen_lo = jnp.where(kt_lo > 0, odd, 0)
    kt_lo = kt_lo - widen_lo
    kt_hi = kt_hi + (odd - widen_lo)
    n_steps = kt_hi - kt_lo + 1

    def load_k(kt):
        off = pl.multiple_of(kt * tk, tk)
        return jnp.concatenate([k_ref[pl.ds(off, tk), :], fk], axis=1), vt_ref[:, pl.ds(off, tk)]

    def tile_of(s):
        t = kt_lo + jnp.minimum(s, n_steps - 1) - 1
        return jnp.where(s == 0, kt_mix, t + (t >= kt_mix).astype(jnp.int32))

    def scores(s, slot):
        kt = tile_of(s)
        kx, _ = load_k(kt)
        raw = lax.dot_general(kx, qop_ref[(kt > kt_mix).astype(jnp.int32)], NT_DIMS, preferred_element_type=F32)
        c = -sl * jnp.abs(iq - (kt * tk + tk // 2).astype(F32))
        raw_ref[slot] = raw
        cmax_ref[slot] = jnp.max(raw, axis=0, keepdims=True) + c
        coff_ref[slot] = c

    def absorb(s, slot, state):
        m_prev, l_prev = state
        _, vt = load_k(tile_of(s))
        m_new = jnp.maximum(m_prev, cmax_ref[slot])
        a = jnp.exp2(m_prev - m_new)
        p = jnp.exp2(raw_ref[slot] - (m_new - coff_ref[slot]))
        l_new = a * l_prev + jnp.sum(p, axis=0, keepdims=True)
        acc_ref[...] = a * acc_ref[...] + jnp.dot(vt, p.astype(BF16), preferred_element_type=F32)
        return m_new, l_new

    acc_ref[...] = jnp.zeros_like(acc_ref)
    state = (jnp.full((1, 2 * tq), -jnp.inf, F32), jnp.zeros((1, 2 * tq), F32))

    kx, _ = load_k(kt_mix)
    pen = pen_ref[0, sel]
    raw = lax.dot_general(kx, q_ops(0), NT_DIMS, preferred_element_type=F32) - jnp.concatenate([pen, pen], axis=1)
    raw_ref[0] = raw
    cmax_ref[0] = jnp.max(raw, axis=0, keepdims=True)
    coff_ref[0] = jnp.zeros((1, 2 * tq), F32)

    def pair(s, st):
        scores(s + 1, 1)
        st = absorb(s, 0, st)
        scores(s + 2, 0)
        return absorb(s + 1, 1, st)

    n_quads = n_steps // 4
    state = lax.fori_loop(0, n_quads, lambda u, st: pair(4 * u + 2, pair(4 * u, st)), state)
    _, l_run = lax.fori_loop(2 * n_quads, n_steps // 2, lambda u, st: pair(2 * u, st), state)

    on = acc_ref[...] / l_run
    o = (on[:, :tq] - lam_ref[0] * on[:, tq:]).T
    o = o * lax.rsqrt(jnp.mean(o * o, axis=-1, keepdims=True) + RMS_EPS)
    o_ref[...] = (o * g_ref[0] * out_scale).astype(o_ref.dtype)


def _diff_tables(slopes, tq, tk):
    j = np.arange(tk)
    feat = slopes[:, None] * (j - tk // 2)[None, :]
    feat = np.broadcast_to(feat[:, :, None], (len(slopes), tk, LANES))
    i = np.arange(tq)
    pos = np.arange(tk // tq)[:, None, None] * tq + i[None, None, :]
    dist = np.abs(pos - j[None, :, None])
    pen = (slopes * LOG2E)[:, None, None, None] * dist[None]
    return jnp.asarray(feat, BF16), jnp.asarray(pen, F32)


def _diff_attention(qk, vt, lam, subln_g, lam_init, batch, seq, tq=512, tk=512):
    nq = seq // tq
    base = 2 * WIDTH_A // LANES
    hb = N_HEADS_B
    slopes = 2.0 ** (-8.0 * (np.arange(hb) + 1) / hb)
    feat, pen = _diff_tables(slopes, tq, tk)
    sl = jnp.asarray(slopes * LOG2E, F32)
    smem = pl.BlockSpec(memory_space=pltpu.SMEM)
    return pl.pallas_call(
        functools.partial(_diff_kernel, tq=tq, tk=tk, out_scale=1.0 - lam_init),
        grid=(batch, hb, nq),
        in_specs=[
            smem, smem,
            pl.BlockSpec((tq, LANES), lambda b, h, i: (b * nq + i, base + h)),
            pl.BlockSpec((seq, LANES), lambda b, h, i: (b, base + hb + h)),
            pl.BlockSpec((LANES, seq), lambda b, h, i: (WIDTH_A // LANES + h, b)),
            pl.BlockSpec((1, tk, LANES), lambda b, h, i: (h, 0, 0)),
            pl.BlockSpec((1, tk // tq, tk, tq), lambda b, h, i: (h, 0, 0, 0)),
            pl.BlockSpec((1, 1, LANES), lambda b, h, i: (h, 0, 0)),
        ],
        out_specs=pl.BlockSpec((tq, LANES), lambda b, h, i: (b * nq + i, h)),
        out_shape=jax.ShapeDtypeStruct((batch * seq, WIDTH_B), BF16),
        scratch_shapes=[
            pltpu.VMEM((2, 2 * tq, 2 * LANES), BF16),
            pltpu.VMEM((2, tk, 2 * tq), F32),
            pltpu.VMEM((2, 1, 2 * tq), F32),
            pltpu.VMEM((2, 1, 2 * tq), F32),
            pltpu.VMEM((LANES, 2 * tq), F32),
            pltpu.VMEM((1, 1), F32),
        ],
        compiler_params=_params("arbitrary", "arbitrary", "arbitrary"),
        name="diff_attn",
    )(sl, lam, qk, qk, vt, feat, pen, subln_g.astype(F32).reshape(hb, 1, LANES))


def _store_both(y, o_ref, tm_ref):
    o_ref[...] = y
    for s in range(y.shape[1] // LANES):
        tm_ref[:, s, :] = y[:, s * LANES:(s + 1) * LANES]


def _both_outputs(t, d, tm):
    assert d == SUBLANES * LANES
    specs = (pl.BlockSpec((tm, d), lambda i: (i, 0)), pl.BlockSpec((tm, SUBLANES, LANES), lambda i: (i, 0, 0)))
    shapes = (jax.ShapeDtypeStruct((t, d), F32), jax.ShapeDtypeStruct((t, SUBLANES, LANES), F32))
    return specs, shapes


def _attn_out_kernel(oa_ref, ob_ref, wa_ref, wb_ref, x_ref, g_ref, b_ref, o_ref, tm_ref):
    h = jnp.dot(oa_ref[...], wa_ref[...], preferred_element_type=F32)
    h = h + jnp.dot(ob_ref[...], wb_ref[...], preferred_element_type=F32)
    _store_both(_layer_norm(ALPHA * x_ref[...] + h, g_ref[...], b_ref[...]), o_ref, tm_ref)


def _attn_out(oa, ob, w_out_bf16, x2, g, b, tm=512):
    t, d = x2.shape
    wa, wb = w_out_bf16[:WIDTH_A], w_out_bf16[WIDTH_A:]
    row = lambda w: pl.BlockSpec((tm, w), lambda i: (i, 0))
    full = lambda a: pl.BlockSpec(a.shape, lambda i: (0,) * a.ndim)
    out_specs, out_shape = _both_outputs(t, d, tm)
    return pl.pallas_call(
        _attn_out_kernel,
        grid=(t // tm,),
        in_specs=[row(WIDTH_A), row(WIDTH_B), full(wa), full(wb), row(d), full(g), full(b)],
        out_specs=out_specs,
        out_shape=out_shape,
        compiler_params=_params("parallel"),
        name="attn_out_ln",
    )(oa, ob, wa, wb, x2, g, b)


def _gmlp_kernel(x_ref, win_ref, lg_ref, lb_ref, ws_ref, bs_ref, wout_ref, g_ref, b_ref, o_ref, tm_ref, *, tm):
    x = x_ref[...]
    width = wout_ref.shape[0]
    gw = width // GMLP_GROUPS
    z = jnp.dot(x.astype(BF16), win_ref[...], preferred_element_type=F32)
    z = 0.5 * z * (1.0 + lax.erf(z * (2.0 ** -0.5)))
    u = z[:, :width]
    v = _layer_norm(z[:, width:], lg_ref[...], lb_ref[...]).astype(BF16)
    chunks = []
    for n in range(tm // CHUNK):
        groups = []
        for gi in range(GMLP_GROUPS):
            vg = v[n * CHUNK:(n + 1) * CHUNK, gi * gw:(gi + 1) * gw]
            groups.append(jnp.dot(ws_ref[gi], vg, preferred_element_type=F32))
        chunks.append(jnp.concatenate(groups, axis=1) + bs_ref[...])
    sv = jnp.concatenate(chunks, axis=0)
    h = jnp.dot((u * sv).astype(BF16), wout_ref[...], preferred_element_type=F32)
    _store_both(_layer_norm(ALPHA * x + h, g_ref[...], b_ref[...]), o_ref, tm_ref)


def _gmlp_layer(x2, w_in, ln_g, ln_b, w_s, bs_full, w_out, g, b, tm=256):
    t, d = x2.shape
    row = pl.BlockSpec((tm, d), lambda i: (i, 0))
    full = lambda a: pl.BlockSpec(a.shape, lambda i: (0,) * a.ndim)
    args = (w_in, ln_g, ln_b, w_s, bs_full, w_out, g, b)
    out_specs, out_shape = _both_outputs(t, d, tm)
    return pl.pallas_call(
        functools.partial(_gmlp_kernel, tm=tm),
        grid=(t // tm,),
        in_specs=[row] + [full(a) for a in args],
        out_specs=out_specs,
        out_shape=out_shape,
        compiler_params=_params("parallel"),
        name="gmlp_layer",
    )(x2, *args)


def _router_kernel(x_ref, wr_ref, o_ref):
    logits = lax.dot_general(wr_ref[...], x_ref[...], NT_DIMS, preferred_element_type=F32,
                             precision=lax.Precision.HIGHEST)
    m = jnp.max(logits, axis=0, keepdims=True)
    p = jnp.exp(logits - m)
    o_ref[0] = p / jnp.sum(p, axis=0, keepdims=True)


def _router(x2, w_router_t, batch, seq, tm=1024):
    d = x2.shape[1]
    ns = seq // tm
    return pl.pallas_call(
        _router_kernel,
        grid=(batch, ns),
        in_specs=[
            pl.BlockSpec((tm, d), lambda b, i: (b * ns + i, 0)),
            pl.BlockSpec((N_EXPERTS, d), lambda b, i: (0, 0)),
        ],
        out_specs=pl.BlockSpec((1, N_EXPERTS, tm), lambda b, i: (b, 0, i)),
        out_shape=jax.ShapeDtypeStruct((batch, N_EXPERTS, seq), F32),
        compiler_params=_params("parallel", "parallel"),
        name="router",
    )(x2, w_router_t)


def _route_kernel(aff_ref, idx_ref, gate_ref, thr_ref, *, cap):
    ne, nt = aff_ref.shape[1], aff_ref.shape[2]
    a_all = aff_ref[0]
    u_all = pltpu.bitcast(a_all, jnp.int32)

    def add_bit(i, thr):
        cand = thr | jnp.left_shift(1, 30 - i)
        count = jnp.sum(jnp.where(u_all >= cand, 1.0, 0.0), axis=(1, 2), keepdims=True)
        return jnp.where(count >= cap, cand, thr)

    thr_ref[...] = lax.fori_loop(0, 31, add_bit, jnp.zeros((ne, 1, 1), jnp.int32))

    lane = lax.broadcasted_iota(jnp.int32, (LANES, LANES), 1)
    sub = lax.broadcasted_iota(jnp.int32, (LANES, LANES), 0)
    tri_incl = jnp.where(sub <= lane, 1.0, 0.0).astype(BF16)
    r64 = lax.broadcasted_iota(jnp.int32, (nt, nt), 0)
    c64 = lax.broadcasted_iota(jnp.int32, (nt, nt), 1)
    low_strict = jnp.where(c64 < r64, 1.0, 0.0).astype(BF16)
    up_incl = jnp.where(r64 <= c64, 1.0, 0.0).astype(BF16)
    ones_rows = jnp.ones((8, LANES), BF16)
    slot = lax.broadcasted_iota(jnp.int32, (cap, 1), 0).astype(F32)
    lane_nt = lax.broadcasted_iota(jnp.int32, (cap, nt), 1)
    lane_f = lax.broadcasted_iota(jnp.int32, (cap, LANES), 1).astype(F32)

    def cumsum_tokens(mask_f32):
        mb = mask_f32.astype(BF16)
        local = jnp.dot(mb, tri_incl, preferred_element_type=F32)
        n_b = jnp.broadcast_to(jnp.sum(mask_f32, axis=1, keepdims=True), mask_f32.shape)
        before = jnp.dot(low_strict, n_b.astype(BF16), preferred_element_type=F32)
        return before + local, local

    def one_expert(e, carry):
        a = aff_ref[0, e]
        u = pltpu.bitcast(a, jnp.int32)
        t = thr_ref[e]
        gt = u > t
        eq = u == t
        need = cap - jnp.sum(jnp.where(gt, 1.0, 0.0), axis=(0, 1), keepdims=True)
        eq_f = jnp.where(eq, 1.0, 0.0)
        eq_incl, _ = cumsum_tokens(eq_f)
        sel = jnp.logical_or(gt, jnp.logical_and(eq, eq_incl - eq_f < need))
        sel_f = jnp.where(sel, 1.0, 0.0)
        _, local = cumsum_tokens(sel_f)
        n_row = lax.dot_general(ones_rows, sel_f.astype(BF16), NT_DIMS, preferred_element_type=F32)[:1]
        incl_row = jnp.dot(jnp.broadcast_to(n_row, (8, nt)).astype(BF16), up_incl, preferred_element_type=F32)[:1]
        excl_row = incl_row - n_row
        krow = jnp.sum(jnp.where(incl_row <= slot, 1.0, 0.0), axis=1, keepdims=True)
        onehot = jnp.where(lane_nt == krow.astype(jnp.int32), 1.0, 0.0)
        rank = slot - jnp.sum(onehot * excl_row, axis=1, keepdims=True)
        row_counts = jnp.dot(onehot.astype(BF16), local.astype(BF16), preferred_element_type=F32)
        pos = jnp.sum(jnp.where(row_counts <= rank, 1.0, 0.0), axis=1, keepdims=True)
        idx_ref[0, e] = (krow * LANES + pos).astype(jnp.int32)
        row_aff = jnp.dot(onehot, a, preferred_element_type=F32, precision=lax.Precision.HIGHEST)
        gate_ref[0, e] = jnp.sum(jnp.where(lane_f == pos, row_aff, 0.0), axis=1, keepdims=True)
        return carry

    lax.fori_loop(0, ne, one_expert, 0)


def _route(aff_t, cap):
    batch, ne, seq = aff_t.shape
    nt = seq // LANES
    out = jax.ShapeDtypeStruct((batch, ne, cap, 1), jnp.int32), jax.ShapeDtypeStruct((batch, ne, cap, 1), F32)
    spec = pl.BlockSpec((1, ne, cap, 1), lambda b: (b, 0, 0, 0))
    return pl.pallas_call(
        functools.partial(_route_kernel, cap=cap),
        grid=(batch,),
        in_specs=[pl.BlockSpec((1, ne, nt, LANES), lambda b: (b, 0, 0, 0))],
        out_specs=(spec, spec),
        out_shape=out,
        scratch_shapes=[pltpu.VMEM((ne, 1, 1), jnp.int32)],
        compiler_params=_params("parallel"),
        name="route",
    )(aff_t.reshape(batch, ne, nt, LANES))


def _expert_kernel(x_ref, wg_ref, wu_ref, wd_ref, gate_ref, o_ref):
    f = pl.program_id(2)
    x = x_ref[0, 0]
    a = jnp.dot(x, wg_ref[0, 0].astype(BF16), preferred_element_type=F32)
    u = jnp.dot(x, wu_ref[0, 0].astype(BF16), preferred_element_type=F32)
    hmid = (a * jax.nn.sigmoid(a) * u).astype(BF16)
    y = jnp.dot(hmid, wd_ref[0, 0].astype(BF16), preferred_element_type=F32)
    cap = y.shape[0]
    last = pl.num_programs(2) - 1

    def put(update):
        for s in range(y.shape[1] // LANES):
            dst = (0, 0, slice(None), slice(s * SUBLANES, (s + 1) * SUBLANES), slice(None))
            o_ref[dst] = update(o_ref[dst], y[:, s * LANES:(s + 1) * LANES].reshape(cap // SUBLANES, SUBLANES, LANES))

    @pl.when(f == 0)
    def _():
        put(lambda old, new: new)

    @pl.when(jnp.logical_and(f > 0, f < last))
    def _():
        put(lambda old, new: old + new)

    @pl.when(jnp.logical_and(f > 0, f == last))
    def _():
        gate = gate_ref[0, 0].reshape(cap // SUBLANES, SUBLANES, 1)
        put(lambda old, new: (old + new) * gate)


def _experts(xg, weights, gate, tf=1024):
    wg, wu, wd, layer = weights
    batch, ne, cap, d = xg.shape
    dff = wg.shape[3]
    assert dff // tf >= 2
    out_tail = (cap // SUBLANES, SUBLANES * (d // LANES), LANES)
    return pl.pallas_call(
        _expert_kernel,
        grid=(ne, batch, dff // tf),
        in_specs=[
            pl.BlockSpec((1, 1, cap, d), lambda e, b, f: (b, e, 0, 0)),
            pl.BlockSpec((1, 1, d, tf), lambda e, b, f: (layer, e, 0, f)),
            pl.BlockSpec((1, 1, d, tf), lambda e, b, f: (layer, e, 0, f)),
            pl.BlockSpec((1, 1, tf, d), lambda e, b, f: (layer, e, f, 0)),
            pl.BlockSpec((1, 1, cap, 1), lambda e, b, f: (b, e, 0, 0)),
        ],
        out_specs=pl.BlockSpec((1, 1) + out_tail, lambda e, b, f: (b, e, 0, 0, 0)),
        out_shape=jax.ShapeDtypeStruct((batch, ne) + out_tail, F32),
        compiler_params=_params("parallel", "parallel", "arbitrary"),
        name="experts",
    )(xg, wg, wu, wd, gate)


def _gather_kernel(cur_ref, next_ref, x_hbm, o_ref, buf, sem, *, cap):
    g = pl.program_id(0)
    last = pl.num_programs(0) - 1
    slot = lax.rem(g, 2)

    def issue_all(idx_ref, buf_slot):
        def issue8(i, carry):
            base = pl.multiple_of(i * 8, 8)
            for j in range(8):
                src = pl.ds(pl.multiple_of(idx_ref[0, 0, base + j] * SUBLANES, SUBLANES), SUBLANES)
                dst = pl.ds(pl.multiple_of((base + j) * SUBLANES, SUBLANES), SUBLANES)
                pltpu.make_async_copy(x_hbm.at[src, :], buf.at[buf_slot, dst, :], sem.at[buf_slot]).start()
            return carry

        lax.fori_loop(0, cap // 8, issue8, 0)

    @pl.when(g == 0)
    def _():
        issue_all(cur_ref, 0)

    @pl.when(g < last)
    def _():
        issue_all(next_ref, 1 - slot)

    pltpu.make_async_copy(x_hbm.at[pl.ds(0, cap * SUBLANES), :], buf.at[slot], sem.at[slot]).wait()
    for s in range(SUBLANES):
        o_ref[:, s * LANES:(s + 1) * LANES] = buf[slot, pl.ds(s, cap, stride=SUBLANES), :].astype(o_ref.dtype)


def _gather_rows(x_tm, rows, cap):
    g = rows.shape[0]
    d = SUBLANES * LANES
    smem = lambda fn: pl.BlockSpec((1, 1, cap), fn, memory_space=pltpu.SMEM)
    return pl.pallas_call(
        functools.partial(_gather_kernel, cap=cap),
        grid=(g,),
        in_specs=[smem(lambda i: (i, 0, 0)), smem(lambda i: (jnp.minimum(i + 1, g - 1), 0, 0)),
                  pl.BlockSpec(memory_space=pl.ANY)],
        out_specs=pl.BlockSpec((cap, d), lambda i: (i, 0)),
        out_shape=jax.ShapeDtypeStruct((g * cap, d), BF16),
        scratch_shapes=[pltpu.VMEM((2, cap * SUBLANES, LANES), F32), pltpu.SemaphoreType.DMA((2,))],
        compiler_params=_params("arbitrary"),
        name="gather_rows",
    )(rows, rows, x_tm)


COMBINE_LN_ROWS = 512


def _combine_kernel(idx_ref, y_ref, x_hbm, g_ref, b_ref, o_hbm, acc_ref, xbuf, obuf, sem_x, sem_o, *, cap, seq):
    b = pl.program_id(0)
    e = pl.program_id(1)

    @pl.when(e == 0)
    def _():
        acc_ref[...] = jnp.zeros_like(acc_ref)

    def step(g, carry):
        r0 = pl.multiple_of(g * SUBLANES, SUBLANES)
        toks = [pl.multiple_of(idx_ref[0, 0, r0 + j] * SUBLANES, SUBLANES) for j in range(SUBLANES)]
        tiles = [acc_ref[pl.ds(toks[j], SUBLANES), :] + y_ref[0, 0, g, pl.ds(j, SUBLANES, stride=SUBLANES), :]
                 for j in range(SUBLANES)]
        for j in range(SUBLANES):
            acc_ref[pl.ds(toks[j], SUBLANES), :] = tiles[j]
        return carry

    lax.fori_loop(0, cap // SUBLANES, step, 0)

    @pl.when(e == pl.num_programs(1) - 1)
    def _():
        tl = COMBINE_LN_ROWS
        n_tiles = seq // tl
        row0 = b * seq

        def x_copy(i, slot):
            return pltpu.make_async_copy(x_hbm.at[pl.ds(pl.multiple_of(row0 + i * tl, tl), tl), :], xbuf.at[slot],
                                         sem_x.at[slot])

        def o_copy(i, slot):
            return pltpu.make_async_copy(obuf.at[slot], o_hbm.at[pl.ds(pl.multiple_of(row0 + i * tl, tl), tl), :],
                                         sem_o.at[slot])

        x_copy(0, 0).start()

        def tile(i, carry):
            slot = lax.rem(i, 2)

            @pl.when(i + 1 < n_tiles)
            def _():
                x_copy(i + 1, 1 - slot).start()

            x_copy(i, slot).wait()

            @pl.when(i >= 2)
            def _():
                o_copy(i - 2, slot).wait()

            base = pl.multiple_of(i * tl * SUBLANES, tl * SUBLANES)
            h = jnp.concatenate([acc_ref[pl.ds(base + s, tl, stride=SUBLANES), :] for s in range(SUBLANES)], axis=1)
            obuf[slot] = _layer_norm(ALPHA * xbuf[slot] + h, g_ref[...], b_ref[...])
            o_copy(i, slot).start()
            return carry

        lax.fori_loop(0, n_tiles, tile, 0)
        o_copy(n_tiles - 2, (n_tiles - 2) % 2).wait()
        o_copy(n_tiles - 1, (n_tiles - 1) % 2).wait()


def _combine_ln(y, idx, x2, g, b, seq):
    batch, ne, ng = y.shape[:3]
    cap = ng * SUBLANES
    d = x2.shape[1]
    assert d == SUBLANES * LANES and seq // COMBINE_LN_ROWS >= 2
    vec = pl.BlockSpec((1, d), lambda bb, ee: (0, 0))
    return pl.pallas_call(
        functools.partial(_combine_kernel, cap=cap, seq=seq),
        grid=(batch, ne),
        in_specs=[
            pl.BlockSpec((1, 1, cap), lambda bb, ee: (bb * ne + ee, 0, 0), memory_space=pltpu.SMEM),
            pl.BlockSpec((1, 1) + y.shape[2:], lambda bb, ee: (bb, ee, 0, 0, 0)),
            pl.BlockSpec(memory_space=pl.ANY), vec, vec,
        ],
        out_specs=pl.BlockSpec(memory_space=pl.ANY),
        out_shape=jax.ShapeDtypeStruct((batch * seq, d), F32),
        scratch_shapes=[pltpu.VMEM((seq * SUBLANES, LANES), F32), pltpu.VMEM((2, COMBINE_LN_ROWS, d), F32),
                        pltpu.VMEM((2, COMBINE_LN_ROWS, d), F32),
                        pltpu.SemaphoreType.DMA((2,)), pltpu.SemaphoreType.DMA((2,))],
        compiler_params=_params("arbitrary", "arbitrary"),
        name="combine_ln",
    )(idx.reshape(batch * ne, 1, cap), y, x2, g, b)


def _moe_layer(x2, x_tm, w_router, weights, g, b, batch, seq):
    d = x2.shape[1]
    cap = EC_FACTOR * seq // N_EXPERTS
    aff_t = _router(x2, w_router.astype(F32).T, batch, seq)
    idx, gate = _route(aff_t, cap)
    idx = idx.reshape(batch, N_EXPERTS, cap)
    rows = idx + (jnp.arange(batch, dtype=idx.dtype) * seq)[:, None, None]
    xg = _gather_rows(x_tm.reshape(-1, LANES), rows.reshape(batch * N_EXPERTS, 1, cap), cap)
    xg = xg.reshape(batch, N_EXPERTS, cap, d)
    y = _experts(xg, weights, gate)
    return _combine_ln(y, idx, x2, g, b, seq)


def _lambda_init(layer_number):
    return 0.8 - 0.6 * float(np.exp(-0.3 * (layer_number - 1)))


def _row(v):
    return v.astype(F32).reshape(1, -1)


def kernel(x, w_in_ab, rpb_a, lambda_qk, subln_g, w_out_ab, w_in_c, ln_v_g, ln_v_b, w_s, b_s,
           w_out_c, ln_mix_g, ln_mix_b, w_router, w_gate, w_up, w_down, ln_ffn_g, ln_ffn_b):
    batch, seq, d = x.shape
    x2 = x.astype(F32).reshape(batch * seq, d)
    v_a = slice(2 * WIDTH_A, 3 * WIDTH_A)
    v_b = slice(3 * WIDTH_A + 2 * WIDTH_B, 3 * WIDTH_A + 3 * WIDTH_B)
    col_scale = np.ones((1, 2 * WIDTH_A + 2 * WIDTH_B), np.float32)
    col_scale[:, :WIDTH_A] = QK_SCALE * LOG2E
    col_scale[:, 2 * WIDTH_A:2 * WIDTH_A + WIDTH_B] = QK_SCALE * LOG2E
    col_scale = jnp.asarray(col_scale)

    for l in range(DEPTH):
        i = l // 2
        if l % 2 == 0:
            lam_init = _lambda_init(l + 1)
            lf = lambda_qk[i].astype(F32)
            lam = jnp.exp(jnp.sum(lf[0] * lf[1])) - jnp.exp(jnp.sum(lf[2] * lf[3])) + lam_init
            w_in = w_in_ab[i].astype(BF16)
            qk = _project(x2, jnp.concatenate([w_in[:, :v_a.start], w_in[:, v_a.stop:v_b.start]], axis=1), col_scale)
            vt = _project_t(x2, jnp.concatenate([w_in[:, v_a], w_in[:, v_b]], axis=1).T)
            oa = _natten(qk, vt, _natten_bias(rpb_a[i], seq // GRID_W), batch, seq)
            ob = _diff_attention(qk, vt, lam.reshape(1), subln_g[i], lam_init, batch, seq)
            x2, x_tm = _attn_out(oa, ob, w_out_ab[i].astype(BF16), x2, _row(ln_mix_g[l]), _row(ln_mix_b[l]))
        else:
            gw = w_in_c.shape[2] // 2 // GMLP_GROUPS
            bs_full = jnp.repeat(b_s[i].astype(F32).T, gw, axis=1)
            x2, x_tm = _gmlp_layer(x2, w_in_c[i].astype(BF16), _row(ln_v_g[i]), _row(ln_v_b[i]),
                                   w_s[i].astype(BF16), bs_full, w_out_c[i].astype(BF16),
                                   _row(ln_mix_g[l]), _row(ln_mix_b[l]))
        x2 = _moe_layer(x2, x_tm, w_router[l], (w_gate, w_up, w_down, l),
                        _row(ln_ffn_g[l]), _row(ln_ffn_b[l]), batch, seq)
    return x2.reshape(batch, seq, d).astype(x.dtype)
```
